```python
import math
import jax
import jax.numpy as jnp
from jax import lax
import numpy as np

D_MODEL = 2048
BATCH = 2
SEQ = 8192
DEPTH = 4

GRID_W = 64
CTX_LEN = 256

NA_HEADS = 4
NA_DIM = 128
NA_WIN_H = 8
NA_WIN_W = 16
DA_HEADS = 4
DA_DIM = 64
DA_VDIM = 2 * DA_DIM
ROPE_BASE = 10000.0
Q_BLOCK = 128
POOL_GROUPS = 4
POOL_WIDTH = 128
POOL_WINDOWS = (2, 4, 8, 16)
SG_GROUPS = 4
SG_WIDTH = 128
SG_CHUNK = 128

NA_W = NA_HEADS * NA_DIM
DA_QK_W = DA_HEADS * 2 * DA_DIM
DA_V_W = DA_HEADS * DA_VDIM
POOL_W = POOL_GROUPS * POOL_WIDTH
SG_W = SG_GROUPS * SG_WIDTH
IN_SPLITS = (NA_W, NA_W, NA_W, DA_QK_W, DA_QK_W, DA_V_W, POOL_W, SG_W, SG_W)
D_IN = 4608
D_MIX = 2048

N_EXPERTS = 16
N_EXPERT_GROUPS = 4
EXPERTS_PER_GROUP = 4
TOP_K = 2
D_EXPERT = 1024
MOE_BLOCK = 256

N_MOD = 6
EPS = 1e-6
NEG_INF = -1e30

kernel_name = 'hybrid_na_diffattn_pool_sgmlp_moe_dit'


def rms_norm(x, g):
    xf = x.astype(jnp.float32)
    y = xf * lax.rsqrt(jnp.mean(xf * xf, axis=-1, keepdims=True) + EPS)
    return (y * g.astype(jnp.float32)).astype(x.dtype)


def modulate(h, shift, scale):
    return h * (1 + scale) + shift


def to_heads(t, n_heads):
    b, n, w = t.shape
    return t.reshape(b, n, n_heads, w // n_heads).transpose(0, 2, 1, 3)


def from_heads(t):
    b, h, n, d = t.shape
    return t.transpose(0, 2, 1, 3).reshape(b, n, h * d)


def to_qk_pairs(t):
    b, n, _ = t.shape
    return t.reshape(b, n, DA_HEADS, 2, DA_DIM).transpose(0, 2, 3, 1, 4)


def axial_rope(n):
    t = jnp.arange(n)
    row = (t // GRID_W).astype(jnp.float32)
    col = (t % GRID_W).astype(jnp.float32)
    n_freq = DA_DIM // 4
    inv_freq = 1.0 / (ROPE_BASE ** (jnp.arange(n_freq, dtype=jnp.float32) / n_freq))
    ang = jnp.concatenate([row[:, None] * inv_freq, col[:, None] * inv_freq], axis=-1)
    return jnp.cos(ang), jnp.sin(ang)


def apply_rope(x, cos, sin):
    half = x.shape[-1] // 2
    x1 = x[..., :half].astype(jnp.float32)
    x2 = x[..., half:].astype(jnp.float32)
    return jnp.concatenate([x1 * cos - x2 * sin, x1 * sin + x2 * cos], axis=-1).astype(x.dtype)


def softmax_attend(q, k, v):
    s = jnp.einsum('bhqd,bhkd->bhqk', q, k).astype(jnp.float32) * (q.shape[-1] ** -0.5)
    return jnp.einsum('bhqk,bhkd->bhqd', jax.nn.softmax(s, axis=-1).astype(v.dtype), v)


def neighbourhood_attention(q, k, v, k_ctx, v_ctx, rpb, rows):
    b, h, s, d = q.shape
    kh = min(NA_WIN_H, rows)
    r = jnp.arange(rows)
    row_start = jnp.clip(r - kh // 2, 0, rows - kh)
    row_idx = row_start[:, None] + jnp.arange(kh)
    col = jnp.arange(GRID_W)
    col_start = jnp.clip(col - NA_WIN_W // 2, 0, GRID_W - NA_WIN_W)
    col_mask = (col[None, :] >= col_start[:, None]) & (col[None, :] < col_start[:, None] + NA_WIN_W)
    dr = row_idx - r[:, None] + (NA_WIN_H - 1)
    dc = jnp.clip(col[None, :] - col[:, None], -(NA_WIN_W - 1), NA_WIN_W - 1) + (NA_WIN_W - 1)
    bias = rpb.astype(jnp.float32)[:, dr[:, None, :, None], dc[None, :, None, :]]
    qg = q.reshape(b, h, rows, GRID_W, d)
    kg = k.reshape(b, h, rows, GRID_W, d)[:, :, row_idx]
    vg = v.reshape(b, h, rows, GRID_W, d)[:, :, row_idx]
    scale = d ** -0.5
    s_loc = jnp.einsum('bhrqd,bhrikd->bhrqik', qg, kg).astype(jnp.float32) * scale + bias[None]
    s_loc = jnp.where(col_mask[:, None, :], s_loc, NEG_INF)
    s_ctx = jnp.einsum('bhrqd,bhcd->bhrqc', qg, k_ctx).astype(jnp.float32) * scale
    n_loc = kh * GRID_W
    p = jax.nn.softmax(jnp.concatenate([s_loc.reshape(b, h, rows, GRID_W, n_loc), s_ctx], axis=-1), axis=-1)
    p = p.astype(v.dtype)
    p_loc = p[..., :n_loc].reshape(b, h, rows, GRID_W, kh, GRID_W)
    o = jnp.einsum('bhrqik,bhrikd->bhrqd', p_loc, vg) + jnp.einsum('bhrqc,bhcd->bhrqd', p[..., n_loc:], v_ctx)
    return o.reshape(b, h, s, d)


def diff_attend(q, k, v, lam):
    s = jnp.einsum('bhmqd,bhmkd->bhmqk', q, k).astype(jnp.float32) * (q.shape[-1] ** -0.5)
    p = jax.nn.softmax(s, axis=-1)
    w = p[:, :, 0] - lam * p[:, :, 1]
    return jnp.einsum('bhqk,bhkd->bhqd', w.astype(v.dtype), v)


def diff_attention_latent(q, k, v, k_ctx, v_ctx, lam):
    b, h, _, n, d = q.shape
    keys = jnp.concatenate([k, k_ctx], axis=3)
    vals = jnp.concatenate([v, v_ctx], axis=2)
    n_blocks = n // Q_BLOCK
    q_blocks = q.reshape(b, h, 2, n_blocks, Q_BLOCK, d).transpose(3, 0, 1, 2, 4, 5)
    out = lax.map(lambda qb: diff_attend(qb, keys, vals, lam), q_blocks)
    return out.transpose(1, 2, 0, 3, 4).reshape(b, h, n, vals.shape[-1])


def multiscale_pool(t, w_pool, scale):
    b, n, _ = t.shape
    tf = t.astype(jnp.float32).reshape(b, n, POOL_GROUPS, POOL_WIDTH)
    csum = jnp.concatenate([jnp.zeros((b, 1, POOL_GROUPS, POOL_WIDTH), jnp.float32), jnp.cumsum(tf, axis=1)], axis=1)
    half = jnp.array(POOL_WINDOWS, jnp.int32) // 2
    pos = jnp.arange(n, dtype=jnp.int32)[:, None]
    lo = jnp.clip(pos - half, 0, n)
    hi = jnp.clip(pos + half, 0, n)
    grp = jnp.arange(POOL_GROUPS)
    mean = (csum[:, hi, grp] - csum[:, lo, grp]) / (hi - lo).astype(jnp.float32)[None, :, :, None]
    resid = (mean - tf).astype(t.dtype)
    y = jnp.einsum('bngc,gcd->bngd', resid, w_pool).reshape(b, n, POOL_W)
    return y * scale


def spatial_gating(u, v, g_v, w_s, b_s):
    b, n, _ = u.shape
    u = jax.nn.gelu(u)
    v = rms_norm(jax.nn.gelu(v), g_v)
    vc = v.reshape(b, n // SG_CHUNK, SG_CHUNK, SG_GROUPS, SG_WIDTH)
    mixed = jnp.einsum('gts,bcsgw->bctgw', w_s, vc) + b_s.T[None, None, :, :, None]
    return u * mixed.reshape(b, n, SG_W)


def token_mixers(px, pc, rows, cos, sin, layer, rpb, lam_vecs, subln_g, pool_w, pool_scale,
                 sg_norm_g, sg_w, sg_b, with_ctx_out):
    cuts = [int(v) for v in np.cumsum(IN_SPLITS)[:-1]]
    qa_x, ka_x, va_x, qd_x, kd_x, vd_x, pl_x, su_x, sv_x = jnp.split(px, cuts, axis=-1)
    qa_c, ka_c, va_c, qd_c, kd_c, vd_c, pl_c, su_c, sv_c = jnp.split(pc, cuts, axis=-1)
    ka_ch, va_ch = to_heads(ka_c, NA_HEADS), to_heads(va_c, NA_HEADS)
    oa_x = neighbourhood_attention(to_heads(qa_x, NA_HEADS), to_heads(ka_x, NA_HEADS),
                                   to_heads(va_x, NA_HEADS), ka_ch, va_ch, rpb, rows)
    lam_init = 0.8 - 0.6 * math.exp(-0.3 * layer)
    lq1, lk1, lq2, lk2 = lam_vecs.astype(jnp.float32)
    lam = jnp.exp(jnp.sum(lq1 * lk1)) - jnp.exp(jnp.sum(lq2 * lk2)) + lam_init
    kd_cp, vd_ch = to_qk_pairs(kd_c), to_heads(vd_c, DA_HEADS)
    od_x = diff_attention_latent(apply_rope(to_qk_pairs(qd_x), cos, sin), apply_rope(to_qk_pairs(kd_x), cos, sin),
                                 to_heads(vd_x, DA_HEADS), kd_cp, vd_ch, lam)
    od_x = rms_norm(od_x, subln_g) * (1 - lam_init)
    mix_x = jnp.concatenate([from_heads(oa_x), from_heads(od_x),
                             multiscale_pool(pl_x, pool_w, pool_scale),
                             spatial_gating(su_x, sv_x, sg_norm_g, sg_w, sg_b)], axis=-1)
    if not with_ctx_out:
        return mix_x, None
    oa_c = softmax_attend(to_heads(qa_c, NA_HEADS), ka_ch, va_ch)
    od_c = rms_norm(diff_attend(to_qk_pairs(qd_c), kd_cp, vd_ch, lam), subln_g) * (1 - lam_init)
    mix_c = jnp.concatenate([from_heads(oa_c), from_heads(od_c),
                             multiscale_pool(pl_c, pool_w, pool_scale),
                             spatial_gating(su_c, sv_c, sg_norm_g, sg_w, sg_b)], axis=-1)
    return mix_x, mix_c


def swiglu(xb, wg, wu, wd):
    return (jax.nn.silu(xb @ wg) * (xb @ wu)) @ wd


def moe_ffn(h, router_w, router_b, w_gate, w_up, w_down):
    lead = h.shape[:-1]
    d = h.shape[-1]
    tok = h.reshape(-1, d)
    n = tok.shape[0]
    scores = jax.nn.sigmoid((tok @ router_w).astype(jnp.float32))
    biased = (scores + router_b.astype(jnp.float32)).reshape(n, N_EXPERT_GROUPS, EXPERTS_PER_GROUP)
    group_score = lax.top_k(biased, 2)[0].sum(-1)
    group = jnp.argmax(group_score, axis=-1)
    in_group = jnp.take_along_axis(biased, group[:, None, None], axis=1)[:, 0]
    _, local = lax.top_k(in_group, TOP_K)
    expert = group[:, None] * EXPERTS_PER_GROUP + local
    gate = jnp.take_along_axis(scores, expert, axis=1)
    gate = gate / jnp.sum(gate, axis=-1, keepdims=True)
    flat = expert.reshape(-1)
    order = jnp.argsort(flat)
    xs = tok[order // TOP_K]
    xs = jnp.concatenate([xs, jnp.zeros((MOE_BLOCK, d), xs.dtype)], axis=0)
    sizes = jnp.bincount(flat, length=N_EXPERTS)
    ends = jnp.cumsum(sizes)
    starts = ends - sizes
    ys = jnp.zeros_like(xs)
    for e in range(N_EXPERTS):
        def body(i, acc, e=e):
            row0 = starts[e] + i * MOE_BLOCK
            xb = lax.dynamic_slice_in_dim(xs, row0, MOE_BLOCK, axis=0)
            yb = swiglu(xb, w_gate[e], w_up[e], w_down[e])
            valid = (row0 + jnp.arange(MOE_BLOCK)) < ends[e]
            cur = lax.dynamic_slice_in_dim(acc, row0, MOE_BLOCK, axis=0)
            return lax.dynamic_update_slice_in_dim(acc, jnp.where(valid[:, None], yb, cur), row0, axis=0)
        n_blocks = (sizes[e] + MOE_BLOCK - 1) // MOE_BLOCK
        ys = lax.fori_loop(0, n_blocks, body, ys)
    y_sorted = ys[: n * TOP_K]
    y = jnp.zeros_like(y_sorted).at[order].set(y_sorted).reshape(n, TOP_K, d)
    out = jnp.einsum('nk,nkd->nd', gate.astype(y.dtype), y)
    return out.reshape(*lead, d)


def setup_inputs(seed: int = 0) -> dict:
    key = jax.random.key(seed)
    ks = jax.random.split(key, 24)
    f32 = jnp.float32

    def nrm(k, shape, scale):
        return jax.random.normal(k, shape, f32) * scale

    def gain(k, shape):
        return 1.0 + 0.02 * jax.random.normal(k, shape, f32)

    return {
        'x': nrm(ks[0], (BATCH, SEQ, D_MODEL), 1.0),
        'c': nrm(ks[1], (BATCH, D_MODEL), 1.0),
        'ctx': nrm(ks[2], (BATCH, CTX_LEN, D_MODEL), 1.0),
        'c_ctx': nrm(ks[3], (D_MODEL,), 1.0),
        'w_ada': nrm(ks[4], (DEPTH, D_MODEL, N_MOD * D_MODEL), 0.5 * D_MODEL ** -0.5),
        'b_ada': nrm(ks[5], (DEPTH, N_MOD * D_MODEL), 0.02),
        'g_norm1': gain(ks[6], (DEPTH, D_MODEL)),
        'g_norm2': gain(ks[7], (DEPTH, D_MODEL)),
        'w_in': nrm(ks[8], (DEPTH, D_MODEL, D_IN), D_MODEL ** -0.5),
        'w_out': nrm(ks[9], (DEPTH, D_MIX, D_MODEL), D_MIX ** -0.5),
        'na_rpb': nrm(ks[10], (DEPTH, NA_HEADS, 2 * NA_WIN_H - 1, 2 * NA_WIN_W - 1), 0.1),
        'da_lambda': nrm(ks[11], (DEPTH, 4, DA_DIM), 0.1),
        'da_subln_g': gain(ks[12], (DEPTH, DA_VDIM)),
        'pool_w': nrm(ks[13], (DEPTH, POOL_GROUPS, POOL_WIDTH, POOL_WIDTH), POOL_WIDTH ** -0.5),
        'pool_scale': gain(ks[14], (DEPTH, POOL_W)),
        'sg_norm_g': gain(ks[15], (DEPTH, SG_W)),
        'sg_w': nrm(ks[16], (DEPTH, SG_GROUPS, SG_CHUNK, SG_CHUNK), SG_CHUNK ** -0.5),
        'sg_b': gain(ks[17], (DEPTH, SG_GROUPS, SG_CHUNK)),
        'router_w': nrm(ks[18], (D_MODEL, N_EXPERTS), D_MODEL ** -0.5),
        'router_b': nrm(ks[19], (N_EXPERTS,), 0.01),
        'exp_w_gate': nrm(ks[20], (DEPTH, N_EXPERTS, D_MODEL, D_EXPERT), D_MODEL ** -0.5),
        'exp_w_up': nrm(ks[21], (DEPTH, N_EXPERTS, D_MODEL, D_EXPERT), D_MODEL ** -0.5),
        'exp_w_down': nrm(ks[22], (DEPTH, N_EXPERTS, D_EXPERT, D_MODEL), D_EXPERT ** -0.5),
        'g_final': gain(ks[23], (D_MODEL,)),
    }


def reference(x, c, ctx, c_ctx, w_ada, b_ada, g_norm1, g_norm2, w_in, w_out,
              na_rpb, da_lambda, da_subln_g, pool_w, pool_scale, sg_norm_g, sg_w, sg_b,
              router_w, router_b, exp_w_gate, exp_w_up, exp_w_down, g_final):
    seq = x.shape[1]
    n_ctx = ctx.shape[1]
    rows = seq // GRID_W
    cos, sin = axial_rope(seq)
    cond_x = jax.nn.silu(c)
    cond_c = jax.nn.silu(c_ctx)
    h_ctx = ctx
    for l in range(DEPTH):
        last = l == DEPTH - 1
        mod_x = jnp.split((cond_x @ w_ada[l] + b_ada[l])[:, None, :], N_MOD, axis=-1)
        mod_c = jnp.split((cond_c @ w_ada[l] + b_ada[l])[None, None, :], N_MOD, axis=-1)
        hx = modulate(rms_norm(x, g_norm1[l]), mod_x[0], mod_x[1])
        hc = modulate(rms_norm(h_ctx, g_norm1[l]), mod_c[0], mod_c[1])
        mix_x, mix_c = token_mixers(hx @ w_in[l], hc @ w_in[l], rows, cos, sin, l, na_rpb[l], da_lambda[l],
                                    da_subln_g[l], pool_w[l], pool_scale[l], sg_norm_g[l], sg_w[l], sg_b[l],
                                    not last)
        x = x + mod_x[2] * (mix_x @ w_out[l])
        hx = modulate(rms_norm(x, g_norm2[l]), mod_x[3], mod_x[4])
        if last:
            x = x + mod_x[5] * moe_ffn(hx, router_w, router_b, exp_w_gate[l], exp_w_up[l], exp_w_down[l])
        else:
            h_ctx = h_ctx + mod_c[2] * (mix_c @ w_out[l])
            hc = modulate(rms_norm(h_ctx, g_norm2[l]), mod_c[3], mod_c[4])
            y = moe_ffn(jnp.concatenate([hc, hx], axis=1), router_w, router_b,
                        exp_w_gate[l], exp_w_up[l], exp_w_down[l])
            h_ctx = h_ctx + mod_c[5] * y[:, :n_ctx]
            x = x + mod_x[5] * y[:, n_ctx:]
    return rms_norm(x, g_final)
```

```python
import functools
import math

import numpy as np
import jax
import jax.numpy as jnp
from jax import lax
from jax.experimental import pallas as pl
from jax.experimental.pallas import tpu as pltpu

GRID_W = 64
NA_HEADS = 4
NA_DIM = 128
NA_WIN_H = 8
NA_WIN_W = 16
DA_HEADS = 4
DA_DIM = 64
DA_VDIM = 2 * DA_DIM
ROPE_BASE = 10000.0
POOL_GROUPS = 4
POOL_WIDTH = 128
POOL_WINDOWS = (2, 4, 8, 16)
SG_GROUPS = 4
SG_WIDTH = 128
SG_CHUNK = 128
N_EXPERT_GROUPS = 4
EXPERTS_PER_GROUP = 4
TOP_K = 2
N_MOD = 6
EPS = 1e-6
NEG_INF = -1e30

HEAD_W = 128
MIX_W = 512
ATT_COLS = 6 * MIX_W
REST_COLS = 3 * MIX_W
MOE_BLOCK = 256
NA_Q_ROWS = 4
ROW_TILE = 256
VMEM_LIMIT = 56 * 1024 * 1024

F32 = jnp.float32
BF16 = jnp.bfloat16


def _cparams(n_axes, vmem=None):
    return pltpu.CompilerParams(dimension_semantics=("arbitrary",) * n_axes,
                                vmem_limit_bytes=vmem)


def _pick(n, candidates):
    for c in candidates:
        if n % c == 0:
            return c
    raise ValueError(f"no tile in {candidates} divides {n}")


def _ada_kernel(c_ref, w_ref, b_ref, o_ref):
    cs = jax.nn.silu(c_ref[...]).astype(BF16)
    o_ref[...] = jnp.dot(cs, w_ref[...].astype(BF16), preferred_element_type=F32) + b_ref[...]


def _ada_all(cond, w_ada, b_ada):
    depth, d, n6 = w_ada.shape
    tn = _pick(n6, (1024, 512, 256, 128))
    return pl.pallas_call(
        _ada_kernel,
        grid=(depth, n6 // tn),
        in_specs=[pl.BlockSpec((8, d), lambda l, j: (0, 0)),
                  pl.BlockSpec((None, d, tn), lambda l, j: (l, 0, j)),
                  pl.BlockSpec((None, 1, tn), lambda l, j: (l, 0, j))],
        out_specs=pl.BlockSpec((None, 8, tn), lambda l, j: (l, 0, j)),
        out_shape=jax.ShapeDtypeStruct((depth, 8, n6), F32),
        compiler_params=_cparams(2, VMEM_LIMIT),
        name="adaln",
    )(cond, w_ada, b_ada.reshape(depth, 1, n6))


def _row_is_ctx(i, tm, s_len):
    row = i * tm + lax.broadcasted_iota(jnp.int32, (tm, 1), 0)
    return row >= s_len


def _norm_mod(x, g, shx, scx, shc, scc, is_ctx):
    y = x * lax.rsqrt(jnp.mean(x * x, axis=-1, keepdims=True) + EPS) * g
    shift = jnp.where(is_ctx, shc, shx)
    scale = jnp.where(is_ctx, scc, scx)
    return y * (1.0 + scale) + shift


def _norm_mod_kernel(x_ref, g_ref, shx_ref, scx_ref, shc_ref, scc_ref, o_ref, *, s_len, tm):
    is_ctx = _row_is_ctx(pl.program_id(1), tm, s_len)
    h = _norm_mod(x_ref[...], g_ref[...], shx_ref[...], scx_ref[...], shc_ref[...], scc_ref[...], is_ctx)
    o_ref[...] = h.astype(o_ref.dtype)


def _norm_mod_call(xc, g, mod3, s_len, k_shift, k_scale):
    b, n, d = xc.shape
    tm = _pick(n, (ROW_TILE, 128))
    mx = lambda k: pl.BlockSpec((None, 1, d), lambda bi, i: (bi, 0, k))
    mc = lambda k: pl.BlockSpec((None, 1, d), lambda bi, i: (b, 0, k))
    return pl.pallas_call(
        functools.partial(_norm_mod_kernel, s_len=s_len, tm=tm),
        grid=(b, n // tm),
        in_specs=[pl.BlockSpec((None, tm, d), lambda bi, i: (bi, i, 0)),
                  pl.BlockSpec((1, d), lambda bi, i: (0, 0)),
                  mx(k_shift), mx(k_scale), mc(k_shift), mc(k_scale)],
        out_specs=pl.BlockSpec((None, tm, d), lambda bi, i: (bi, i, 0)),
        out_shape=jax.ShapeDtypeStruct((b, n, d), BF16),
        compiler_params=_cparams(2, VMEM_LIMIT),
        name="norm_mod",
    )(xc, g.reshape(1, d), mod3, mod3, mod3, mod3)


def _mm_kernel(a_ref, w_ref, o_ref, wbf_ref):
    @pl.when(pl.program_id(1) == 0)
    def _():
        wbf_ref[...] = w_ref[...].astype(BF16)

    o_ref[...] = jnp.dot(a_ref[...], wbf_ref[...], preferred_element_type=F32).astype(o_ref.dtype)


def _matmul_cols(a, w, layer, col0, ncols, out_dtype, tn):
    m, k = a.shape
    tm = _pick(m, (768, 512, 384, 256, 128))
    off = col0 // tn
    return pl.pallas_call(
        _mm_kernel,
        grid=(ncols // tn, m // tm),
        in_specs=[pl.BlockSpec((tm, k), lambda j, i: (i, 0)),
                  pl.BlockSpec((None, k, tn), lambda j, i: (layer, 0, j + off))],
        out_specs=pl.BlockSpec((tm, tn), lambda j, i: (i, j)),
        out_shape=jax.ShapeDtypeStruct((m, ncols), out_dtype),
        scratch_shapes=[pltpu.VMEM((k, tn), BF16)],
        compiler_params=_cparams(2, VMEM_LIMIT),
        name="in_proj",
    )(a, w)


def _out_proj_kernel(ma_ref, mb_ref, mc_ref, md_ref, w_ref, x_ref, gx_ref, gc_ref, o_ref, wbf_ref,
                     *, s_len, tm):
    @pl.when((pl.program_id(1) == 0) & (pl.program_id(2) == 0))
    def _():
        wbf_ref[...] = w_ref[...].astype(BF16)

    acc = jnp.dot(ma_ref[...], wbf_ref[0 * MIX_W:1 * MIX_W, :], preferred_element_type=F32)
    acc += jnp.dot(mb_ref[...], wbf_ref[1 * MIX_W:2 * MIX_W, :], preferred_element_type=F32)
    acc += jnp.dot(mc_ref[...], wbf_ref[2 * MIX_W:3 * MIX_W, :], preferred_element_type=F32)
    acc += jnp.dot(md_ref[...], wbf_ref[3 * MIX_W:4 * MIX_W, :], preferred_element_type=F32)
    gate = jnp.where(_row_is_ctx(pl.program_id(2), tm, s_len), gc_ref[...], gx_ref[...])
    o_ref[...] = x_ref[...] + gate * acc


def _out_proj(mixes, w_out, layer, xc, mod3, s_len, k_gate):
    b, n, d = xc.shape
    tm = _pick(n, (768, 512, 384, 256, 128))
    tn = _pick(d, (512, 256, 128))
    per = d // tn
    mix_spec = pl.BlockSpec((None, tm, MIX_W), lambda j, bi, i: (bi, i, 0))
    return pl.pallas_call(
        functools.partial(_out_proj_kernel, s_len=s_len, tm=tm),
        grid=(d // tn, b, n // tm),
        in_specs=[mix_spec, mix_spec, mix_spec, mix_spec,
                  pl.BlockSpec((None, 4 * MIX_W, tn), lambda j, bi, i: (layer, 0, j)),
                  pl.BlockSpec((None, tm, tn), lambda j, bi, i: (bi, i, j)),
                  pl.BlockSpec((None, 1, tn), lambda j, bi, i: (bi, 0, k_gate * per + j)),
                  pl.BlockSpec((None, 1, tn), lambda j, bi, i: (b, 0, k_gate * per + j))],
        out_specs=pl.BlockSpec((None, tm, tn), lambda j, bi, i: (bi, i, j)),
        out_shape=jax.ShapeDtypeStruct((b, n, d), F32),
        scratch_shapes=[pltpu.VMEM((4 * MIX_W, tn), BF16)],
        compiler_params=_cparams(3, VMEM_LIMIT),
        name="out_proj",
    )(*mixes, w_out, xc, mod3, mod3)


def _rope_tables(s_len, c_len):
    t = jnp.arange(s_len)
    row = (t // GRID_W).astype(F32)
    col = (t % GRID_W).astype(F32)
    n_freq = DA_DIM // 4
    inv_freq = 1.0 / (ROPE_BASE ** (jnp.arange(n_freq, dtype=F32) / n_freq))
    ang = jnp.concatenate([row[:, None] * inv_freq, col[:, None] * inv_freq], axis=-1)
    cos = jnp.concatenate([jnp.cos(ang), jnp.ones((c_len, DA_DIM // 2), F32)], axis=0)
    sin = jnp.concatenate([jnp.sin(ang), jnp.zeros((c_len, DA_DIM // 2), F32)], axis=0)
    return jnp.tile(cos, (1, 4)), jnp.tile(jnp.concatenate([-sin, sin], axis=-1), (1, 2))


def _rope_kernel(q_ref, k_ref, c_ref, s_ref, o_ref, *, tm, q_scale):
    cos = c_ref[...]
    sin = s_ref[...]
    lane = lax.broadcasted_iota(jnp.int32, (tm, HEAD_W), 1)
    first_half = (lane % DA_DIM) < (DA_DIM // 2)
    for src, base, scale in ((q_ref, 0, q_scale), (k_ref, MIX_W, 1.0)):
        for h in range(DA_HEADS):
            x = src[:, h * HEAD_W:(h + 1) * HEAD_W].astype(F32)
            swapped = jnp.where(first_half, pltpu.roll(x, HEAD_W - DA_DIM // 2, 1),
                                pltpu.roll(x, DA_DIM // 2, 1))
            r = (x * cos + swapped * sin) * scale
            o_ref[:, base + h * HEAD_W:base + (h + 1) * HEAD_W] = r.astype(o_ref.dtype)


def _rope_call(pxa, cos, sin):
    b, n, _ = pxa.shape
    tm = _pick(n, (ROW_TILE, 128))
    return pl.pallas_call(
        functools.partial(_rope_kernel, tm=tm, q_scale=DA_DIM ** -0.5),
        grid=(b, n // tm),
        in_specs=[pl.BlockSpec((None, tm, MIX_W), lambda bi, i: (bi, i, 3)),
                  pl.BlockSpec((None, tm, MIX_W), lambda bi, i: (bi, i, 4)),
                  pl.BlockSpec((tm, HEAD_W), lambda bi, i: (i, 0)),
                  pl.BlockSpec((tm, HEAD_W), lambda bi, i: (i, 0))],
        out_specs=pl.BlockSpec((None, tm, 2 * MIX_W), lambda bi, i: (bi, i, 0)),
        out_shape=jax.ShapeDtypeStruct((b, n, 2 * MIX_W), BF16),
        compiler_params=_cparams(2, VMEM_LIMIT),
        name="rope",
    )(pxa, pxa, cos, sin)


def _na_plan(rows, r_q):
    kh = min(NA_WIN_H, rows)
    key_rows = r_q + kh - 1
    assert rows % r_q == 0 and rows >= key_rows
    patterns, var, ustart = [], [], []
    for j in range(rows // r_q):
        r = r_q * j + np.arange(r_q)
        rs = np.clip(r - kh // 2, 0, rows - kh)
        u = int(np.clip(rs.min(), 0, rows - key_rows))
        kr = u + np.arange(key_rows)
        valid = (kr[None, :] >= rs[:, None]) & (kr[None, :] < rs[:, None] + kh)
        assert (valid.sum(1) == kh).all()
        dr = np.where(valid, kr[None, :] - r[:, None] + (NA_WIN_H - 1), 0)
        key = (valid.tobytes(), dr.tobytes())
        for v, (k2, _, _) in enumerate(patterns):
            if k2 == key:
                break
        else:
            v = len(patterns)
            patterns.append((key, valid, dr))
        var.append(v)
        ustart.append(u * GRID_W)
    valid = np.stack([p[1] for p in patterns])
    dr = np.stack([p[2] for p in patterns])
    return np.asarray(var, np.int32), np.asarray(ustart, np.int32), valid, dr


def _na_bias_tables(rpb, valid, dr):
    col = np.arange(GRID_W)
    cs = np.clip(col - NA_WIN_W // 2, 0, GRID_W - NA_WIN_W)
    cmask = (col[None, :] >= cs[:, None]) & (col[None, :] < cs[:, None] + NA_WIN_W)
    dc = np.clip(col[None, :] - col[:, None], -(NA_WIN_W - 1), NA_WIN_W - 1) + (NA_WIN_W - 1)
    v, r_q, key_rows = valid.shape
    bias = rpb.astype(F32)[:, dr[:, :, None, :, None], dc[None, None, :, None, :]]
    mask = valid[:, :, None, :, None] & cmask[None, None, :, None, :]
    tab = jnp.where(mask[None], bias, NEG_INF)
    return tab.reshape(rpb.shape[0], v, r_q * GRID_W, key_rows * GRID_W)


def _na_kernel(var_ref, u_ref, q_ref, k_ref, v_ref, kc_ref, vc_ref, bias_ref, o_ref, *, n_lat, kw, scale):
    j = pl.program_id(2)
    q = q_ref[...]
    nt = (((1,), (1,)), ((), ()))
    s_ctx = lax.dot_general(q, kc_ref[...], nt, preferred_element_type=F32) * scale

    @pl.when(j < n_lat)
    def _():
        u = pl.multiple_of(u_ref[j], GRID_W)
        k_win = k_ref[pl.ds(u, kw), :]
        v_win = v_ref[pl.ds(u, kw), :]
        s = lax.dot_general(q, k_win, nt, preferred_element_type=F32) * scale + bias_ref[...]
        m = jnp.maximum(jnp.max(s, axis=-1, keepdims=True), jnp.max(s_ctx, axis=-1, keepdims=True))
        p = jnp.exp(s - m)
        pc = jnp.exp(s_ctx - m)
        l = jnp.sum(p, axis=-1, keepdims=True) + jnp.sum(pc, axis=-1, keepdims=True)
        o = jnp.dot(p.astype(BF16), v_win, preferred_element_type=F32)
        o += jnp.dot(pc.astype(BF16), vc_ref[...], preferred_element_type=F32)
        o_ref[...] = (o / l).astype(o_ref.dtype)

    @pl.when(j >= n_lat)
    def _():
        m = jnp.max(s_ctx, axis=-1, keepdims=True)
        pc = jnp.exp(s_ctx - m)
        l = jnp.sum(pc, axis=-1, keepdims=True)
        o = jnp.dot(pc.astype(BF16), vc_ref[...], preferred_element_type=F32)
        o_ref[...] = (o / l).astype(o_ref.dtype)


def _na_call(pxa, rpb, s_len, c_len):
    b, n, _ = pxa.shape
    rows = s_len // GRID_W
    tq = NA_Q_ROWS * GRID_W
    assert s_len % tq == 0 and c_len % tq == 0 and s_len % c_len == 0
    var, ustart, valid, dr = _na_plan(rows, NA_Q_ROWS)
    n_lat, n_q = s_len // tq, n // tq
    kw = valid.shape[2] * GRID_W
    tabs = _na_bias_tables(rpb, valid, dr)
    pad = np.zeros(n_q - n_lat, np.int32)
    var = jnp.asarray(np.concatenate([var, pad]))
    ustart = jnp.asarray(np.concatenate([ustart, pad]))
    ctx_blk = s_len // c_len
    grid_spec = pltpu.PrefetchScalarGridSpec(
        num_scalar_prefetch=2,
        grid=(b, NA_HEADS, n_q),
        in_specs=[pl.BlockSpec((None, tq, HEAD_W), lambda bi, h, j, vr, ur: (bi, j, h)),
                  pl.BlockSpec((None, s_len, HEAD_W), lambda bi, h, j, vr, ur: (bi, 0, 4 + h)),
                  pl.BlockSpec((None, s_len, HEAD_W), lambda bi, h, j, vr, ur: (bi, 0, 8 + h)),
                  pl.BlockSpec((None, c_len, HEAD_W), lambda bi, h, j, vr, ur: (bi, ctx_blk, 4 + h)),
                  pl.BlockSpec((None, c_len, HEAD_W), lambda bi, h, j, vr, ur: (bi, ctx_blk, 8 + h)),
                  pl.BlockSpec((None, None, tq, kw), lambda bi, h, j, vr, ur: (h, vr[j], 0, 0))],
        out_specs=pl.BlockSpec((None, tq, HEAD_W), lambda bi, h, j, vr, ur: (bi, j, h)),
    )
    return pl.pallas_call(
        functools.partial(_na_kernel, n_lat=n_lat, kw=kw, scale=NA_DIM ** -0.5),
        grid_spec=grid_spec,
        out_shape=jax.ShapeDtypeStruct((b, n, MIX_W), BF16),
        compiler_params=_cparams(3, VMEM_LIMIT),
        name="nbr_attn",
    )(var, ustart, pxa, pxa, pxa, pxa, pxa, tabs)


def _da_kernel(q_ref, k_ref, v_ref, lam_ref, g_ref, o_ref, *, s_len, c_len, tq, tk, lam_init):
    qi = pl.program_id(2)
    n = s_len + c_len
    q = q_ref[...]
    lane = lax.broadcasted_iota(jnp.int32, (tq, HEAD_W), 1)
    zero = jnp.zeros_like(q)
    q_maps = (jnp.where(lane < DA_DIM, q, zero), jnp.where(lane >= DA_DIM, q, zero))
    nt = (((1,), (1,)), ((), ()))

    def attend(k_blk, v_blk, carry):
        out = []
        for qm, (m, l, acc) in zip(q_maps, carry):
            s = lax.dot_general(qm, k_blk, nt, preferred_element_type=F32)
            m_new = jnp.maximum(m, jnp.max(s, axis=-1, keepdims=True))
            alpha = jnp.exp(m - m_new)
            p = jnp.exp(s - m_new)
            l_new = alpha * l + jnp.sum(p, axis=-1, keepdims=True)
            acc_new = alpha * acc + jnp.dot(p.astype(BF16), v_blk, preferred_element_type=F32)
            out.append((m_new, l_new, acc_new))
        return tuple(out)

    def init():
        one = (jnp.full((tq, 1), NEG_INF, F32), jnp.zeros((tq, 1), F32), jnp.zeros((tq, DA_VDIM), F32))
        return (one, one)

    def finalize(carry):
        lv = lam_ref[...]
        t1 = jnp.sum(lv[0:1, :] * lv[1:2, :], axis=-1, keepdims=True)
        t2 = jnp.sum(lv[2:3, :] * lv[3:4, :], axis=-1, keepdims=True)
        lam = jnp.exp(t1) - jnp.exp(t2) + lam_init
        (_, l1, a1), (_, l2, a2) = carry
        o = a1 / l1 - lam * (a2 / l2)
        y = o * lax.rsqrt(jnp.mean(o * o, axis=-1, keepdims=True) + EPS) * g_ref[...]
        o_ref[...] = (y * (1.0 - lam_init)).astype(o_ref.dtype)

    @pl.when(qi < s_len // tq)
    def _():
        def body(c, carry):
            start = pl.multiple_of(c * tk, tk)
            return attend(k_ref[pl.ds(start, tk), :], v_ref[pl.ds(start, tk), :], carry)

        finalize(lax.fori_loop(0, n // tk, body, init()))

    @pl.when(qi >= s_len // tq)
    def _():
        finalize(attend(k_ref[s_len:n, :], v_ref[s_len:n, :], init()))


def _da_call(qk_rot, pxa, lam_vecs, subln_g, s_len, c_len, layer):
    b, n, _ = pxa.shape
    tq = _pick(math.gcd(s_len, c_len), (256, 128))
    tk = _pick(n, (768, 512, 256, 128))
    lam_init = 0.8 - 0.6 * math.exp(-0.3 * layer)
    return pl.pallas_call(
        functools.partial(_da_kernel, s_len=s_len, c_len=c_len, tq=tq, tk=tk, lam_init=lam_init),
        grid=(b, DA_HEADS, n // tq),
        in_specs=[pl.BlockSpec((None, tq, HEAD_W), lambda bi, h, i: (bi, i, h)),
                  pl.BlockSpec((None, n, HEAD_W), lambda bi, h, i: (bi, 0, 4 + h)),
                  pl.BlockSpec((None, n, HEAD_W), lambda bi, h, i: (bi, 0, 20 + h)),
                  pl.BlockSpec((4, DA_DIM), lambda bi, h, i: (0, 0)),
                  pl.BlockSpec((1, DA_VDIM), lambda bi, h, i: (0, 0))],
        out_specs=pl.BlockSpec((None, tq, HEAD_W), lambda bi, h, i: (bi, i, h)),
        out_shape=jax.ShapeDtypeStruct((b, n, MIX_W), BF16),
        compiler_params=_cparams(3, VMEM_LIMIT),
        name="diff_attn",
    )(qk_rot, qk_rot, pxa, lam_vecs, subln_g.reshape(1, DA_VDIM))


POOL_HALO = 8


def _pool_kernel(prev_ref, cur_ref, next_ref, w_ref, sc_ref, o_ref, buf_ref, *, s_len, c_len, tm):
    i = pl.program_id(1)
    n_lat = s_len // tm
    n_all = (s_len + c_len) // tm
    seq_start = (i == 0) | (i == n_lat)
    seq_end = (i == n_lat - 1) | (i == n_all - 1)
    buf_ref[0:POOL_HALO, :] = jnp.where(seq_start, 0.0, prev_ref[...])
    buf_ref[POOL_HALO:POOL_HALO + tm, :] = cur_ref[...]
    buf_ref[POOL_HALO + tm:, :] = jnp.where(seq_end, 0.0, next_ref[...])
    in_ctx = i >= n_lat
    seq_len = jnp.where(in_ctx, c_len, s_len)
    pos = i * tm - jnp.where(in_ctx, s_len, 0) + lax.broadcasted_iota(jnp.int32, (tm, 1), 0)
    for g, win in enumerate(POOL_WINDOWS):
        half = win // 2
        cols = slice(g * POOL_WIDTH, (g + 1) * POOL_WIDTH)
        tot = buf_ref[POOL_HALO - half:POOL_HALO - half + tm, cols]
        for d in range(-half + 1, half):
            tot = tot + buf_ref[POOL_HALO + d:POOL_HALO + d + tm, cols]
        cnt = jnp.clip(pos + half, 0, seq_len) - jnp.clip(pos - half, 0, seq_len)
        resid = tot / cnt.astype(F32) - cur_ref[:, cols]
        y = jnp.dot(resid.astype(BF16), w_ref[g].astype(BF16), preferred_element_type=F32)
        o_ref[:, cols] = (y * sc_ref[:, cols]).astype(o_ref.dtype)


def _pool_call(pxb, pool_w, pool_scale, s_len, c_len):
    b, n, _ = pxb.shape
    tm = _pick(math.gcd(s_len, c_len), (ROW_TILE, 128))
    hb = tm // POOL_HALO
    last = n // POOL_HALO - 1
    return pl.pallas_call(
        functools.partial(_pool_kernel, s_len=s_len, c_len=c_len, tm=tm),
        grid=(b, n // tm),
        in_specs=[pl.BlockSpec((None, POOL_HALO, MIX_W), lambda bi, i: (bi, jnp.maximum(i * hb - 1, 0), 0)),
                  pl.BlockSpec((None, tm, MIX_W), lambda bi, i: (bi, i, 0)),
                  pl.BlockSpec((None, POOL_HALO, MIX_W), lambda bi, i: (bi, jnp.minimum((i + 1) * hb, last), 0)),
                  pl.BlockSpec((POOL_GROUPS, POOL_WIDTH, POOL_WIDTH), lambda bi, i: (0, 0, 0)),
                  pl.BlockSpec((1, MIX_W), lambda bi, i: (0, 0))],
        out_specs=pl.BlockSpec((None, tm, MIX_W), lambda bi, i: (bi, i, 0)),
        out_shape=jax.ShapeDtypeStruct((b, n, MIX_W), BF16),
        scratch_shapes=[pltpu.VMEM((tm + 2 * POOL_HALO, MIX_W), F32)],
        compiler_params=_cparams(2, VMEM_LIMIT),
        name="pool",
    )(pxb, pxb, pxb, pool_w, pool_scale.reshape(1, MIX_W))


def _sg_kernel(u_ref, v_ref, g_ref, w_ref, b_ref, o_ref, *, tm):
    v = jax.nn.gelu(v_ref[...])
    vn = (v * lax.rsqrt(jnp.mean(v * v, axis=-1, keepdims=True) + EPS) * g_ref[...]).astype(BF16)
    bias = b_ref[...]
    for c in range(tm // SG_CHUNK):
        rows = slice(c * SG_CHUNK, (c + 1) * SG_CHUNK)
        for g in range(SG_GROUPS):
            cols = slice(g * SG_WIDTH, (g + 1) * SG_WIDTH)
            mixed = jnp.dot(w_ref[g].astype(BF16), vn[rows, cols], preferred_element_type=F32)
            mixed = mixed + bias[:, g:g + 1]
            o_ref[rows, cols] = (jax.nn.gelu(u_ref[rows, cols]) * mixed).astype(o_ref.dtype)


def _sg_call(pxb, sg_norm_g, sg_w, sg_b):
    b, n, _ = pxb.shape
    tm = _pick(n, (ROW_TILE, 128))
    return pl.pallas_call(
        functools.partial(_sg_kernel, tm=tm),
        grid=(b, n // tm),
        in_specs=[pl.BlockSpec((None, tm, MIX_W), lambda bi, i: (bi, i, 1)),
                  pl.BlockSpec((None, tm, MIX_W), lambda bi, i: (bi, i, 2)),
                  pl.BlockSpec((1, MIX_W), lambda bi, i: (0, 0)),
                  pl.BlockSpec((SG_GROUPS, SG_CHUNK, SG_CHUNK), lambda bi, i: (0, 0, 0)),
                  pl.BlockSpec((SG_CHUNK, SG_GROUPS), lambda bi, i: (0, 0))],
        out_specs=pl.BlockSpec((None, tm, MIX_W), lambda bi, i: (bi, i, 0)),
        out_shape=jax.ShapeDtypeStruct((b, n, MIX_W), BF16),
        compiler_params=_cparams(2, VMEM_LIMIT),
        name="spatial_gate",
    )(pxb, pxb, sg_norm_g.reshape(1, MIX_W), sg_w, sg_b.T)


def _first_argmax(vals):
    best, idx = vals[0], jnp.zeros(vals[0].shape, jnp.int32)
    for k in range(1, len(vals)):
        take = vals[k] > best
        best = jnp.where(take, vals[k], best)
        idx = jnp.where(take, k, idx)
    return idx, best


def _router_kernel(x_ref, g_ref, shx_ref, scx_ref, shc_ref, scc_ref, rw_ref, rb_ref,
                   h_ref, ri_ref, rf_ref, cnt_ref, run_ref, *, s_len, tm, n_exp):
    bi, i = pl.program_id(0), pl.program_id(1)

    @pl.when((bi == 0) & (i == 0))
    def _():
        run_ref[...] = jnp.zeros_like(run_ref)

    is_ctx = _row_is_ctx(i, tm, s_len)
    h = _norm_mod(x_ref[...], g_ref[...], shx_ref[...], scx_ref[...], shc_ref[...], scc_ref[...], is_ctx)
    h_ref[...] = h.astype(h_ref.dtype)

    logits = lax.dot_general(rw_ref[...], h, (((1,), (1,)), ((), ())),
                             precision=lax.Precision.HIGHEST, preferred_element_type=F32)
    scores = jax.nn.sigmoid(logits)
    biased = scores + rb_ref[...]
    b_rows = [biased[e:e + 1, :] for e in range(n_exp)]
    s_rows = [scores[e:e + 1, :] for e in range(n_exp)]
    epg = EXPERTS_PER_GROUP
    group_scores = []
    for g in range(N_EXPERT_GROUPS):
        r = b_rows[g * epg:(g + 1) * epg]
        pair = None
        for a in range(epg):
            for c in range(a + 1, epg):
                t = r[a] + r[c]
                pair = t if pair is None else jnp.maximum(pair, t)
        group_scores.append(pair)
    group, _ = _first_argmax(group_scores)
    in_group = []
    for k in range(epg):
        sel = b_rows[k]
        for g in range(1, N_EXPERT_GROUPS):
            sel = jnp.where(group == g, b_rows[g * epg + k], sel)
        in_group.append(sel)
    loc0, _ = _first_argmax(in_group)
    rest = [jnp.where(loc0 == k, -jnp.inf, in_group[k]) for k in range(epg)]
    loc1, _ = _first_argmax(rest)
    e0 = group * epg + loc0
    e1 = group * epg + loc1
    g0 = jnp.zeros_like(s_rows[0])
    g1 = jnp.zeros_like(s_rows[0])
    for e in range(n_exp):
        g0 = jnp.where(e0 == e, s_rows[e], g0)
        g1 = jnp.where(e1 == e, s_rows[e], g1)
    tot = g0 + g1

    eidx = lax.broadcasted_iota(jnp.int32, (n_exp, tm), 0)
    hit0 = eidx == e0
    hit1 = eidx == e1
    onehot = jnp.where(hit0 | hit1, 1.0, 0.0)
    before = lax.broadcasted_iota(jnp.int32, (tm, tm), 0) < lax.broadcasted_iota(jnp.int32, (tm, tm), 1)
    tri = jnp.where(before, 1.0, 0.0).astype(BF16)
    run = run_ref[:, 0:1]
    rank = jnp.dot(onehot.astype(BF16), tri, preferred_element_type=F32) + run
    rank0 = jnp.sum(jnp.where(hit0, rank, 0.0), axis=0, keepdims=True)
    rank1 = jnp.sum(jnp.where(hit1, rank, 0.0), axis=0, keepdims=True)
    new_run = run + jnp.sum(onehot, axis=1, keepdims=True)
    run_ref[...] = jnp.broadcast_to(new_run, run_ref.shape)
    cnt_ref[...] = jnp.broadcast_to(new_run, cnt_ref.shape).astype(jnp.int32)

    zi = jnp.zeros((4, tm), jnp.int32)
    ri_ref[...] = jnp.concatenate([e0, e1, rank0.astype(jnp.int32), rank1.astype(jnp.int32), zi], axis=0)
    zf = jnp.zeros((6, tm), F32)
    rf_ref[...] = jnp.concatenate([g0 / tot, g1 / tot, zf], axis=0)


def _router_call(x1, g, mod3, router_w, router_b, s_len, k_shift, k_scale):
    b, n, d = x1.shape
    n_exp = router_w.shape[1]
    tm = _pick(n, (ROW_TILE, 128))
    per_b = n // tm
    mx = lambda k: pl.BlockSpec((None, 1, d), lambda bi, i: (bi, 0, k))
    mc = lambda k: pl.BlockSpec((None, 1, d), lambda bi, i: (b, 0, k))
    tok_spec = pl.BlockSpec((8, tm), lambda bi, i: (0, bi * per_b + i))
    return pl.pallas_call(
        functools.partial(_router_kernel, s_len=s_len, tm=tm, n_exp=n_exp),
        grid=(b, per_b),
        in_specs=[pl.BlockSpec((None, tm, d), lambda bi, i: (bi, i, 0)),
                  pl.BlockSpec((1, d), lambda bi, i: (0, 0)),
                  mx(k_shift), mx(k_scale), mc(k_shift), mc(k_scale),
                  pl.BlockSpec((n_exp, d), lambda bi, i: (0, 0)),
                  pl.BlockSpec((n_exp, 1), lambda bi, i: (0, 0))],
        out_specs=[pl.BlockSpec((None, tm, d), lambda bi, i: (bi, i, 0)),
                   tok_spec, tok_spec,
                   pl.BlockSpec((n_exp, HEAD_W), lambda bi, i: (0, 0))],
        out_shape=[jax.ShapeDtypeStruct((b, n, d), F32),
                   jax.ShapeDtypeStruct((8, b * n), jnp.int32),
                   jax.ShapeDtypeStruct((8, b * n), F32),
                   jax.ShapeDtypeStruct((n_exp, HEAD_W), jnp.int32)],
        scratch_shapes=[pltpu.VMEM((n_exp, HEAD_W), F32)],
        compiler_params=_cparams(2, VMEM_LIMIT),
        name="router",
    )(x1, g.reshape(1, d), mod3, mod3, mod3, mod3, router_w.T, router_b.reshape(n_exp, 1))


def _row_copy(src_ref, src_row, dst_ref, dst_row, sem):
    return pltpu.make_async_copy(src_ref.at[pl.ds(src_row, 1)], dst_ref.at[pl.ds(dst_row, 1)], sem)


def _dispatch_kernel(p0_ref, p1_ref, h_ref, xs_in_ref, xs_ref, sem, *, tm):
    del xs_in_ref
    base = pl.program_id(0) * tm

    def issue(r, carry):
        _row_copy(h_ref, r, xs_ref, p0_ref[base + r], sem).start()
        _row_copy(h_ref, r, xs_ref, p1_ref[base + r], sem).start()
        return carry

    lax.fori_loop(0, tm, issue, 0)

    def drain(r, carry):
        _row_copy(h_ref, 0, xs_ref, 0, sem).wait()
        _row_copy(h_ref, 0, xs_ref, 0, sem).wait()
        return carry

    lax.fori_loop(0, tm, drain, 0)


def _dispatch_call(h2, pos0, pos1, n_pad):
    t, d = h2.shape
    tm = _pick(t, (ROW_TILE, 128))
    grid_spec = pltpu.PrefetchScalarGridSpec(
        num_scalar_prefetch=2,
        grid=(t // tm,),
        in_specs=[pl.BlockSpec((tm, d), lambda i, p0, p1: (i, 0)),
                  pl.BlockSpec(memory_space=pl.ANY)],
        out_specs=pl.BlockSpec(memory_space=pl.ANY),
        scratch_shapes=[pltpu.SemaphoreType.DMA],
    )
    return pl.pallas_call(
        functools.partial(_dispatch_kernel, tm=tm),
        grid_spec=grid_spec,
        out_shape=jax.ShapeDtypeStruct((n_pad, d), F32),
        input_output_aliases={3: 0},
        compiler_params=_cparams(1, VMEM_LIMIT),
        name="dispatch",
    )(pos0, pos1, h2, jnp.zeros((n_pad, d), F32))


def _ffn_kernel(be_ref, nb_ref, x_ref, wg_ref, wu_ref, wd_ref, y_ref):
    blk = pl.program_id(0)

    @pl.when(blk < nb_ref[0])
    def _():
        x = x_ref[...].astype(BF16)
        gate = jnp.dot(x, wg_ref[...], preferred_element_type=F32)
        up = jnp.dot(x, wu_ref[...], preferred_element_type=F32)
        hid = (jax.nn.silu(gate) * up).astype(BF16)
        y_ref[...] = jnp.dot(hid, wd_ref[...], preferred_element_type=F32)

    @pl.when(blk >= nb_ref[0])
    def _():
        y_ref[...] = jnp.zeros_like(y_ref)


def _ffn_call(xs, blk_expert, n_used, wg, wu, wd):
    n_pad, d = xs.shape
    d_exp = wg.shape[2]
    grid_spec = pltpu.PrefetchScalarGridSpec(
        num_scalar_prefetch=2,
        grid=(n_pad // MOE_BLOCK,),
        in_specs=[pl.BlockSpec((MOE_BLOCK, d), lambda i, be, nb: (i, 0)),
                  pl.BlockSpec((None, d, d_exp), lambda i, be, nb: (be[i], 0, 0)),
                  pl.BlockSpec((None, d, d_exp), lambda i, be, nb: (be[i], 0, 0)),
                  pl.BlockSpec((None, d_exp, d), lambda i, be, nb: (be[i], 0, 0))],
        out_specs=pl.BlockSpec((MOE_BLOCK, d), lambda i, be, nb: (i, 0)),
    )
    return pl.pallas_call(
        _ffn_kernel,
        grid_spec=grid_spec,
        out_shape=jax.ShapeDtypeStruct((n_pad, d), F32),
        compiler_params=_cparams(1, VMEM_LIMIT),
        name="expert_ffn",
    )(blk_expert, n_used, xs, wg, wu, wd)


def _combine_kernel(p0_ref, p1_ref, ys_ref, x_ref, gt_ref, gx_ref, gc_ref, o_ref, y0_ref, y1_ref, sem,
                    *, s_len, n_len, tm):
    i = pl.program_id(0)
    base = i * tm

    def issue(r, carry):
        _row_copy(ys_ref, p0_ref[base + r], y0_ref, r, sem).start()
        _row_copy(ys_ref, p1_ref[base + r], y1_ref, r, sem).start()
        return carry

    lax.fori_loop(0, tm, issue, 0)

    def drain(r, carry):
        _row_copy(ys_ref, 0, y0_ref, 0, sem).wait()
        _row_copy(ys_ref, 0, y1_ref, 0, sem).wait()
        return carry

    lax.fori_loop(0, tm, drain, 0)

    row = (base % n_len) + lax.broadcasted_iota(jnp.int32, (tm, 1), 0)
    gate = jnp.where(row >= s_len, gc_ref[...], gx_ref[...])
    gt = gt_ref[...]
    y = gt[:, 0:1] * y0_ref[...] + gt[:, 1:2] * y1_ref[...]
    o_ref[...] = x_ref[...] + gate * y


def _combine_call(ys, pos0, pos1, x1, gates, mod3, s_len, k_gate):
    b, n, d = x1.shape
    t = b * n
    tm = _pick(n, (ROW_TILE, 128))
    grid_spec = pltpu.PrefetchScalarGridSpec(
        num_scalar_prefetch=2,
        grid=(t // tm,),
        in_specs=[pl.BlockSpec(memory_space=pl.ANY),
                  pl.BlockSpec((tm, d), lambda i, p0, p1: (i, 0)),
                  pl.BlockSpec((tm, TOP_K), lambda i, p0, p1: (i, 0)),
                  pl.BlockSpec((None, 1, d), lambda i, p0, p1: ((i * tm) // n, 0, k_gate)),
                  pl.BlockSpec((None, 1, d), lambda i, p0, p1: (b, 0, k_gate))],
        out_specs=pl.BlockSpec((tm, d), lambda i, p0, p1: (i, 0)),
        scratch_shapes=[pltpu.VMEM((tm, d), F32), pltpu.VMEM((tm, d), F32), pltpu.SemaphoreType.DMA],
    )
    out = pl.pallas_call(
        functools.partial(_combine_kernel, s_len=s_len, n_len=n, tm=tm),
        grid_spec=grid_spec,
        out_shape=jax.ShapeDtypeStruct((t, d), F32),
        compiler_params=_cparams(1, VMEM_LIMIT),
        name="combine",
    )(pos0, pos1, ys, x1.reshape(t, d), gates, mod3, mod3)
    return out.reshape(b, n, d)


def _moe(x1, g2, mod3, router_w, router_b, wg, wu, wd, s_len):
    b, n, d = x1.shape
    t = b * n
    n_exp = router_w.shape[1]
    h2, ri, rf, cnt = _router_call(x1, g2, mod3, router_w, router_b, s_len, 3, 4)
    counts = cnt[:, 0]
    n_blk = (counts + MOE_BLOCK - 1) // MOE_BLOCK
    blk_end = jnp.cumsum(n_blk)
    starts = (blk_end - n_blk) * MOE_BLOCK
    pos0 = (starts[ri[0]] + ri[2]).astype(jnp.int32)
    pos1 = (starts[ri[1]] + ri[3]).astype(jnp.int32)
    n_pad = t * TOP_K + n_exp * MOE_BLOCK
    n_used = blk_end[-1:].astype(jnp.int32)
    blk_ids = jnp.minimum(jnp.arange(n_pad // MOE_BLOCK, dtype=jnp.int32), n_used[0] - 1)
    blk_expert = jnp.searchsorted(blk_end, blk_ids, side="right").astype(jnp.int32)
    xs = _dispatch_call(h2.reshape(t, d), pos0, pos1, n_pad)
    ys = _ffn_call(xs, blk_expert, n_used, wg.astype(BF16), wu.astype(BF16), wd.astype(BF16))
    return _combine_call(ys, pos0, pos1, x1, rf[:TOP_K].T, mod3, s_len, 5)


def _final_norm_kernel(x_ref, g_ref, o_ref):
    x = x_ref[...]
    o_ref[...] = x * lax.rsqrt(jnp.mean(x * x, axis=-1, keepdims=True) + EPS) * g_ref[...]


def _final_norm(xc, g, s_len):
    b, _, d = xc.shape
    tm = _pick(s_len, (ROW_TILE, 128))
    return pl.pallas_call(
        _final_norm_kernel,
        grid=(b, s_len // tm),
        in_specs=[pl.BlockSpec((None, tm, d), lambda bi, i: (bi, i, 0)),
                  pl.BlockSpec((1, d), lambda bi, i: (0, 0))],
        out_specs=pl.BlockSpec((None, tm, d), lambda bi, i: (bi, i, 0)),
        out_shape=jax.ShapeDtypeStruct((b, s_len, d), F32),
        compiler_params=_cparams(2, VMEM_LIMIT),
        name="final_norm",
    )(xc, g.reshape(1, d))


def kernel(x, c, ctx, c_ctx, w_ada, b_ada, g_norm1, g_norm2, w_in, w_out, na_rpb, da_lambda, da_subln_g,
           pool_w, pool_scale, sg_norm_g, sg_w, sg_b, router_w, router_b, exp_w_gate, exp_w_up, exp_w_down,
           g_final):
    b, s_len, d = x.shape
    c_len = ctx.shape[1]
    n = s_len + c_len
    depth = w_ada.shape[0]
    assert b + 1 <= 8 and w_in.shape[2] == ATT_COLS + REST_COLS and w_out.shape[1] == 4 * MIX_W

    cond = jnp.zeros((8, d), F32).at[:b].set(c).at[b].set(c_ctx)
    mod_all = _ada_all(cond, w_ada, b_ada)
    cos, sin = _rope_tables(s_len, c_len)
    xc = jnp.concatenate([x, ctx], axis=1)

    for l in range(depth):
        mod3 = mod_all[l].reshape(8, 1, N_MOD * d)
        h1 = _norm_mod_call(xc, g_norm1[l], mod3, s_len, 0, 1).reshape(b * n, d)
        pxa = _matmul_cols(h1, w_in, l, 0, ATT_COLS, BF16, 1024).reshape(b, n, ATT_COLS)
        pxb = _matmul_cols(h1, w_in, l, ATT_COLS, REST_COLS, F32, 768).reshape(b, n, REST_COLS)
        qk_rot = _rope_call(pxa, cos, sin)
        mix_a = _na_call(pxa, na_rpb[l], s_len, c_len)
        mix_b = _da_call(qk_rot, pxa, da_lambda[l], da_subln_g[l], s_len, c_len, l)
        mix_c = _pool_call(pxb, pool_w[l], pool_scale[l], s_len, c_len)
        mix_d = _sg_call(pxb, sg_norm_g[l], sg_w[l], sg_b[l])
        x1 = _out_proj((mix_a, mix_b, mix_c, mix_d), w_out, l, xc, mod3, s_len, 2)
        xc = _moe(x1, g_norm2[l], mod3, router_w, router_b, exp_w_gate[l], exp_w_up[l], exp_w_down[l], s_len)
    return _final_norm(xc, g_final, s_len)
```

```python
import functools
import math

import numpy as np
import jax
import jax.numpy as jnp
from jax import lax
from jax.experimental import pallas as pl
from jax.experimental.pallas import tpu as pltpu

GRID_W = 64
NA_HEADS = 4
NA_DIM = 128
NA_WIN_H = 8
NA_WIN_W = 16
DA_HEADS = 4
DA_DIM = 64
DA_VDIM = 2 * DA_DIM
ROPE_BASE = 10000.0
POOL_GROUPS = 4
POOL_WIDTH = 128
POOL_WINDOWS = (2, 4, 8, 16)
SG_GROUPS = 4
SG_WIDTH = 128
SG_CHUNK = 128
N_EXPERT_GROUPS = 4
EXPERTS_PER_GROUP = 4
TOP_K = 2
N_MOD = 6
EPS = 1e-6
NEG_INF = -1e30
LOG2_E = math.log2(math.e)

HEAD_W = 128
MIX_W = 512
ATT_COLS = 6 * MIX_W
REST_COLS = 3 * MIX_W
MOE_BLOCK = 256
NA_Q_ROWS = 4
ROW_TILE = 256
DA_KEY_CHUNKS = (1408, 768, 512, 256, 128)
DA_VT_ROWS = DA_VDIM + 16
VMEM_LIMIT = 56 * 1024 * 1024

F32 = jnp.float32
BF16 = jnp.bfloat16


def _cparams(n_axes, vmem=None):
    return pltpu.CompilerParams(dimension_semantics=("arbitrary",) * n_axes,
                                vmem_limit_bytes=vmem)


def _pick(n, candidates):
    for c in candidates:
        if n % c == 0:
            return c
    raise ValueError(f"no tile in {candidates} divides {n}")


def _ada_kernel(c_ref, w_ref, b_ref, o_ref):
    cs = jax.nn.silu(c_ref[...]).astype(BF16)
    o_ref[...] = jnp.dot(cs, w_ref[...].astype(BF16), preferred_element_type=F32) + b_ref[...]


def _ada_all(cond, w_ada, b_ada):
    depth, d, n6 = w_ada.shape
    tn = _pick(n6, (1024, 512, 256, 128))
    return pl.pallas_call(
        _ada_kernel,
        grid=(depth, n6 // tn),
        in_specs=[pl.BlockSpec((8, d), lambda l, j: (0, 0)),
                  pl.BlockSpec((None, d, tn), lambda l, j: (l, 0, j)),
                  pl.BlockSpec((None, 1, tn), lambda l, j: (l, 0, j))],
        out_specs=pl.BlockSpec((None, 8, tn), lambda l, j: (l, 0, j)),
        out_shape=jax.ShapeDtypeStruct((depth, 8, n6), F32),
        compiler_params=_cparams(2, VMEM_LIMIT),
        name="adaln",
    )(cond, w_ada, b_ada.reshape(depth, 1, n6))


def _row_is_ctx(i, tm, s_len):
    row = i * tm + lax.broadcasted_iota(jnp.int32, (tm, 1), 0)
    return row >= s_len


def _norm_mod(x, g, shx, scx, shc, scc, is_ctx):
    y = x * lax.rsqrt(jnp.mean(x * x, axis=-1, keepdims=True) + EPS) * g
    shift = jnp.where(is_ctx, shc, shx)
    scale = jnp.where(is_ctx, scc, scx)
    return y * (1.0 + scale) + shift


def _norm_mod_kernel(x_ref, g_ref, shx_ref, scx_ref, shc_ref, scc_ref, o_ref, *, s_len, tm):
    is_ctx = _row_is_ctx(pl.program_id(1), tm, s_len)
    h = _norm_mod(x_ref[...], g_ref[...], shx_ref[...], scx_ref[...], shc_ref[...], scc_ref[...], is_ctx)
    o_ref[...] = h.astype(o_ref.dtype)


def _norm_mod_call(xc, g, mod3, s_len, k_shift, k_scale):
    b, n, d = xc.shape
    tm = _pick(n, (ROW_TILE, 128))
    mx = lambda k: pl.BlockSpec((None, 1, d), lambda bi, i: (bi, 0, k))
    mc = lambda k: pl.BlockSpec((None, 1, d), lambda bi, i: (b, 0, k))
    return pl.pallas_call(
        functools.partial(_norm_mod_kernel, s_len=s_len, tm=tm),
        grid=(b, n // tm),
        in_specs=[pl.BlockSpec((None, tm, d), lambda bi, i: (bi, i, 0)),
                  pl.BlockSpec((1, d), lambda bi, i: (0, 0)),
                  mx(k_shift), mx(k_scale), mc(k_shift), mc(k_scale)],
        out_specs=pl.BlockSpec((None, tm, d), lambda bi, i: (bi, i, 0)),
        out_shape=jax.ShapeDtypeStruct((b, n, d), BF16),
        compiler_params=_cparams(2, VMEM_LIMIT),
        name="norm_mod",
    )(xc, g.reshape(1, d), mod3, mod3, mod3, mod3)


def _mm_kernel(a_ref, w_ref, o_ref, wbf_ref):
    @pl.when(pl.program_id(1) == 0)
    def _():
        wbf_ref[...] = w_ref[...].astype(BF16)

    o_ref[...] = jnp.dot(a_ref[...], wbf_ref[...], preferred_element_type=F32).astype(o_ref.dtype)


def _matmul_cols(a, w, layer, col0, ncols, out_dtype, tn):
    m, k = a.shape
    tm = _pick(m, (768, 512, 384, 256, 128))
    off = col0 // tn
    return pl.pallas_call(
        _mm_kernel,
        grid=(ncols // tn, m // tm),
        in_specs=[pl.BlockSpec((tm, k), lambda j, i: (i, 0)),
                  pl.BlockSpec((None, k, tn), lambda j, i: (layer, 0, j + off))],
        out_specs=pl.BlockSpec((tm, tn), lambda j, i: (i, j)),
        out_shape=jax.ShapeDtypeStruct((m, ncols), out_dtype),
        scratch_shapes=[pltpu.VMEM((k, tn), BF16)],
        compiler_params=_cparams(2, VMEM_LIMIT),
        name="in_proj",
    )(a, w)


def _out_proj_kernel(ma_ref, mb_ref, mc_ref, md_ref, w_ref, x_ref, gx_ref, gc_ref, o_ref, wbf_ref,
                     *, s_len, tm):
    @pl.when((pl.program_id(1) == 0) & (pl.program_id(2) == 0))
    def _():
        wbf_ref[...] = w_ref[...].astype(BF16)

    acc = jnp.dot(ma_ref[...], wbf_ref[0 * MIX_W:1 * MIX_W, :], preferred_element_type=F32)
    acc += jnp.dot(mb_ref[...], wbf_ref[1 * MIX_W:2 * MIX_W, :], preferred_element_type=F32)
    acc += jnp.dot(mc_ref[...], wbf_ref[2 * MIX_W:3 * MIX_W, :], preferred_element_type=F32)
    acc += jnp.dot(md_ref[...], wbf_ref[3 * MIX_W:4 * MIX_W, :], preferred_element_type=F32)
    gate = jnp.where(_row_is_ctx(pl.program_id(2), tm, s_len), gc_ref[...], gx_ref[...])
    o_ref[...] = x_ref[...] + gate * acc


def _out_proj(mixes, w_out, layer, xc, mod3, s_len, k_gate):
    b, n, d = xc.shape
    tm = _pick(n, (768, 512, 384, 256, 128))
    tn = _pick(d, (512, 256, 128))
    per = d // tn
    mix_spec = pl.BlockSpec((None, tm, MIX_W), lambda j, bi, i: (bi, i, 0))
    return pl.pallas_call(
        functools.partial(_out_proj_kernel, s_len=s_len, tm=tm),
        grid=(d // tn, b, n // tm),
        in_specs=[mix_spec, mix_spec, mix_spec, mix_spec,
                  pl.BlockSpec((None, 4 * MIX_W, tn), lambda j, bi, i: (layer, 0, j)),
                  pl.BlockSpec((None, tm, tn), lambda j, bi, i: (bi, i, j)),
                  pl.BlockSpec((None, 1, tn), lambda j, bi, i: (bi, 0, k_gate * per + j)),
                  pl.BlockSpec((None, 1, tn), lambda j, bi, i: (b, 0, k_gate * per + j))],
        out_specs=pl.BlockSpec((None, tm, tn), lambda j, bi, i: (bi, i, j)),
        out_shape=jax.ShapeDtypeStruct((b, n, d), F32),
        scratch_shapes=[pltpu.VMEM((4 * MIX_W, tn), BF16)],
        compiler_params=_cparams(3, VMEM_LIMIT),
        name="out_proj",
    )(*mixes, w_out, xc, mod3, mod3)


def _rope_tables(s_len, c_len):
    t = jnp.arange(s_len)
    row = (t // GRID_W).astype(F32)
    col = (t % GRID_W).astype(F32)
    n_freq = DA_DIM // 4
    inv_freq = 1.0 / (ROPE_BASE ** (jnp.arange(n_freq, dtype=F32) / n_freq))
    ang = jnp.concatenate([row[:, None] * inv_freq, col[:, None] * inv_freq], axis=-1)
    cos = jnp.concatenate([jnp.cos(ang), jnp.ones((c_len, DA_DIM // 2), F32)], axis=0)
    sin = jnp.concatenate([jnp.sin(ang), jnp.zeros((c_len, DA_DIM // 2), F32)], axis=0)
    return jnp.tile(cos, (1, 4)), jnp.tile(jnp.concatenate([-sin, sin], axis=-1), (1, 2))


def _rope_kernel(q_ref, k_ref, v_ref, c_ref, s_ref, qt_ref, kr_ref, vt_ref, *, tm, q_scale):
    cos = c_ref[...]
    sin = s_ref[...]
    lane = lax.broadcasted_iota(jnp.int32, (tm, HEAD_W), 1)
    first_half = (lane % DA_DIM) < (DA_DIM // 2)

    def rotate(x):
        swapped = jnp.where(first_half, pltpu.roll(x, HEAD_W - DA_DIM // 2, 1), pltpu.roll(x, DA_DIM // 2, 1))
        return x * cos + swapped * sin

    for h in range(DA_HEADS):
        cols = slice(h * HEAD_W, (h + 1) * HEAD_W)
        q = rotate(q_ref[:, cols].astype(F32)) * q_scale
        qt_ref[cols, :] = q.T.astype(qt_ref.dtype)
        kr_ref[:, cols] = rotate(k_ref[:, cols].astype(F32)).astype(kr_ref.dtype)
        vt_ref[h * DA_VT_ROWS:h * DA_VT_ROWS + DA_VDIM, :] = v_ref[:, cols].astype(F32).T.astype(vt_ref.dtype)
        pad = lax.broadcasted_iota(jnp.int32, (DA_VT_ROWS - DA_VDIM, tm), 0)
        vt_ref[h * DA_VT_ROWS + DA_VDIM:(h + 1) * DA_VT_ROWS, :] = jnp.where(pad == 0, 1.0, 0.0).astype(vt_ref.dtype)


def _rope_call(pxa, cos, sin):
    b, n, _ = pxa.shape
    tm = _pick(n, (ROW_TILE, 128))
    col_spec = lambda k: pl.BlockSpec((None, tm, MIX_W), lambda bi, i: (bi, i, k))
    t_spec = lambda rows: pl.BlockSpec((None, rows, tm), lambda bi, i: (bi, 0, i))
    return pl.pallas_call(
        functools.partial(_rope_kernel, tm=tm, q_scale=DA_DIM ** -0.5 * LOG2_E),
        grid=(b, n // tm),
        in_specs=[col_spec(3), col_spec(4), col_spec(5),
                  pl.BlockSpec((tm, HEAD_W), lambda bi, i: (i, 0)),
                  pl.BlockSpec((tm, HEAD_W), lambda bi, i: (i, 0))],
        out_specs=[t_spec(MIX_W), col_spec(0), t_spec(DA_HEADS * DA_VT_ROWS)],
        out_shape=[jax.ShapeDtypeStruct((b, MIX_W, n), BF16),
                   jax.ShapeDtypeStruct((b, n, MIX_W), BF16),
                   jax.ShapeDtypeStruct((b, DA_HEADS * DA_VT_ROWS, n), BF16)],
        compiler_params=_cparams(2, VMEM_LIMIT),
        name="rope",
    )(pxa, pxa, pxa, cos, sin)


def _na_plan(rows, r_q):
    kh = min(NA_WIN_H, rows)
    key_rows = r_q + kh - 1
    assert rows % r_q == 0 and rows >= key_rows
    patterns, var, ustart = [], [], []
    for j in range(rows // r_q):
        r = r_q * j + np.arange(r_q)
        rs = np.clip(r - kh // 2, 0, rows - kh)
        u = int(np.clip(rs.min(), 0, rows - key_rows))
        kr = u + np.arange(key_rows)
        valid = (kr[None, :] >= rs[:, None]) & (kr[None, :] < rs[:, None] + kh)
        assert (valid.sum(1) == kh).all()
        dr0 = u - r + (NA_WIN_H - 1)
        key = (valid.tobytes(), dr0.tobytes())
        for v, (k2, _, _) in enumerate(patterns):
            if k2 == key:
                break
        else:
            v = len(patterns)
            patterns.append((key, valid, dr0))
        var.append(v)
        ustart.append(u * GRID_W)
    valid = np.stack([p[1] for p in patterns])
    dr0 = np.stack([p[2] for p in patterns])
    return np.asarray(var, np.int32), np.asarray(ustart, np.int32), valid, dr0


def _na_bias_tables(rpb, valid, dr0):
    col = np.arange(GRID_W)
    cs = np.clip(col - NA_WIN_W // 2, 0, GRID_W - NA_WIN_W)
    cmask = (col[None, :] >= cs[:, None]) & (col[None, :] < cs[:, None] + NA_WIN_W)
    dc = np.clip(col[None, :] - col[:, None], -(NA_WIN_W - 1), NA_WIN_W - 1) + (NA_WIN_W - 1)
    heads, n_dr, n_dc = rpb.shape
    v, r_q, key_rows = valid.shape
    pick = (dc.reshape(1, -1) == np.arange(n_dc)[:, None]).astype(np.float32)
    by_col = jnp.dot(rpb.astype(F32).reshape(heads * n_dr, n_dc), pick,
                     precision=lax.Precision.HIGHEST).reshape(heads, n_dr, GRID_W, GRID_W)
    padded = jnp.pad(by_col, ((0, 0), (key_rows, key_rows), (0, 0), (0, 0)))
    tab = jnp.stack([lax.slice_in_dim(padded, int(d) + key_rows, int(d) + 2 * key_rows, axis=1)
                     for d in dr0.reshape(-1)], axis=1)
    tab = tab.reshape(heads, v, r_q, key_rows, GRID_W, GRID_W).transpose(0, 1, 2, 4, 3, 5)
    mask = valid[:, :, None, :, None] & cmask[None, None, :, None, :]
    tab = jnp.where(mask[None], tab, NEG_INF)
    return tab.reshape(heads, v, r_q * GRID_W, key_rows * GRID_W)


def _na_kernel(var_ref, u_ref, q_ref, k_ref, v_ref, kc_ref, vc_ref, bias_ref, o_ref, *, n_lat, kw, scale):
    j = pl.program_id(2)
    q = q_ref[...]
    nt = (((1,), (1,)), ((), ()))
    s_ctx = lax.dot_general(q, kc_ref[...], nt, preferred_element_type=F32) * scale

    @pl.when(j < n_lat)
    def _():
        u = pl.multiple_of(u_ref[j], GRID_W)
        k_win = k_ref[pl.ds(u, kw), :]
        v_win = v_ref[pl.ds(u, kw), :]
        s = lax.dot_general(q, k_win, nt, preferred_element_type=F32) * scale + bias_ref[...]
        m = jnp.maximum(jnp.max(s, axis=-1, keepdims=True), jnp.max(s_ctx, axis=-1, keepdims=True))
        p = jnp.exp(s - m)
        pc = jnp.exp(s_ctx - m)
        l = jnp.sum(p, axis=-1, keepdims=True) + jnp.sum(pc, axis=-1, keepdims=True)
        o = jnp.dot(p.astype(BF16), v_win, preferred_element_type=F32)
        o += jnp.dot(pc.astype(BF16), vc_ref[...], preferred_element_type=F32)
        o_ref[...] = (o / l).astype(o_ref.dtype)

    @pl.when(j >= n_lat)
    def _():
        m = jnp.max(s_ctx, axis=-1, keepdims=True)
        pc = jnp.exp(s_ctx - m)
        l = jnp.sum(pc, axis=-1, keepdims=True)
        o = jnp.dot(pc.astype(BF16), vc_ref[...], preferred_element_type=F32)
        o_ref[...] = (o / l).astype(o_ref.dtype)


def _na_call(pxa, rpb, s_len, c_len):
    b, n, _ = pxa.shape
    rows = s_len // GRID_W
    tq = NA_Q_ROWS * GRID_W
    assert s_len % tq == 0 and c_len % tq == 0 and s_len % c_len == 0
    var, ustart, valid, dr0 = _na_plan(rows, NA_Q_ROWS)
    n_lat, n_q = s_len // tq, n // tq
    kw = valid.shape[2] * GRID_W
    tabs = _na_bias_tables(rpb, valid, dr0)
    pad = np.zeros(n_q - n_lat, np.int32)
    var = jnp.asarray(np.concatenate([var, pad]))
    ustart = jnp.asarray(np.concatenate([ustart, pad]))
    ctx_blk = s_len // c_len
    grid_spec = pltpu.PrefetchScalarGridSpec(
        num_scalar_prefetch=2,
        grid=(b, NA_HEADS, n_q),
        in_specs=[pl.BlockSpec((None, tq, HEAD_W), lambda bi, h, j, vr, ur: (bi, j, h)),
                  pl.BlockSpec((None, s_len, HEAD_W), lambda bi, h, j, vr, ur: (bi, 0, 4 + h)),
                  pl.BlockSpec((None, s_len, HEAD_W), lambda bi, h, j, vr, ur: (bi, 0, 8 + h)),
                  pl.BlockSpec((None, c_len, HEAD_W), lambda bi, h, j, vr, ur: (bi, ctx_blk, 4 + h)),
                  pl.BlockSpec((None, c_len, HEAD_W), lambda bi, h, j, vr, ur: (bi, ctx_blk, 8 + h)),
                  pl.BlockSpec((None, None, tq, kw), lambda bi, h, j, vr, ur: (h, vr[j], 0, 0))],
        out_specs=pl.BlockSpec((None, tq, HEAD_W), lambda bi, h, j, vr, ur: (bi, j, h)),
    )
    return pl.pallas_call(
        functools.partial(_na_kernel, n_lat=n_lat, kw=kw, scale=NA_DIM ** -0.5),
        grid_spec=grid_spec,
        out_shape=jax.ShapeDtypeStruct((b, n, MIX_W), BF16),
        compiler_params=_cparams(3, VMEM_LIMIT),
        name="nbr_attn",
    )(var, ustart, pxa, pxa, pxa, pxa, pxa, tabs)


def _da_kernel(qt_ref, k_ref, vt_ref, lam_ref, g_ref, o_ref, sa_ref, sb_ref, *, s_len, c_len, tq, tk, lam_init):
    qi = pl.program_id(2)
    n = s_len + c_len
    qt = qt_ref[...]
    feat = lax.broadcasted_iota(jnp.int32, qt.shape, 0)
    zero = jnp.zeros_like(qt)
    q2 = jnp.concatenate([jnp.where(feat < DA_DIM, qt, zero), jnp.where(feat >= DA_DIM, qt, zero)], axis=1)

    def scores(k_blk):
        return jnp.dot(k_blk, q2, preferred_element_type=F32)

    def update(s, vt_blk, carry):
        m, acc = carry
        m_new = jnp.maximum(m, jnp.max(s, axis=0, keepdims=True))
        alpha = jnp.exp2(m - m_new)
        p = jnp.exp2(s - m_new)
        acc_new = alpha * acc + jnp.dot(vt_blk, p.astype(BF16), preferred_element_type=F32)
        return m_new, acc_new

    def init():
        return jnp.full((1, 2 * tq), NEG_INF, F32), jnp.zeros((DA_VT_ROWS, 2 * tq), F32)

    def finalize(carry):
        lv = lam_ref[...]
        t1 = jnp.sum(lv[0:1, :] * lv[1:2, :], axis=-1, keepdims=True)
        t2 = jnp.sum(lv[2:3, :] * lv[3:4, :], axis=-1, keepdims=True)
        lam = jnp.exp(t1) - jnp.exp(t2) + lam_init
        _, acc = carry
        o = acc[:DA_VDIM, :] / acc[DA_VDIM:DA_VDIM + 1, :]
        o = o[:, :tq] - lam * o[:, tq:]
        y = o * lax.rsqrt(jnp.mean(o * o, axis=0, keepdims=True) + EPS) * g_ref[...]
        o_ref[...] = (y * (1.0 - lam_init)).T.astype(o_ref.dtype)

    @pl.when(qi < s_len // tq)
    def _():
        def scores_into(dst_ref, step):
            start = pl.multiple_of(step * tk, tk)
            dst_ref[...] = scores(k_ref[pl.ds(start, tk), :])

        def update_from(src_ref, step, stats):
            start = pl.multiple_of(step * tk, tk)
            return update(src_ref[...], vt_ref[:, pl.ds(start, tk)], stats)

        def pair(j, stats):
            c0 = 2 * j
            scores_into(sb_ref, c0 + 1)
            stats = update_from(sa_ref, c0, stats)
            scores_into(sa_ref, c0 + 2)
            return update_from(sb_ref, c0 + 1, stats)

        n_steps = n // tk
        full_pairs = (n_steps - 1) // 2
        scores_into(sa_ref, 0)
        stats = lax.fori_loop(0, full_pairs, pair, init())
        c0 = 2 * full_pairs
        if n_steps - c0 == 2:
            scores_into(sb_ref, c0 + 1)
            stats = update_from(sa_ref, c0, stats)
            stats = update_from(sb_ref, c0 + 1, stats)
        else:
            stats = update_from(sa_ref, c0, stats)
        finalize(stats)

    @pl.when(qi >= s_len // tq)
    def _():
        finalize(update(scores(k_ref[s_len:n, :]), vt_ref[:, s_len:n], init()))


def _da_call(qt, kr, vt, lam_vecs, subln_g, s_len, c_len, layer):
    b, n, _ = kr.shape
    tq = _pick(math.gcd(s_len, c_len), (256, 128))
    tk = _pick(n, DA_KEY_CHUNKS)
    lam_init = 0.8 - 0.6 * math.exp(-0.3 * layer)
    return pl.pallas_call(
        functools.partial(_da_kernel, s_len=s_len, c_len=c_len, tq=tq, tk=tk, lam_init=lam_init),
        grid=(b, DA_HEADS, n // tq),
        in_specs=[pl.BlockSpec((None, HEAD_W, tq), lambda bi, h, i: (bi, h, i)),
                  pl.BlockSpec((None, n, HEAD_W), lambda bi, h, i: (bi, 0, h)),
                  pl.BlockSpec((None, DA_VT_ROWS, n), lambda bi, h, i: (bi, h, 0)),
                  pl.BlockSpec((4, DA_DIM), lambda bi, h, i: (0, 0)),
                  pl.BlockSpec((DA_VDIM, 1), lambda bi, h, i: (0, 0))],
        out_specs=pl.BlockSpec((None, tq, HEAD_W), lambda bi, h, i: (bi, i, h)),
        out_shape=jax.ShapeDtypeStruct((b, n, MIX_W), BF16),
        scratch_shapes=[pltpu.VMEM((tk, 2 * tq), F32), pltpu.VMEM((tk, 2 * tq), F32)],
        compiler_params=_cparams(3, VMEM_LIMIT),
        name="diff_attn",
    )(qt, kr, vt, lam_vecs, subln_g.reshape(DA_VDIM, 1))


POOL_HALO = 8


def _pool_kernel(prev_ref, cur_ref, next_ref, w_ref, sc_ref, o_ref, buf_ref, *, s_len, c_len, tm):
    i = pl.program_id(1)
    n_lat = s_len // tm
    n_all = (s_len + c_len) // tm
    seq_start = (i == 0) | (i == n_lat)
    seq_end = (i == n_lat - 1) | (i == n_all - 1)
    buf_ref[0:POOL_HALO, :] = jnp.where(seq_start, 0.0, prev_ref[...])
    buf_ref[POOL_HALO:POOL_HALO + tm, :] = cur_ref[...]
    buf_ref[POOL_HALO + tm:, :] = jnp.where(seq_end, 0.0, next_ref[...])
    in_ctx = i >= n_lat
    seq_len = jnp.where(in_ctx, c_len, s_len)
    pos = i * tm - jnp.where(in_ctx, s_len, 0) + lax.broadcasted_iota(jnp.int32, (tm, 1), 0)
    for g, win in enumerate(POOL_WINDOWS):
        half = win // 2
        cols = slice(g * POOL_WIDTH, (g + 1) * POOL_WIDTH)
        tot = buf_ref[POOL_HALO - half:POOL_HALO - half + tm, cols]
        for d in range(-half + 1, half):
            tot = tot + buf_ref[POOL_HALO + d:POOL_HALO + d + tm, cols]
        cnt = jnp.clip(pos + half, 0, seq_len) - jnp.clip(pos - half, 0, seq_len)
        resid = tot / cnt.astype(F32) - cur_ref[:, cols]
        y = jnp.dot(resid.astype(BF16), w_ref[g].astype(BF16), preferred_element_type=F32)
        o_ref[:, cols] = (y * sc_ref[:, cols]).astype(o_ref.dtype)


def _pool_call(pxb, pool_w, pool_scale, s_len, c_len):
    b, n, _ = pxb.shape
    tm = _pick(math.gcd(s_len, c_len), (ROW_TILE, 128))
    hb = tm // POOL_HALO
    last = n // POOL_HALO - 1
    return pl.pallas_call(
        functools.partial(_pool_kernel, s_len=s_len, c_len=c_len, tm=tm),
        grid=(b, n // tm),
        in_specs=[pl.BlockSpec((None, POOL_HALO, MIX_W), lambda bi, i: (bi, jnp.maximum(i * hb - 1, 0), 0)),
                  pl.BlockSpec((None, tm, MIX_W), lambda bi, i: (bi, i, 0)),
                  pl.BlockSpec((None, POOL_HALO, MIX_W), lambda bi, i: (bi, jnp.minimum((i + 1) * hb, last), 0)),
                  pl.BlockSpec((POOL_GROUPS, POOL_WIDTH, POOL_WIDTH), lambda bi, i: (0, 0, 0)),
                  pl.BlockSpec((1, MIX_W), lambda bi, i: (0, 0))],
        out_specs=pl.BlockSpec((None, tm, MIX_W), lambda bi, i: (bi, i, 0)),
        out_shape=jax.ShapeDtypeStruct((b, n, MIX_W), BF16),
        scratch_shapes=[pltpu.VMEM((tm + 2 * POOL_HALO, MIX_W), F32)],
        compiler_params=_cparams(2, VMEM_LIMIT),
        name="pool",
    )(pxb, pxb, pxb, pool_w, pool_scale.reshape(1, MIX_W))


def _sg_kernel(u_ref, v_ref, g_ref, w_ref, b_ref, o_ref, *, tm):
    v = jax.nn.gelu(v_ref[...])
    vn = (v * lax.rsqrt(jnp.mean(v * v, axis=-1, keepdims=True) + EPS) * g_ref[...]).astype(BF16)
    bias = b_ref[...]
    for c in range(tm // SG_CHUNK):
        rows = slice(c * SG_CHUNK, (c + 1) * SG_CHUNK)
        for g in range(SG_GROUPS):
            cols = slice(g * SG_WIDTH, (g + 1) * SG_WIDTH)
            mixed = jnp.dot(w_ref[g].astype(BF16), vn[rows, cols], preferred_element_type=F32)
            mixed = mixed + bias[:, g:g + 1]
            o_ref[rows, cols] = (jax.nn.gelu(u_ref[rows, cols]) * mixed).astype(o_ref.dtype)


def _sg_call(pxb, sg_norm_g, sg_w, sg_b):
    b, n, _ = pxb.shape
    tm = _pick(n, (ROW_TILE, 128))
    return pl.pallas_call(
        functools.partial(_sg_kernel, tm=tm),
        grid=(b, n // tm),
        in_specs=[pl.BlockSpec((None, tm, MIX_W), lambda bi, i: (bi, i, 1)),
                  pl.BlockSpec((None, tm, MIX_W), lambda bi, i: (bi, i, 2)),
                  pl.BlockSpec((1, MIX_W), lambda bi, i: (0, 0)),
                  pl.BlockSpec((SG_GROUPS, SG_CHUNK, SG_CHUNK), lambda bi, i: (0, 0, 0)),
                  pl.BlockSpec((SG_CHUNK, SG_GROUPS), lambda bi, i: (0, 0))],
        out_specs=pl.BlockSpec((None, tm, MIX_W), lambda bi, i: (bi, i, 0)),
        out_shape=jax.ShapeDtypeStruct((b, n, MIX_W), BF16),
        compiler_params=_cparams(2, VMEM_LIMIT),
        name="spatial_gate",
    )(pxb, pxb, sg_norm_g.reshape(1, MIX_W), sg_w, sg_b.T)


def _first_argmax(vals):
    best, idx = vals[0], jnp.zeros(vals[0].shape, jnp.int32)
    for k in range(1, len(vals)):
        take = vals[k] > best
        best = jnp.where(take, vals[k], best)
        idx = jnp.where(take, k, idx)
    return idx, best


def _router_kernel(x_ref, g_ref, shx_ref, scx_ref, shc_ref, scc_ref, rw_ref, rb_ref,
                   h_ref, ri_ref, rf_ref, cnt_ref, run_ref, *, s_len, tm, n_exp):
    bi, i = pl.program_id(0), pl.program_id(1)

    @pl.when((bi == 0) & (i == 0))
    def _():
        run_ref[...] = jnp.zeros_like(run_ref)

    is_ctx = _row_is_ctx(i, tm, s_len)
    h = _norm_mod(x_ref[...], g_ref[...], shx_ref[...], scx_ref[...], shc_ref[...], scc_ref[...], is_ctx)
    h_ref[...] = h.astype(h_ref.dtype)

    logits = lax.dot_general(rw_ref[...], h, (((1,), (1,)), ((), ())),
                             precision=lax.Precision.HIGHEST, preferred_element_type=F32)
    scores = jax.nn.sigmoid(logits)
    biased = scores + rb_ref[...]
    b_rows = [biased[e:e + 1, :] for e in range(n_exp)]
    s_rows = [scores[e:e + 1, :] for e in range(n_exp)]
    epg = EXPERTS_PER_GROUP
    group_scores = []
    for g in range(N_EXPERT_GROUPS):
        r = b_rows[g * epg:(g + 1) * epg]
        pair = None
        for a in range(epg):
            for c in range(a + 1, epg):
                t = r[a] + r[c]
                pair = t if pair is None else jnp.maximum(pair, t)
        group_scores.append(pair)
    group, _ = _first_argmax(group_scores)
    in_group = []
    for k in range(epg):
        sel = b_rows[k]
        for g in range(1, N_EXPERT_GROUPS):
            sel = jnp.where(group == g, b_rows[g * epg + k], sel)
        in_group.append(sel)
    loc0, _ = _first_argmax(in_group)
    rest = [jnp.where(loc0 == k, -jnp.inf, in_group[k]) for k in range(epg)]
    loc1, _ = _first_argmax(rest)
    e0 = group * epg + loc0
    e1 = group * epg + loc1
    g0 = jnp.zeros_like(s_rows[0])
    g1 = jnp.zeros_like(s_rows[0])
    for e in range(n_exp):
        g0 = jnp.where(e0 == e, s_rows[e], g0)
        g1 = jnp.where(e1 == e, s_rows[e], g1)
    tot = g0 + g1

    eidx = lax.broadcasted_iota(jnp.int32, (n_exp, tm), 0)
    hit0 = eidx == e0
    hit1 = eidx == e1
    onehot = jnp.where(hit0 | hit1, 1.0, 0.0)
    before = lax.broadcasted_iota(jnp.int32, (tm, tm), 0) < lax.broadcasted_iota(jnp.int32, (tm, tm), 1)
    tri = jnp.where(before, 1.0, 0.0).astype(BF16)
    run = run_ref[:, 0:1]
    rank = jnp.dot(onehot.astype(BF16), tri, preferred_element_type=F32) + run
    rank0 = jnp.sum(jnp.where(hit0, rank, 0.0), axis=0, keepdims=True)
    rank1 = jnp.sum(jnp.where(hit1, rank, 0.0), axis=0, keepdims=True)
    new_run = run + jnp.sum(onehot, axis=1, keepdims=True)
    run_ref[...] = jnp.broadcast_to(new_run, run_ref.shape)
    cnt_ref[...] = jnp.broadcast_to(new_run, cnt_ref.shape).astype(jnp.int32)

    zi = jnp.zeros((4, tm), jnp.int32)
    ri_ref[...] = jnp.concatenate([e0, e1, rank0.astype(jnp.int32), rank1.astype(jnp.int32), zi], axis=0)
    zf = jnp.zeros((6, tm), F32)
    rf_ref[...] = jnp.concatenate([g0 / tot, g1 / tot, zf], axis=0)


def _router_call(x1, g, mod3, router_w, router_b, s_len, k_shift, k_scale):
    b, n, d = x1.shape
    n_exp = router_w.shape[1]
    tm = _pick(n, (ROW_TILE, 128))
    per_b = n // tm
    mx = lambda k: pl.BlockSpec((None, 1, d), lambda bi, i: (bi, 0, k))
    mc = lambda k: pl.BlockSpec((None, 1, d), lambda bi, i: (b, 0, k))
    tok_spec = pl.BlockSpec((8, tm), lambda bi, i: (0, bi * per_b + i))
    return pl.pallas_call(
        functools.partial(_router_kernel, s_len=s_len, tm=tm, n_exp=n_exp),
        grid=(b, per_b),
        in_specs=[pl.BlockSpec((None, tm, d), lambda bi, i: (bi, i, 0)),
                  pl.BlockSpec((1, d), lambda bi, i: (0, 0)),
                  mx(k_shift), mx(k_scale), mc(k_shift), mc(k_scale),
                  pl.BlockSpec((n_exp, d), lambda bi, i: (0, 0)),
                  pl.BlockSpec((n_exp, 1), lambda bi, i: (0, 0))],
        out_specs=[pl.BlockSpec((None, tm, d), lambda bi, i: (bi, i, 0)),
                   tok_spec, tok_spec,
                   pl.BlockSpec((n_exp, HEAD_W), lambda bi, i: (0, 0))],
        out_shape=[jax.ShapeDtypeStruct((b, n, d), F32),
                   jax.ShapeDtypeStruct((8, b * n), jnp.int32),
                   jax.ShapeDtypeStruct((8, b * n), F32),
                   jax.ShapeDtypeStruct((n_exp, HEAD_W), jnp.int32)],
        scratch_shapes=[pltpu.VMEM((n_exp, HEAD_W), F32)],
        compiler_params=_cparams(2, VMEM_LIMIT),
        name="router",
    )(x1, g.reshape(1, d), mod3, mod3, mod3, mod3, router_w.T, router_b.reshape(n_exp, 1))


def _row_copy(src_ref, src_row, dst_ref, dst_row, sem):
    return pltpu.make_async_copy(src_ref.at[pl.ds(src_row, 1)], dst_ref.at[pl.ds(dst_row, 1)], sem)


def _dispatch_kernel(p0_ref, p1_ref, h_ref, xs_in_ref, xs_ref, sem, *, tm):
    del xs_in_ref
    base = pl.program_id(0) * tm

    def issue(r, carry):
        _row_copy(h_ref, r, xs_ref, p0_ref[base + r], sem).start()
        _row_copy(h_ref, r, xs_ref, p1_ref[base + r], sem).start()
        return carry

    lax.fori_loop(0, tm, issue, 0)

    def drain(r, carry):
        _row_copy(h_ref, 0, xs_ref, 0, sem).wait()
        _row_copy(h_ref, 0, xs_ref, 0, sem).wait()
        return carry

    lax.fori_loop(0, tm, drain, 0)


def _dispatch_call(h2, pos0, pos1, n_pad):
    t, d = h2.shape
    tm = _pick(t, (ROW_TILE, 128))
    grid_spec = pltpu.PrefetchScalarGridSpec(
        num_scalar_prefetch=2,
        grid=(t // tm,),
        in_specs=[pl.BlockSpec((tm, d), lambda i, p0, p1: (i, 0)),
                  pl.BlockSpec(memory_space=pl.ANY)],
        out_specs=pl.BlockSpec(memory_space=pl.ANY),
        scratch_shapes=[pltpu.SemaphoreType.DMA],
    )
    return pl.pallas_call(
        functools.partial(_dispatch_kernel, tm=tm),
        grid_spec=grid_spec,
        out_shape=jax.ShapeDtypeStruct((n_pad, d), F32),
        input_output_aliases={3: 0},
        compiler_params=_cparams(1, VMEM_LIMIT),
        name="dispatch",
    )(pos0, pos1, h2, jnp.zeros((n_pad, d), F32))


def _ffn_kernel(be_ref, nb_ref, x_ref, wg_ref, wu_ref, wd_ref, y_ref):
    blk = pl.program_id(0)

    @pl.when(blk < nb_ref[0])
    def _():
        x = x_ref[...].astype(BF16)
        gate = jnp.dot(x, wg_ref[...], preferred_element_type=F32)
        up = jnp.dot(x, wu_ref[...], preferred_element_type=F32)
        hid = (jax.nn.silu(gate) * up).astype(BF16)
        y_ref[...] = jnp.dot(hid, wd_ref[...], preferred_element_type=F32)

    @pl.when(blk >= nb_ref[0])
    def _():
        y_ref[...] = jnp.zeros_like(y_ref)


def _ffn_call(xs, blk_expert, n_used, wg, wu, wd, layer):
    n_pad, d = xs.shape
    d_exp = wg.shape[3]
    grid_spec = pltpu.PrefetchScalarGridSpec(
        num_scalar_prefetch=2,
        grid=(n_pad // MOE_BLOCK,),
        in_specs=[pl.BlockSpec((MOE_BLOCK, d), lambda i, be, nb: (i, 0)),
                  pl.BlockSpec((None, None, d, d_exp), lambda i, be, nb: (layer, be[i], 0, 0)),
                  pl.BlockSpec((None, None, d, d_exp), lambda i, be, nb: (layer, be[i], 0, 0)),
                  pl.BlockSpec((None, None, d_exp, d), lambda i, be, nb: (layer, be[i], 0, 0))],
        out_specs=pl.BlockSpec((MOE_BLOCK, d), lambda i, be, nb: (i, 0)),
    )
    return pl.pallas_call(
        _ffn_kernel,
        grid_spec=grid_spec,
        out_shape=jax.ShapeDtypeStruct((n_pad, d), F32),
        compiler_params=_cparams(1, VMEM_LIMIT),
        name="expert_ffn",
    )(blk_expert, n_used, xs, wg, wu, wd)


def _combine_kernel(p0_ref, p1_ref, ys_ref, x_ref, gt_ref, gx_ref, gc_ref, o_ref, y0_ref, y1_ref, sem,
                    *, s_len, n_len, tm):
    i = pl.program_id(0)
    base = i * tm

    def issue(r, carry):
        _row_copy(ys_ref, p0_ref[base + r], y0_ref, r, sem).start()
        _row_copy(ys_ref, p1_ref[base + r], y1_ref, r, sem).start()
        return carry

    lax.fori_loop(0, tm, issue, 0)

    def drain(r, carry):
        _row_copy(ys_ref, 0, y0_ref, 0, sem).wait()
        _row_copy(ys_ref, 0, y1_ref, 0, sem).wait()
        return carry

    lax.fori_loop(0, tm, drain, 0)

    row = (base % n_len) + lax.broadcasted_iota(jnp.int32, (tm, 1), 0)
    gate = jnp.where(row >= s_len, gc_ref[...], gx_ref[...])
    gt = gt_ref[...]
    y = gt[:, 0:1] * y0_ref[...] + gt[:, 1:2] * y1_ref[...]
    o_ref[...] = x_ref[...] + gate * y


def _combine_call(ys, pos0, pos1, x1, gates, mod3, s_len, k_gate):
    b, n, d = x1.shape
    t = b * n
    tm = _pick(n, (ROW_TILE, 128))
    grid_spec = pltpu.PrefetchScalarGridSpec(
        num_scalar_prefetch=2,
        grid=(t // tm,),
        in_specs=[pl.BlockSpec(memory_space=pl.ANY),
                  pl.BlockSpec((tm, d), lambda i, p0, p1: (i, 0)),
                  pl.BlockSpec((tm, TOP_K), lambda i, p0, p1: (i, 0)),
                  pl.BlockSpec((None, 1, d), lambda i, p0, p1: ((i * tm) // n, 0, k_gate)),
                  pl.BlockSpec((None, 1, d), lambda i, p0, p1: (b, 0, k_gate))],
        out_specs=pl.BlockSpec((tm, d), lambda i, p0, p1: (i, 0)),
        scratch_shapes=[pltpu.VMEM((tm, d), F32), pltpu.VMEM((tm, d), F32), pltpu.SemaphoreType.DMA],
    )
    out = pl.pallas_call(
        functools.partial(_combine_kernel, s_len=s_len, n_len=n, tm=tm),
        grid_spec=grid_spec,
        out_shape=jax.ShapeDtypeStruct((t, d), F32),
        compiler_params=_cparams(1, VMEM_LIMIT),
        name="combine",
    )(pos0, pos1, ys, x1.reshape(t, d), gates, mod3, mod3)
    return out.reshape(b, n, d)


def _moe(x1, g2, mod3, router_w, router_b, wg, wu, wd, layer, s_len):
    b, n, d = x1.shape
    t = b * n
    n_exp = router_w.shape[1]
    h2, ri, rf, cnt = _router_call(x1, g2, mod3, router_w, router_b, s_len, 3, 4)
    counts = cnt[:, 0]
    n_blk = (counts + MOE_BLOCK - 1) // MOE_BLOCK
    blk_end = jnp.cumsum(n_blk)
    starts = (blk_end - n_blk) * MOE_BLOCK
    pos0 = (starts[ri[0]] + ri[2]).astype(jnp.int32)
    pos1 = (starts[ri[1]] + ri[3]).astype(jnp.int32)
    n_pad = t * TOP_K + n_exp * MOE_BLOCK
    n_used = blk_end[-1:].astype(jnp.int32)
    blk_ids = jnp.minimum(jnp.arange(n_pad // MOE_BLOCK, dtype=jnp.int32), n_used[0] - 1)
    blk_expert = jnp.sum(blk_ids[:, None] >= blk_end[None, :], axis=1).astype(jnp.int32)
    xs = _dispatch_call(h2.reshape(t, d), pos0, pos1, n_pad)
    ys = _ffn_call(xs, blk_expert, n_used, wg, wu, wd, layer)
    return _combine_call(ys, pos0, pos1, x1, rf[:TOP_K].T, mod3, s_len, 5)


def _final_norm_kernel(x_ref, g_ref, o_ref):
    x = x_ref[...]
    o_ref[...] = x * lax.rsqrt(jnp.mean(x * x, axis=-1, keepdims=True) + EPS) * g_ref[...]


def _final_norm(xc, g, s_len):
    b, _, d = xc.shape
    tm = _pick(s_len, (ROW_TILE, 128))
    return pl.pallas_call(
        _final_norm_kernel,
        grid=(b, s_len // tm),
        in_specs=[pl.BlockSpec((None, tm, d), lambda bi, i: (bi, i, 0)),
                  pl.BlockSpec((1, d), lambda bi, i: (0, 0))],
        out_specs=pl.BlockSpec((None, tm, d), lambda bi, i: (bi, i, 0)),
        out_shape=jax.ShapeDtypeStruct((b, s_len, d), F32),
        compiler_params=_cparams(2, VMEM_LIMIT),
        name="final_norm",
    )(xc, g.reshape(1, d))


def kernel(x, c, ctx, c_ctx, w_ada, b_ada, g_norm1, g_norm2, w_in, w_out, na_rpb, da_lambda, da_subln_g,
           pool_w, pool_scale, sg_norm_g, sg_w, sg_b, router_w, router_b, exp_w_gate, exp_w_up, exp_w_down,
           g_final):
    b, s_len, d = x.shape
    c_len = ctx.shape[1]
    n = s_len + c_len
    depth = w_ada.shape[0]
    assert b + 1 <= 8 and w_in.shape[2] == ATT_COLS + REST_COLS and w_out.shape[1] == 4 * MIX_W

    cond = jnp.zeros((8, d), F32).at[:b].set(c).at[b].set(c_ctx)
    mod_all = _ada_all(cond, w_ada, b_ada)
    cos, sin = _rope_tables(s_len, c_len)
    xc = jnp.concatenate([x, ctx], axis=1)
    wg, wu, wd = exp_w_gate.astype(BF16), exp_w_up.astype(BF16), exp_w_down.astype(BF16)

    for l in range(depth):
        mod3 = mod_all[l].reshape(8, 1, N_MOD * d)
        h1 = _norm_mod_call(xc, g_norm1[l], mod3, s_len, 0, 1).reshape(b * n, d)
        pxa = _matmul_cols(h1, w_in, l, 0, ATT_COLS, BF16, 1024).reshape(b, n, ATT_COLS)
        pxb = _matmul_cols(h1, w_in, l, ATT_COLS, REST_COLS, F32, 768).reshape(b, n, REST_COLS)
        qt, kr, vt = _rope_call(pxa, cos, sin)
        mix_a = _na_call(pxa, na_rpb[l], s_len, c_len)
        mix_b = _da_call(qt, kr, vt, da_lambda[l], da_subln_g[l], s_len, c_len, l)
        mix_c = _pool_call(pxb, pool_w[l], pool_scale[l], s_len, c_len)
        mix_d = _sg_call(pxb, sg_norm_g[l], sg_w[l], sg_b[l])
        x1 = _out_proj((mix_a, mix_b, mix_c, mix_d), w_out, l, xc, mod3, s_len, 2)
        xc = _moe(x1, g_norm2[l], mod3, router_w, router_b, wg, wu, wd, l, s_len)
    return _final_norm(xc, g_final, s_len)
```

```python
import functools
import math

import numpy as np
import jax
import jax.numpy as jnp
from jax import lax
from jax.experimental import pallas as pl
from jax.experimental.pallas import tpu as pltpu

GRID_W = 64
NA_HEADS = 4
NA_DIM = 128
NA_WIN_H = 8
NA_WIN_W = 16
DA_HEADS = 4
DA_DIM = 64
DA_VDIM = 2 * DA_DIM
ROPE_BASE = 10000.0
POOL_GROUPS = 4
POOL_WIDTH = 128
POOL_WINDOWS = (2, 4, 8, 16)
SG_GROUPS = 4
SG_WIDTH = 128
SG_CHUNK = 128
N_EXPERT_GROUPS = 4
EXPERTS_PER_GROUP = 4
TOP_K = 2
N_MOD = 6
EPS = 1e-6
NEG_INF = -1e30
LOG2_E = math.log2(math.e)

HEAD_W = 128
MIX_W = 512
ATT_COLS = 6 * MIX_W
REST_COLS = 3 * MIX_W
MOE_BLOCK = 256
NA_Q_ROWS = 4
ROW_TILE = 256
DA_KEY_CHUNKS = (1408, 768, 512, 256, 128)
DA_VT_ROWS = DA_VDIM + 16
VMEM_LIMIT = 56 * 1024 * 1024

F32 = jnp.float32
BF16 = jnp.bfloat16


def _cparams(n_axes, vmem=None):
    return pltpu.CompilerParams(dimension_semantics=("arbitrary",) * n_axes,
                                vmem_limit_bytes=vmem)


def _pick(n, candidates):
    for c in candidates:
        if n % c == 0:
            return c
    raise ValueError(f"no tile in {candidates} divides {n}")


def _ada_kernel(c_ref, w_ref, b_ref, o_ref):
    cs = jax.nn.silu(c_ref[...]).astype(BF16)
    o_ref[...] = jnp.dot(cs, w_ref[...].astype(BF16), preferred_element_type=F32) + b_ref[...]


def _ada_all(cond, w_ada, b_ada):
    depth, d, n6 = w_ada.shape
    tn = _pick(n6, (1024, 512, 256, 128))
    return pl.pallas_call(
        _ada_kernel,
        grid=(depth, n6 // tn),
        in_specs=[pl.BlockSpec((8, d), lambda l, j: (0, 0)),
                  pl.BlockSpec((None, d, tn), lambda l, j: (l, 0, j)),
                  pl.BlockSpec((None, 1, tn), lambda l, j: (l, 0, j))],
        out_specs=pl.BlockSpec((None, 8, tn), lambda l, j: (l, 0, j)),
        out_shape=jax.ShapeDtypeStruct((depth, 8, n6), F32),
        compiler_params=_cparams(2, VMEM_LIMIT),
        name="adaln",
    )(cond, w_ada, b_ada.reshape(depth, 1, n6))


def _row_is_ctx(i, tm, s_len):
    row = i * tm + lax.broadcasted_iota(jnp.int32, (tm, 1), 0)
    return row >= s_len


def _norm_mod(x, g, shx, scx, shc, scc, is_ctx):
    y = x * lax.rsqrt(jnp.mean(x * x, axis=-1, keepdims=True) + EPS) * g
    shift = jnp.where(is_ctx, shc, shx)
    scale = jnp.where(is_ctx, scc, scx)
    return y * (1.0 + scale) + shift


def _norm_mod_kernel(x_ref, g_ref, shx_ref, scx_ref, shc_ref, scc_ref, o_ref, *, s_len, tm):
    is_ctx = _row_is_ctx(pl.program_id(1), tm, s_len)
    h = _norm_mod(x_ref[...], g_ref[...], shx_ref[...], scx_ref[...], shc_ref[...], scc_ref[...], is_ctx)
    o_ref[...] = h.astype(o_ref.dtype)


def _norm_mod_call(xc, g, mod3, s_len, k_shift, k_scale):
    b, n, d = xc.shape
    tm = _pick(n, (ROW_TILE, 128))
    mx = lambda k: pl.BlockSpec((None, 1, d), lambda bi, i: (bi, 0, k))
    mc = lambda k: pl.BlockSpec((None, 1, d), lambda bi, i: (b, 0, k))
    return pl.pallas_call(
        functools.partial(_norm_mod_kernel, s_len=s_len, tm=tm),
        grid=(b, n // tm),
        in_specs=[pl.BlockSpec((None, tm, d), lambda bi, i: (bi, i, 0)),
                  pl.BlockSpec((1, d), lambda bi, i: (0, 0)),
                  mx(k_shift), mx(k_scale), mc(k_shift), mc(k_scale)],
        out_specs=pl.BlockSpec((None, tm, d), lambda bi, i: (bi, i, 0)),
        out_shape=jax.ShapeDtypeStruct((b, n, d), BF16),
        compiler_params=_cparams(2, VMEM_LIMIT),
        name="norm_mod",
    )(xc, g.reshape(1, d), mod3, mod3, mod3, mod3)


def _mm_kernel(a_ref, w_ref, o_ref, wbf_ref):
    @pl.when(pl.program_id(1) == 0)
    def _():
        wbf_ref[...] = w_ref[...].astype(BF16)

    o_ref[...] = jnp.dot(a_ref[...], wbf_ref[...], preferred_element_type=F32).astype(o_ref.dtype)


def _matmul_cols(a, w, layer, col0, ncols, out_dtype, tn):
    m, k = a.shape
    tm = _pick(m, (768, 512, 384, 256, 128))
    off = col0 // tn
    return pl.pallas_call(
        _mm_kernel,
        grid=(ncols // tn, m // tm),
        in_specs=[pl.BlockSpec((tm, k), lambda j, i: (i, 0)),
                  pl.BlockSpec((None, k, tn), lambda j, i: (layer, 0, j + off))],
        out_specs=pl.BlockSpec((tm, tn), lambda j, i: (i, j)),
        out_shape=jax.ShapeDtypeStruct((m, ncols), out_dtype),
        scratch_shapes=[pltpu.VMEM((k, tn), BF16)],
        compiler_params=_cparams(2, VMEM_LIMIT),
        name="in_proj",
    )(a, w)


def _rope_tables(s_len, c_len):
    t = jnp.arange(s_len)
    row = (t // GRID_W).astype(F32)
    col = (t % GRID_W).astype(F32)
    n_freq = DA_DIM // 4
    inv_freq = 1.0 / (ROPE_BASE ** (jnp.arange(n_freq, dtype=F32) / n_freq))
    ang = jnp.concatenate([row[:, None] * inv_freq, col[:, None] * inv_freq], axis=-1)
    cos = jnp.concatenate([jnp.cos(ang), jnp.ones((c_len, DA_DIM // 2), F32)], axis=0)
    sin = jnp.concatenate([jnp.sin(ang), jnp.zeros((c_len, DA_DIM // 2), F32)], axis=0)
    return jnp.tile(cos, (1, 4)), jnp.tile(jnp.concatenate([-sin, sin], axis=-1), (1, 2))


def _rope_kernel(q_ref, k_ref, v_ref, c_ref, s_ref, qt_ref, kr_ref, vt_ref, *, tm, q_scale):
    cos = c_ref[...]
    sin = s_ref[...]
    lane = lax.broadcasted_iota(jnp.int32, (tm, HEAD_W), 1)
    first_half = (lane % DA_DIM) < (DA_DIM // 2)

    def rotate(x):
        swapped = jnp.where(first_half, pltpu.roll(x, HEAD_W - DA_DIM // 2, 1), pltpu.roll(x, DA_DIM // 2, 1))
        return x * cos + swapped * sin

    for h in range(DA_HEADS):
        cols = slice(h * HEAD_W, (h + 1) * HEAD_W)
        q = rotate(q_ref[:, cols].astype(F32)) * q_scale
        qt_ref[cols, :] = q.T.astype(qt_ref.dtype)
        kr_ref[:, cols] = rotate(k_ref[:, cols].astype(F32)).astype(kr_ref.dtype)
        vt_ref[h * DA_VT_ROWS:h * DA_VT_ROWS + DA_VDIM, :] = v_ref[:, cols].astype(F32).T.astype(vt_ref.dtype)
        pad = lax.broadcasted_iota(jnp.int32, (DA_VT_ROWS - DA_VDIM, tm), 0)
        vt_ref[h * DA_VT_ROWS + DA_VDIM:(h + 1) * DA_VT_ROWS, :] = jnp.where(pad == 0, 1.0, 0.0).astype(vt_ref.dtype)


def _rope_call(pxa, cos, sin):
    b, n, _ = pxa.shape
    tm = _pick(n, (ROW_TILE, 128))
    col_spec = lambda k: pl.BlockSpec((None, tm, MIX_W), lambda bi, i: (bi, i, k))
    t_spec = lambda rows: pl.BlockSpec((None, rows, tm), lambda bi, i: (bi, 0, i))
    return pl.pallas_call(
        functools.partial(_rope_kernel, tm=tm, q_scale=DA_DIM ** -0.5 * LOG2_E),
        grid=(b, n // tm),
        in_specs=[col_spec(3), col_spec(4), col_spec(5),
                  pl.BlockSpec((tm, HEAD_W), lambda bi, i: (i, 0)),
                  pl.BlockSpec((tm, HEAD_W), lambda bi, i: (i, 0))],
        out_specs=[t_spec(MIX_W), col_spec(0), t_spec(DA_HEADS * DA_VT_ROWS)],
        out_shape=[jax.ShapeDtypeStruct((b, MIX_W, n), BF16),
                   jax.ShapeDtypeStruct((b, n, MIX_W), BF16),
                   jax.ShapeDtypeStruct((b, DA_HEADS * DA_VT_ROWS, n), BF16)],
        compiler_params=_cparams(2, VMEM_LIMIT),
        name="rope",
    )(pxa, pxa, pxa, cos, sin)


def _na_plan(rows, r_q):
    kh = min(NA_WIN_H, rows)
    key_rows = r_q + kh - 1
    assert rows % r_q == 0 and rows >= key_rows
    patterns, var, ustart = [], [], []
    for j in range(rows // r_q):
        r = r_q * j + np.arange(r_q)
        rs = np.clip(r - kh // 2, 0, rows - kh)
        u = int(np.clip(rs.min(), 0, rows - key_rows))
        kr = u + np.arange(key_rows)
        valid = (kr[None, :] >= rs[:, None]) & (kr[None, :] < rs[:, None] + kh)
        assert (valid.sum(1) == kh).all()
        dr0 = u - r + (NA_WIN_H - 1)
        key = (valid.tobytes(), dr0.tobytes())
        for v, (k2, _, _) in enumerate(patterns):
            if k2 == key:
                break
        else:
            v = len(patterns)
            patterns.append((key, valid, dr0))
        var.append(v)
        ustart.append(u * GRID_W)
    valid = np.stack([p[1] for p in patterns])
    dr0 = np.stack([p[2] for p in patterns])
    return np.asarray(var, np.int32), np.asarray(ustart, np.int32), valid, dr0


def _na_bias_tables(rpb, valid, dr0):
    col = np.arange(GRID_W)
    cs = np.clip(col - NA_WIN_W // 2, 0, GRID_W - NA_WIN_W)
    cmask = (col[None, :] >= cs[:, None]) & (col[None, :] < cs[:, None] + NA_WIN_W)
    dc = np.clip(col[None, :] - col[:, None], -(NA_WIN_W - 1), NA_WIN_W - 1) + (NA_WIN_W - 1)
    heads, n_dr, n_dc = rpb.shape
    v, r_q, key_rows = valid.shape
    pick = (dc.reshape(1, -1) == np.arange(n_dc)[:, None]).astype(np.float32)
    by_col = jnp.dot(rpb.astype(F32).reshape(heads * n_dr, n_dc), pick,
                     precision=lax.Precision.HIGHEST).reshape(heads, n_dr, GRID_W, GRID_W)
    padded = jnp.pad(by_col, ((0, 0), (key_rows, key_rows), (0, 0), (0, 0)))
    tab = jnp.stack([lax.slice_in_dim(padded, int(d) + key_rows, int(d) + 2 * key_rows, axis=1)
                     for d in dr0.reshape(-1)], axis=1)
    tab = tab.reshape(heads, v, r_q, key_rows, GRID_W, GRID_W).transpose(0, 1, 2, 4, 3, 5)
    mask = valid[:, :, None, :, None] & cmask[None, None, :, None, :]
    tab = jnp.where(mask[None], tab, NEG_INF)
    return tab.reshape(heads, v, r_q * GRID_W, key_rows * GRID_W)


def _na_kernel(var_ref, u_ref, q_ref, k_ref, v_ref, kc_ref, vc_ref, bias_ref, o_ref, *, n_lat, kw, scale):
    j = pl.program_id(2)
    q = q_ref[...]
    nt = (((1,), (1,)), ((), ()))
    s_ctx = lax.dot_general(q, kc_ref[...], nt, preferred_element_type=F32) * scale

    @pl.when(j < n_lat)
    def _():
        u = pl.multiple_of(u_ref[j], GRID_W)
        k_win = k_ref[pl.ds(u, kw), :]
        v_win = v_ref[pl.ds(u, kw), :]
        s = lax.dot_general(q, k_win, nt, preferred_element_type=F32) * scale + bias_ref[...]
        m = jnp.maximum(jnp.max(s, axis=-1, keepdims=True), jnp.max(s_ctx, axis=-1, keepdims=True))
        p = jnp.exp(s - m)
        pc = jnp.exp(s_ctx - m)
        l = jnp.sum(p, axis=-1, keepdims=True) + jnp.sum(pc, axis=-1, keepdims=True)
        o = jnp.dot(p.astype(BF16), v_win, preferred_element_type=F32)
        o += jnp.dot(pc.astype(BF16), vc_ref[...], preferred_element_type=F32)
        o_ref[...] = (o / l).astype(o_ref.dtype)

    @pl.when(j >= n_lat)
    def _():
        m = jnp.max(s_ctx, axis=-1, keepdims=True)
        pc = jnp.exp(s_ctx - m)
        l = jnp.sum(pc, axis=-1, keepdims=True)
        o = jnp.dot(pc.astype(BF16), vc_ref[...], preferred_element_type=F32)
        o_ref[...] = (o / l).astype(o_ref.dtype)


def _na_call(pxa, rpb, s_len, c_len):
    b, n, _ = pxa.shape
    rows = s_len // GRID_W
    tq = NA_Q_ROWS * GRID_W
    assert s_len % tq == 0 and c_len % tq == 0 and s_len % c_len == 0
    var, ustart, valid, dr0 = _na_plan(rows, NA_Q_ROWS)
    n_lat, n_q = s_len // tq, n // tq
    kw = valid.shape[2] * GRID_W
    tabs = _na_bias_tables(rpb, valid, dr0)
    pad = np.zeros(n_q - n_lat, np.int32)
    var = jnp.asarray(np.concatenate([var, pad]))
    ustart = jnp.asarray(np.concatenate([ustart, pad]))
    ctx_blk = s_len // c_len
    grid_spec = pltpu.PrefetchScalarGridSpec(
        num_scalar_prefetch=2,
        grid=(b, NA_HEADS, n_q),
        in_specs=[pl.BlockSpec((None, tq, HEAD_W), lambda bi, h, j, vr, ur: (bi, j, h)),
                  pl.BlockSpec((None, s_len, HEAD_W), lambda bi, h, j, vr, ur: (bi, 0, 4 + h)),
                  pl.BlockSpec((None, s_len, HEAD_W), lambda bi, h, j, vr, ur: (bi, 0, 8 + h)),
                  pl.BlockSpec((None, c_len, HEAD_W), lambda bi, h, j, vr, ur: (bi, ctx_blk, 4 + h)),
                  pl.BlockSpec((None, c_len, HEAD_W), lambda bi, h, j, vr, ur: (bi, ctx_blk, 8 + h)),
                  pl.BlockSpec((None, None, tq, kw), lambda bi, h, j, vr, ur: (h, vr[j], 0, 0))],
        out_specs=pl.BlockSpec((None, tq, HEAD_W), lambda bi, h, j, vr, ur: (bi, j, h)),
    )
    return pl.pallas_call(
        functools.partial(_na_kernel, n_lat=n_lat, kw=kw, scale=NA_DIM ** -0.5),
        grid_spec=grid_spec,
        out_shape=jax.ShapeDtypeStruct((b, n, MIX_W), BF16),
        compiler_params=_cparams(3, VMEM_LIMIT),
        name="nbr_attn",
    )(var, ustart, pxa, pxa, pxa, pxa, pxa, tabs)


def _da_kernel(qt_ref, k_ref, vt_ref, lam_ref, g_ref, o_ref, sa_ref, sb_ref, *, s_len, c_len, tq, tk, lam_init):
    qi = pl.program_id(2)
    n = s_len + c_len
    feat = lax.broadcasted_iota(jnp.int32, (2 * DA_DIM, tq), 0)

    def q_tile(idx):
        start = pl.multiple_of(idx * tq, tq)
        qt = qt_ref[:, pl.ds(start, tq)]
        zero = jnp.zeros_like(qt)
        return jnp.concatenate([jnp.where(feat < DA_DIM, qt, zero), jnp.where(feat >= DA_DIM, qt, zero)], axis=1)

    q2 = q_tile(qi)

    def scores(k_blk, q):
        return jnp.dot(k_blk, q, preferred_element_type=F32)

    def update(s, vt_blk, carry):
        m, acc = carry
        m_new = jnp.maximum(m, jnp.max(s, axis=0, keepdims=True))
        alpha = jnp.exp2(m - m_new)
        p = jnp.exp2(s - m_new)
        acc_new = alpha * acc + jnp.dot(vt_blk, p.astype(BF16), preferred_element_type=F32)
        return m_new, acc_new

    def init():
        return jnp.full((1, 2 * tq), NEG_INF, F32), jnp.zeros((DA_VT_ROWS, 2 * tq), F32)

    def finalize(carry):
        lv = lam_ref[...]
        t1 = jnp.sum(lv[0:1, :] * lv[1:2, :], axis=-1, keepdims=True)
        t2 = jnp.sum(lv[2:3, :] * lv[3:4, :], axis=-1, keepdims=True)
        lam = jnp.exp(t1) - jnp.exp(t2) + lam_init
        _, acc = carry
        o = acc[:DA_VDIM, :] / acc[DA_VDIM:DA_VDIM + 1, :]
        o = o[:, :tq] - lam * o[:, tq:]
        y = o * lax.rsqrt(jnp.mean(o * o, axis=0, keepdims=True) + EPS) * g_ref[...]
        o_ref[...] = (y * (1.0 - lam_init)).T.astype(o_ref.dtype)

    n_steps = n // tk
    full_pairs = (n_steps - 1) // 2
    carry_first = n_steps % 2 == 0

    @pl.when(qi < s_len // tq)
    def _():
        def scores_into(dst_ref, step, q):
            start = pl.multiple_of(step * tk, tk)
            dst_ref[...] = scores(k_ref[pl.ds(start, tk), :], q)

        def update_from(src_ref, step, stats):
            start = pl.multiple_of(step * tk, tk)
            return update(src_ref[...], vt_ref[:, pl.ds(start, tk)], stats)

        def pair(j, stats):
            c0 = 2 * j
            scores_into(sb_ref, c0 + 1, q2)
            stats = update_from(sa_ref, c0, stats)
            scores_into(sa_ref, c0 + 2, q2)
            return update_from(sb_ref, c0 + 1, stats)

        if carry_first:
            @pl.when(qi == 0)
            def _():
                scores_into(sa_ref, 0, q2)
        else:
            scores_into(sa_ref, 0, q2)
        stats = lax.fori_loop(0, full_pairs, pair, init())
        c0 = 2 * full_pairs
        if carry_first:
            scores_into(sb_ref, c0 + 1, q2)
            stats = update_from(sa_ref, c0, stats)
            scores_into(sa_ref, 0, q_tile(qi + 1))
            stats = update_from(sb_ref, c0 + 1, stats)
        else:
            stats = update_from(sa_ref, c0, stats)
        finalize(stats)

    @pl.when(qi >= s_len // tq)
    def _():
        finalize(update(scores(k_ref[s_len:n, :], q2), vt_ref[:, s_len:n], init()))


def _da_call(qt, kr, vt, lam_vecs, subln_g, s_len, c_len, layer):
    b, n, _ = kr.shape
    tq = _pick(math.gcd(s_len, c_len), (256, 128))
    tk = _pick(n, DA_KEY_CHUNKS)
    lam_init = 0.8 - 0.6 * math.exp(-0.3 * layer)
    return pl.pallas_call(
        functools.partial(_da_kernel, s_len=s_len, c_len=c_len, tq=tq, tk=tk, lam_init=lam_init),
        grid=(b, DA_HEADS, n // tq),
        in_specs=[pl.BlockSpec((None, HEAD_W, n), lambda bi, h, i: (bi, h, 0)),
                  pl.BlockSpec((None, n, HEAD_W), lambda bi, h, i: (bi, 0, h)),
                  pl.BlockSpec((None, DA_VT_ROWS, n), lambda bi, h, i: (bi, h, 0)),
                  pl.BlockSpec((4, DA_DIM), lambda bi, h, i: (0, 0)),
                  pl.BlockSpec((DA_VDIM, 1), lambda bi, h, i: (0, 0))],
        out_specs=pl.BlockSpec((None, tq, HEAD_W), lambda bi, h, i: (bi, i, h)),
        out_shape=jax.ShapeDtypeStruct((b, n, MIX_W), BF16),
        scratch_shapes=[pltpu.VMEM((tk, 2 * tq), F32), pltpu.VMEM((tk, 2 * tq), F32)],
        compiler_params=_cparams(3, VMEM_LIMIT),
        name="diff_attn",
    )(qt, kr, vt, lam_vecs, subln_g.reshape(DA_VDIM, 1))


POOL_HALO = 8


def _pool_kernel(prev_ref, cur_ref, next_ref, w_ref, sc_ref, o_ref, buf_ref, *, s_len, c_len, tm):
    i = pl.program_id(1)
    n_lat = s_len // tm
    n_all = (s_len + c_len) // tm
    seq_start = (i == 0) | (i == n_lat)
    seq_end = (i == n_lat - 1) | (i == n_all - 1)
    buf_ref[0:POOL_HALO, :] = jnp.where(seq_start, 0.0, prev_ref[...])
    buf_ref[POOL_HALO:POOL_HALO + tm, :] = cur_ref[...]
    buf_ref[POOL_HALO + tm:, :] = jnp.where(seq_end, 0.0, next_ref[...])
    in_ctx = i >= n_lat
    seq_len = jnp.where(in_ctx, c_len, s_len)
    pos = i * tm - jnp.where(in_ctx, s_len, 0) + lax.broadcasted_iota(jnp.int32, (tm, 1), 0)
    for g, win in enumerate(POOL_WINDOWS):
        half = win // 2
        cols = slice(g * POOL_WIDTH, (g + 1) * POOL_WIDTH)
        tot = buf_ref[POOL_HALO - half:POOL_HALO - half + tm, cols]
        for d in range(-half + 1, half):
            tot = tot + buf_ref[POOL_HALO + d:POOL_HALO + d + tm, cols]
        cnt = jnp.clip(pos + half, 0, seq_len) - jnp.clip(pos - half, 0, seq_len)
        resid = tot / cnt.astype(F32) - cur_ref[:, cols]
        y = jnp.dot(resid.astype(BF16), w_ref[g].astype(BF16), preferred_element_type=F32)
        o_ref[:, cols] = (y * sc_ref[:, cols]).astype(o_ref.dtype)


def _pool_call(pxb, pool_w, pool_scale, s_len, c_len):
    b, n, _ = pxb.shape
    tm = _pick(math.gcd(s_len, c_len), (ROW_TILE, 128))
    hb = tm // POOL_HALO
    last = n // POOL_HALO - 1
    return pl.pallas_call(
        functools.partial(_pool_kernel, s_len=s_len, c_len=c_len, tm=tm),
        grid=(b, n // tm),
        in_specs=[pl.BlockSpec((None, POOL_HALO, MIX_W), lambda bi, i: (bi, jnp.maximum(i * hb - 1, 0), 0)),
                  pl.BlockSpec((None, tm, MIX_W), lambda bi, i: (bi, i, 0)),
                  pl.BlockSpec((None, POOL_HALO, MIX_W), lambda bi, i: (bi, jnp.minimum((i + 1) * hb, last), 0)),
                  pl.BlockSpec((POOL_GROUPS, POOL_WIDTH, POOL_WIDTH), lambda bi, i: (0, 0, 0)),
                  pl.BlockSpec((1, MIX_W), lambda bi, i: (0, 0))],
        out_specs=pl.BlockSpec((None, tm, MIX_W), lambda bi, i: (bi, i, 0)),
        out_shape=jax.ShapeDtypeStruct((b, n, MIX_W), BF16),
        scratch_shapes=[pltpu.VMEM((tm + 2 * POOL_HALO, MIX_W), F32)],
        compiler_params=_cparams(2, VMEM_LIMIT),
        name="pool",
    )(pxb, pxb, pxb, pool_w, pool_scale.reshape(1, MIX_W))


def _sg_kernel(u_ref, v_ref, g_ref, w_ref, b_ref, o_ref, *, tm):
    v = jax.nn.gelu(v_ref[...])
    vn = (v * lax.rsqrt(jnp.mean(v * v, axis=-1, keepdims=True) + EPS) * g_ref[...]).astype(BF16)
    bias = b_ref[...]
    for c in range(tm // SG_CHUNK):
        rows = slice(c * SG_CHUNK, (c + 1) * SG_CHUNK)
        for g in range(SG_GROUPS):
            cols = slice(g * SG_WIDTH, (g + 1) * SG_WIDTH)
            mixed = jnp.dot(w_ref[g].astype(BF16), vn[rows, cols], preferred_element_type=F32)
            mixed = mixed + bias[:, g:g + 1]
            o_ref[rows, cols] = (jax.nn.gelu(u_ref[rows, cols]) * mixed).astype(o_ref.dtype)


def _sg_call(pxb, sg_norm_g, sg_w, sg_b):
    b, n, _ = pxb.shape
    tm = _pick(n, (ROW_TILE, 128))
    return pl.pallas_call(
        functools.partial(_sg_kernel, tm=tm),
        grid=(b, n // tm),
        in_specs=[pl.BlockSpec((None, tm, MIX_W), lambda bi, i: (bi, i, 1)),
                  pl.BlockSpec((None, tm, MIX_W), lambda bi, i: (bi, i, 2)),
                  pl.BlockSpec((1, MIX_W), lambda bi, i: (0, 0)),
                  pl.BlockSpec((SG_GROUPS, SG_CHUNK, SG_CHUNK), lambda bi, i: (0, 0, 0)),
                  pl.BlockSpec((SG_CHUNK, SG_GROUPS), lambda bi, i: (0, 0))],
        out_specs=pl.BlockSpec((None, tm, MIX_W), lambda bi, i: (bi, i, 0)),
        out_shape=jax.ShapeDtypeStruct((b, n, MIX_W), BF16),
        compiler_params=_cparams(2, VMEM_LIMIT),
        name="spatial_gate",
    )(pxb, pxb, sg_norm_g.reshape(1, MIX_W), sg_w, sg_b.T)


def _first_argmax(vals):
    best, idx = vals[0], jnp.zeros(vals[0].shape, jnp.int32)
    for k in range(1, len(vals)):
        take = vals[k] > best
        best = jnp.where(take, vals[k], best)
        idx = jnp.where(take, k, idx)
    return idx, best


def _out_router_kernel(ma_ref, mb_ref, mc_ref, md_ref, w_ref, x_ref, gx_ref, gc_ref,
                       g_ref, shx_ref, scx_ref, shc_ref, scc_ref, rw_ref, rb_ref,
                       x1_ref, h_ref, ri_ref, rf_ref, cnt_ref, run_ref, *, s_len, tm, n_exp):
    bi, i = pl.program_id(0), pl.program_id(1)

    @pl.when((bi == 0) & (i == 0))
    def _():
        run_ref[...] = jnp.zeros_like(run_ref)

    mix = jnp.concatenate([ma_ref[...], mb_ref[...], mc_ref[...], md_ref[...]], axis=1)
    acc = jnp.dot(mix, w_ref[...], preferred_element_type=F32)
    is_ctx = _row_is_ctx(i, tm, s_len)
    x1 = x_ref[...] + jnp.where(is_ctx, gc_ref[...], gx_ref[...]) * acc
    x1_ref[...] = x1
    h = _norm_mod(x1, g_ref[...], shx_ref[...], scx_ref[...], shc_ref[...], scc_ref[...], is_ctx)
    h_ref[...] = h.astype(h_ref.dtype)

    h_hi = h.astype(BF16)
    h_lo = (h - h_hi.astype(F32)).astype(BF16)
    part = jnp.dot(h_hi, rw_ref[0], preferred_element_type=F32)
    part += jnp.dot(h_lo, rw_ref[1], preferred_element_type=F32)
    part_t = part.T
    logits = part_t[0:n_exp, :] + part_t[n_exp:2 * n_exp, :]
    scores = jax.nn.sigmoid(logits)
    biased = scores + rb_ref[...]
    b_rows = [biased[e:e + 1, :] for e in range(n_exp)]
    s_rows = [scores[e:e + 1, :] for e in range(n_exp)]
    epg = EXPERTS_PER_GROUP
    group_scores = []
    for g in range(N_EXPERT_GROUPS):
        r = b_rows[g * epg:(g + 1) * epg]
        pair = None
        for a in range(epg):
            for c in range(a + 1, epg):
                t = r[a] + r[c]
                pair = t if pair is None else jnp.maximum(pair, t)
        group_scores.append(pair)
    group, _ = _first_argmax(group_scores)
    in_group = []
    for k in range(epg):
        sel = b_rows[k]
        for g in range(1, N_EXPERT_GROUPS):
            sel = jnp.where(group == g, b_rows[g * epg + k], sel)
        in_group.append(sel)
    loc0, _ = _first_argmax(in_group)
    rest = [jnp.where(loc0 == k, -jnp.inf, in_group[k]) for k in range(epg)]
    loc1, _ = _first_argmax(rest)
    e0 = group * epg + loc0
    e1 = group * epg + loc1
    g0 = jnp.zeros_like(s_rows[0])
    g1 = jnp.zeros_like(s_rows[0])
    for e in range(n_exp):
        g0 = jnp.where(e0 == e, s_rows[e], g0)
        g1 = jnp.where(e1 == e, s_rows[e], g1)
    tot = g0 + g1

    eidx = lax.broadcasted_iota(jnp.int32, (n_exp, tm), 0)
    hit0 = eidx == e0
    hit1 = eidx == e1
    onehot = jnp.where(hit0 | hit1, 1.0, 0.0)
    before = lax.broadcasted_iota(jnp.int32, (tm, tm), 0) < lax.broadcasted_iota(jnp.int32, (tm, tm), 1)
    tri = jnp.where(before, 1.0, 0.0).astype(BF16)
    run = run_ref[:, 0:1]
    rank = jnp.dot(onehot.astype(BF16), tri, preferred_element_type=F32) + run
    rank0 = jnp.sum(jnp.where(hit0, rank, 0.0), axis=0, keepdims=True)
    rank1 = jnp.sum(jnp.where(hit1, rank, 0.0), axis=0, keepdims=True)
    new_run = run + jnp.sum(onehot, axis=1, keepdims=True)
    run_ref[...] = jnp.broadcast_to(new_run, run_ref.shape)
    cnt_ref[...] = jnp.broadcast_to(new_run, cnt_ref.shape).astype(jnp.int32)

    zi = jnp.zeros((4, tm), jnp.int32)
    ri_ref[...] = jnp.concatenate([e0, e1, rank0.astype(jnp.int32), rank1.astype(jnp.int32), zi], axis=0)
    zf = jnp.zeros((6, tm), F32)
    rf_ref[...] = jnp.concatenate([g0 / tot, g1 / tot, zf], axis=0)


def _router_slabs(router_w):
    d, n_exp = router_w.shape
    w_hi = router_w.astype(BF16)
    w_lo = (router_w - w_hi.astype(F32)).astype(BF16)
    zeros = lambda cols: jnp.zeros((d, cols), BF16)
    return jnp.stack([jnp.concatenate([w_hi, w_lo, zeros(HEAD_W - 2 * n_exp)], axis=1),
                      jnp.concatenate([w_hi, zeros(HEAD_W - n_exp)], axis=1)])


def _out_router_call(mixes, w_out_bf, layer, xc, g2, mod3, router_w, router_b, s_len):
    b, n, d = xc.shape
    n_exp = router_w.shape[1]
    tm = _pick(n, (ROW_TILE, 128))
    per_b = n // tm
    mx = lambda k: pl.BlockSpec((None, 1, d), lambda bi, i: (bi, 0, k))
    mc = lambda k: pl.BlockSpec((None, 1, d), lambda bi, i: (b, 0, k))
    mix_spec = pl.BlockSpec((None, tm, MIX_W), lambda bi, i: (bi, i, 0))
    row_spec = pl.BlockSpec((None, tm, d), lambda bi, i: (bi, i, 0))
    tok_spec = pl.BlockSpec((8, tm), lambda bi, i: (0, bi * per_b + i))
    return pl.pallas_call(
        functools.partial(_out_router_kernel, s_len=s_len, tm=tm, n_exp=n_exp),
        grid=(b, per_b),
        in_specs=[mix_spec, mix_spec, mix_spec, mix_spec,
                  pl.BlockSpec((None, 4 * MIX_W, d), lambda bi, i: (layer, 0, 0)),
                  row_spec, mx(2), mc(2),
                  pl.BlockSpec((1, d), lambda bi, i: (0, 0)),
                  mx(3), mx(4), mc(3), mc(4),
                  pl.BlockSpec((2, d, HEAD_W), lambda bi, i: (0, 0, 0)),
                  pl.BlockSpec((n_exp, 1), lambda bi, i: (0, 0))],
        out_specs=[row_spec, row_spec, tok_spec, tok_spec,
                   pl.BlockSpec((n_exp, HEAD_W), lambda bi, i: (0, 0))],
        out_shape=[jax.ShapeDtypeStruct((b, n, d), F32),
                   jax.ShapeDtypeStruct((b, n, d), F32),
                   jax.ShapeDtypeStruct((8, b * n), jnp.int32),
                   jax.ShapeDtypeStruct((8, b * n), F32),
                   jax.ShapeDtypeStruct((n_exp, HEAD_W), jnp.int32)],
        scratch_shapes=[pltpu.VMEM((n_exp, HEAD_W), F32)],
        compiler_params=_cparams(2, VMEM_LIMIT),
        name="out_router",
    )(*mixes, w_out_bf, xc, mod3, mod3, g2.reshape(1, d), mod3, mod3, mod3, mod3,
      _router_slabs(router_w), router_b.reshape(n_exp, 1))


def _row_copy(src_ref, src_row, dst_ref, dst_row, sem):
    return pltpu.make_async_copy(src_ref.at[pl.ds(src_row, 1)], dst_ref.at[pl.ds(dst_row, 1)], sem)


def _dispatch_kernel(p0_ref, p1_ref, h_ref, xs_in_ref, xs_ref, sem, *, tm):
    del xs_in_ref
    base = pl.program_id(0) * tm

    def issue(r, carry):
        _row_copy(h_ref, r, xs_ref, p0_ref[base + r], sem).start()
        _row_copy(h_ref, r, xs_ref, p1_ref[base + r], sem).start()
        return carry

    lax.fori_loop(0, tm, issue, 0)

    tile_copy = pltpu.make_async_copy(h_ref, xs_ref.at[pl.ds(0, tm)], sem)
    tile_copy.wait()
    tile_copy.wait()


def _dispatch_call(h2, pos0, pos1, n_pad):
    t, d = h2.shape
    tm = _pick(t, (ROW_TILE, 128))
    grid_spec = pltpu.PrefetchScalarGridSpec(
        num_scalar_prefetch=2,
        grid=(t // tm,),
        in_specs=[pl.BlockSpec((tm, d), lambda i, p0, p1: (i, 0)),
                  pl.BlockSpec(memory_space=pl.ANY)],
        out_specs=pl.BlockSpec(memory_space=pl.ANY),
        scratch_shapes=[pltpu.SemaphoreType.DMA],
    )
    return pl.pallas_call(
        functools.partial(_dispatch_kernel, tm=tm),
        grid_spec=grid_spec,
        out_shape=jax.ShapeDtypeStruct((n_pad, d), F32),
        input_output_aliases={3: 0},
        compiler_params=_cparams(1, VMEM_LIMIT),
        name="dispatch",
    )(pos0, pos1, h2, jnp.zeros((n_pad, d), F32))


def _ffn_kernel(be_ref, nb_ref, x_ref, wg_ref, wu_ref, wd_ref, y_ref):
    blk = pl.program_id(0)

    @pl.when(blk < nb_ref[0])
    def _():
        x = x_ref[...].astype(BF16)
        gate = jnp.dot(x, wg_ref[...], preferred_element_type=F32)
        up = jnp.dot(x, wu_ref[...], preferred_element_type=F32)
        hid = (jax.nn.silu(gate) * up).astype(BF16)
        y_ref[...] = jnp.dot(hid, wd_ref[...], preferred_element_type=F32)

    @pl.when(blk >= nb_ref[0])
    def _():
        y_ref[...] = jnp.zeros_like(y_ref)


def _ffn_call(xs, blk_expert, n_used, wg, wu, wd, layer):
    n_pad, d = xs.shape
    d_exp = wg.shape[3]
    grid_spec = pltpu.PrefetchScalarGridSpec(
        num_scalar_prefetch=2,
        grid=(n_pad // MOE_BLOCK,),
        in_specs=[pl.BlockSpec((MOE_BLOCK, d), lambda i, be, nb: (i, 0)),
                  pl.BlockSpec((None, None, d, d_exp), lambda i, be, nb: (layer, be[i], 0, 0)),
                  pl.BlockSpec((None, None, d, d_exp), lambda i, be, nb: (layer, be[i], 0, 0)),
                  pl.BlockSpec((None, None, d_exp, d), lambda i, be, nb: (layer, be[i], 0, 0))],
        out_specs=pl.BlockSpec((MOE_BLOCK, d), lambda i, be, nb: (i, 0)),
    )
    return pl.pallas_call(
        _ffn_kernel,
        grid_spec=grid_spec,
        out_shape=jax.ShapeDtypeStruct((n_pad, d), F32),
        compiler_params=_cparams(1, VMEM_LIMIT),
        name="expert_ffn",
    )(blk_expert, n_used, xs, wg, wu, wd)


def _combine_kernel(p0_ref, p1_ref, ys_ref, x_ref, gt_ref, gx_ref, gc_ref, *rest, s_len, n_len, tm, with_next):
    if with_next:
        g_ref, shx_ref, scx_ref, shc_ref, scc_ref, o_ref, h_ref, y0_ref, y1_ref, sem = rest
    else:
        o_ref, y0_ref, y1_ref, sem = rest
    base = pl.program_id(0) * tm

    def issue(r, carry):
        _row_copy(ys_ref, p0_ref[base + r], y0_ref, r, sem).start()
        _row_copy(ys_ref, p1_ref[base + r], y1_ref, r, sem).start()
        return carry

    lax.fori_loop(0, tm, issue, 0)
    pltpu.make_async_copy(ys_ref.at[pl.ds(0, tm)], y0_ref, sem).wait()
    pltpu.make_async_copy(ys_ref.at[pl.ds(0, tm)], y1_ref, sem).wait()

    row = (base % n_len) + lax.broadcasted_iota(jnp.int32, (tm, 1), 0)
    is_ctx = row >= s_len
    gate = jnp.where(is_ctx, gc_ref[...], gx_ref[...])
    gt = gt_ref[...]
    y = gt[:, 0:1] * y0_ref[...] + gt[:, 1:2] * y1_ref[...]
    x_new = x_ref[...] + gate * y
    o_ref[...] = x_new
    if with_next:
        h = _norm_mod(x_new, g_ref[...], shx_ref[...], scx_ref[...], shc_ref[...], scc_ref[...], is_ctx)
        h_ref[...] = h.astype(h_ref.dtype)


def _combine_call(ys, pos0, pos1, x1, gates, mod3, s_len, k_gate, next_norm=None):
    b, n, d = x1.shape
    t = b * n
    tm = _pick(n, (ROW_TILE, 128))
    row_spec = pl.BlockSpec((tm, d), lambda i, p0, p1: (i, 0))
    mx = lambda k: pl.BlockSpec((None, 1, d), lambda i, p0, p1: ((i * tm) // n, 0, k))
    mc = lambda k: pl.BlockSpec((None, 1, d), lambda i, p0, p1: (b, 0, k))
    in_specs = [pl.BlockSpec(memory_space=pl.ANY), row_spec,
                pl.BlockSpec((tm, TOP_K), lambda i, p0, p1: (i, 0)), mx(k_gate), mc(k_gate)]
    args = [pos0, pos1, ys, x1.reshape(t, d), gates, mod3, mod3]
    out_specs, out_shape = [row_spec], [jax.ShapeDtypeStruct((t, d), F32)]
    if next_norm is not None:
        g_next, mod_next = next_norm
        in_specs += [pl.BlockSpec((1, d), lambda i, p0, p1: (0, 0)), mx(0), mx(1), mc(0), mc(1)]
        args += [g_next.reshape(1, d), mod_next, mod_next, mod_next, mod_next]
        out_specs.append(row_spec)
        out_shape.append(jax.ShapeDtypeStruct((t, d), BF16))
    grid_spec = pltpu.PrefetchScalarGridSpec(
        num_scalar_prefetch=2,
        grid=(t // tm,),
        in_specs=in_specs,
        out_specs=out_specs,
        scratch_shapes=[pltpu.VMEM((tm, d), F32), pltpu.VMEM((tm, d), F32), pltpu.SemaphoreType.DMA],
    )
    outs = pl.pallas_call(
        functools.partial(_combine_kernel, s_len=s_len, n_len=n, tm=tm, with_next=next_norm is not None),
        grid_spec=grid_spec,
        out_shape=out_shape,
        compiler_params=_cparams(1, VMEM_LIMIT),
        name="combine",
    )(*args)
    return outs[0].reshape(b, n, d), (outs[1] if next_norm is not None else None)


def _moe(x1, h2, ri, rf, cnt, mod3, wg, wu, wd, layer, s_len, next_norm):
    b, n, d = x1.shape
    t = b * n
    n_exp = cnt.shape[0]
    counts = cnt[:, 0]
    n_blk = (counts + MOE_BLOCK - 1) // MOE_BLOCK
    blk_end = jnp.cumsum(n_blk)
    starts = (blk_end - n_blk) * MOE_BLOCK
    pos0 = (starts[ri[0]] + ri[2]).astype(jnp.int32)
    pos1 = (starts[ri[1]] + ri[3]).astype(jnp.int32)
    n_pad = t * TOP_K + n_exp * MOE_BLOCK
    n_used = blk_end[-1:].astype(jnp.int32)
    blk_ids = jnp.minimum(jnp.arange(n_pad // MOE_BLOCK, dtype=jnp.int32), n_used[0] - 1)
    blk_expert = jnp.sum(blk_ids[:, None] >= blk_end[None, :], axis=1).astype(jnp.int32)
    xs = _dispatch_call(h2.reshape(t, d), pos0, pos1, n_pad)
    ys = _ffn_call(xs, blk_expert, n_used, wg, wu, wd, layer)
    return _combine_call(ys, pos0, pos1, x1, rf[:TOP_K].T, mod3, s_len, 5, next_norm)


def _final_norm_kernel(x_ref, g_ref, o_ref):
    x = x_ref[...]
    o_ref[...] = x * lax.rsqrt(jnp.mean(x * x, axis=-1, keepdims=True) + EPS) * g_ref[...]


def _final_norm(xc, g, s_len):
    b, _, d = xc.shape
    tm = _pick(s_len, (ROW_TILE, 128))
    return pl.pallas_call(
        _final_norm_kernel,
        grid=(b, s_len // tm),
        in_specs=[pl.BlockSpec((None, tm, d), lambda bi, i: (bi, i, 0)),
                  pl.BlockSpec((1, d), lambda bi, i: (0, 0))],
        out_specs=pl.BlockSpec((None, tm, d), lambda bi, i: (bi, i, 0)),
        out_shape=jax.ShapeDtypeStruct((b, s_len, d), F32),
        compiler_params=_cparams(2, VMEM_LIMIT),
        name="final_norm",
    )(xc, g.reshape(1, d))


def kernel(x, c, ctx, c_ctx, w_ada, b_ada, g_norm1, g_norm2, w_in, w_out, na_rpb, da_lambda, da_subln_g,
           pool_w, pool_scale, sg_norm_g, sg_w, sg_b, router_w, router_b, exp_w_gate, exp_w_up, exp_w_down,
           g_final):
    b, s_len, d = x.shape
    c_len = ctx.shape[1]
    n = s_len + c_len
    depth = w_ada.shape[0]
    assert b + 1 <= 8 and w_in.shape[2] == ATT_COLS + REST_COLS and w_out.shape[1] == 4 * MIX_W

    cond = jnp.zeros((8, d), F32).at[:b].set(c).at[b].set(c_ctx)
    mod_all = _ada_all(cond, w_ada, b_ada)
    cos, sin = _rope_tables(s_len, c_len)
    xc = jnp.concatenate([x, ctx], axis=1)
    wg, wu, wd = exp_w_gate.astype(BF16), exp_w_up.astype(BF16), exp_w_down.astype(BF16)
    w_out_bf = w_out.astype(BF16)
    mods = [mod_all[l].reshape(8, 1, N_MOD * d) for l in range(depth)]

    h1 = _norm_mod_call(xc, g_norm1[0], mods[0], s_len, 0, 1).reshape(b * n, d)
    for l in range(depth):
        mod3 = mods[l]
        pxa = _matmul_cols(h1, w_in, l, 0, ATT_COLS, BF16, 1024).reshape(b, n, ATT_COLS)
        pxb = _matmul_cols(h1, w_in, l, ATT_COLS, REST_COLS, F32, 768).reshape(b, n, REST_COLS)
        qt, kr, vt = _rope_call(pxa, cos, sin)
        mix_a = _na_call(pxa, na_rpb[l], s_len, c_len)
        mix_b = _da_call(qt, kr, vt, da_lambda[l], da_subln_g[l], s_len, c_len, l)
        mix_c = _pool_call(pxb, pool_w[l], pool_scale[l], s_len, c_len)
        mix_d = _sg_call(pxb, sg_norm_g[l], sg_w[l], sg_b[l])
        x1, h2, ri, rf, cnt = _out_router_call((mix_a, mix_b, mix_c, mix_d), w_out_bf, l, xc, g_norm2[l], mod3,
                                               router_w, router_b, s_len)
        next_norm = (g_norm1[l + 1], mods[l + 1]) if l + 1 < depth else None
        xc, h1 = _moe(x1, h2, ri, rf, cnt, mod3, wg, wu, wd, l, s_len, next_norm)
    return _final_norm(xc, g_final, s_len)
```

```python
import functools
import math

import numpy as np
import jax
import jax.numpy as jnp
from jax import lax
from jax.experimental import pallas as pl
from jax.experimental.pallas import tpu as pltpu

GRID_W = 64
NA_HEADS = 4
NA_DIM = 128
NA_WIN_H = 8
NA_WIN_W = 16
DA_HEADS = 4
DA_DIM = 64
DA_VDIM = 2 * DA_DIM
ROPE_BASE = 10000.0
POOL_GROUPS = 4
POOL_WIDTH = 128
POOL_WINDOWS = (2, 4, 8, 16)
SG_GROUPS = 4
SG_WIDTH = 128
SG_CHUNK = 128
N_EXPERT_GROUPS = 4
EXPERTS_PER_GROUP = 4
TOP_K = 2
N_MOD = 6
EPS = 1e-6
NEG_INF = -1e30
LOG2_E = math.log2(math.e)

HEAD_W = 128
MIX_W = 512
ATT_COLS = 6 * MIX_W
REST_COLS = 3 * MIX_W
MOE_BLOCK = 256
NA_Q_ROWS = 4
NA_HEADS_PER_STEP = 2
ROW_TILE = 256
DA_KEY_CHUNKS = (1408, 768, 512, 256, 128)
DA_VT_ROWS = DA_VDIM + 16
VMEM_LIMIT = 56 * 1024 * 1024

F32 = jnp.float32
BF16 = jnp.bfloat16


def _cparams(n_axes, vmem=None):
    return pltpu.CompilerParams(dimension_semantics=("arbitrary",) * n_axes,
                                vmem_limit_bytes=vmem)


def _pick(n, candidates):
    for c in candidates:
        if n % c == 0:
            return c
    raise ValueError(f"no tile in {candidates} divides {n}")


def _ada_kernel(c_ref, w_ref, b_ref, o_ref):
    cs = jax.nn.silu(c_ref[...]).astype(BF16)
    o_ref[...] = jnp.dot(cs, w_ref[...].astype(BF16), preferred_element_type=F32) + b_ref[...]


def _ada_all(cond, w_ada, b_ada):
    depth, d, n6 = w_ada.shape
    tn = _pick(n6, (1024, 512, 256, 128))
    return pl.pallas_call(
        _ada_kernel,
        grid=(depth, n6 // tn),
        in_specs=[pl.BlockSpec((8, d), lambda l, j: (0, 0)),
                  pl.BlockSpec((None, d, tn), lambda l, j: (l, 0, j)),
                  pl.BlockSpec((None, 1, tn), lambda l, j: (l, 0, j))],
        out_specs=pl.BlockSpec((None, 8, tn), lambda l, j: (l, 0, j)),
        out_shape=jax.ShapeDtypeStruct((depth, 8, n6), F32),
        compiler_params=_cparams(2, VMEM_LIMIT),
        name="adaln",
    )(cond, w_ada, b_ada.reshape(depth, 1, n6))


def _row_is_ctx(i, tm, s_len):
    row = i * tm + lax.broadcasted_iota(jnp.int32, (tm, 1), 0)
    return row >= s_len


def _norm_mod(x, g, shx, scx, shc, scc, is_ctx):
    y = x * lax.rsqrt(jnp.mean(x * x, axis=-1, keepdims=True) + EPS) * g
    shift = jnp.where(is_ctx, shc, shx)
    scale = jnp.where(is_ctx, scc, scx)
    return y * (1.0 + scale) + shift


def _norm_mod_kernel(x_ref, g_ref, shx_ref, scx_ref, shc_ref, scc_ref, o_ref, *, s_len, tm):
    is_ctx = _row_is_ctx(pl.program_id(1), tm, s_len)
    h = _norm_mod(x_ref[...], g_ref[...], shx_ref[...], scx_ref[...], shc_ref[...], scc_ref[...], is_ctx)
    o_ref[...] = h.astype(o_ref.dtype)


def _norm_mod_call(xc, g, mod3, s_len, k_shift, k_scale):
    b, n, d = xc.shape
    tm = _pick(n, (ROW_TILE, 128))
    mx = lambda k: pl.BlockSpec((None, 1, d), lambda bi, i: (bi, 0, k))
    mc = lambda k: pl.BlockSpec((None, 1, d), lambda bi, i: (b, 0, k))
    return pl.pallas_call(
        functools.partial(_norm_mod_kernel, s_len=s_len, tm=tm),
        grid=(b, n // tm),
        in_specs=[pl.BlockSpec((None, tm, d), lambda bi, i: (bi, i, 0)),
                  pl.BlockSpec((1, d), lambda bi, i: (0, 0)),
                  mx(k_shift), mx(k_scale), mc(k_shift), mc(k_scale)],
        out_specs=pl.BlockSpec((None, tm, d), lambda bi, i: (bi, i, 0)),
        out_shape=jax.ShapeDtypeStruct((b, n, d), BF16),
        compiler_params=_cparams(2, VMEM_LIMIT),
        name="norm_mod",
    )(xc, g.reshape(1, d), mod3, mod3, mod3, mod3)


def _mm_kernel(a_ref, w_ref, o_ref, wbf_ref):
    @pl.when(pl.program_id(1) == 0)
    def _():
        wbf_ref[...] = w_ref[...].astype(BF16)

    o_ref[...] = jnp.dot(a_ref[...], wbf_ref[...], preferred_element_type=F32).astype(o_ref.dtype)


def _matmul_cols(a, w, layer, col0, ncols, out_dtype, tn):
    m, k = a.shape
    tm = _pick(m, (768, 512, 384, 256, 128))
    off = col0 // tn
    return pl.pallas_call(
        _mm_kernel,
        grid=(ncols // tn, m // tm),
        in_specs=[pl.BlockSpec((tm, k), lambda j, i: (i, 0)),
                  pl.BlockSpec((None, k, tn), lambda j, i: (layer, 0, j + off))],
        out_specs=pl.BlockSpec((tm, tn), lambda j, i: (i, j)),
        out_shape=jax.ShapeDtypeStruct((m, ncols), out_dtype),
        scratch_shapes=[pltpu.VMEM((k, tn), BF16)],
        compiler_params=_cparams(2, VMEM_LIMIT),
        name="in_proj",
    )(a, w)


def _rope_tables(s_len, c_len):
    t = jnp.arange(s_len)
    row = (t // GRID_W).astype(F32)
    col = (t % GRID_W).astype(F32)
    n_freq = DA_DIM // 4
    inv_freq = 1.0 / (ROPE_BASE ** (jnp.arange(n_freq, dtype=F32) / n_freq))
    ang = jnp.concatenate([row[:, None] * inv_freq, col[:, None] * inv_freq], axis=-1)
    cos = jnp.concatenate([jnp.cos(ang), jnp.ones((c_len, DA_DIM // 2), F32)], axis=0)
    sin = jnp.concatenate([jnp.sin(ang), jnp.zeros((c_len, DA_DIM // 2), F32)], axis=0)
    return jnp.tile(cos, (1, 4)), jnp.tile(jnp.concatenate([-sin, sin], axis=-1), (1, 2))


def _rope_kernel(q_ref, k_ref, v_ref, c_ref, s_ref, qt_ref, kr_ref, vt_ref, *, tm, q_scale):
    cos = c_ref[...]
    sin = s_ref[...]
    lane = lax.broadcasted_iota(jnp.int32, (tm, HEAD_W), 1)
    first_half = (lane % DA_DIM) < (DA_DIM // 2)

    def rotate(x):
        swapped = jnp.where(first_half, pltpu.roll(x, HEAD_W - DA_DIM // 2, 1), pltpu.roll(x, DA_DIM // 2, 1))
        return x * cos + swapped * sin

    for h in range(DA_HEADS):
        cols = slice(h * HEAD_W, (h + 1) * HEAD_W)
        q = rotate(q_ref[:, cols].astype(F32)) * q_scale
        qt_ref[cols, :] = q.T.astype(qt_ref.dtype)
        kr_ref[:, cols] = rotate(k_ref[:, cols].astype(F32)).astype(kr_ref.dtype)
        vt_ref[h * DA_VT_ROWS:h * DA_VT_ROWS + DA_VDIM, :] = v_ref[:, cols].astype(F32).T.astype(vt_ref.dtype)
        pad = lax.broadcasted_iota(jnp.int32, (DA_VT_ROWS - DA_VDIM, tm), 0)
        vt_ref[h * DA_VT_ROWS + DA_VDIM:(h + 1) * DA_VT_ROWS, :] = jnp.where(pad == 0, 1.0, 0.0).astype(vt_ref.dtype)


def _rope_call(pxa, cos, sin):
    b, n, _ = pxa.shape
    tm = _pick(n, (ROW_TILE, 128))
    col_spec = lambda k: pl.BlockSpec((None, tm, MIX_W), lambda bi, i: (bi, i, k))
    t_spec = lambda rows: pl.BlockSpec((None, rows, tm), lambda bi, i: (bi, 0, i))
    return pl.pallas_call(
        functools.partial(_rope_kernel, tm=tm, q_scale=DA_DIM ** -0.5 * LOG2_E),
        grid=(b, n // tm),
        in_specs=[col_spec(3), col_spec(4), col_spec(5),
                  pl.BlockSpec((tm, HEAD_W), lambda bi, i: (i, 0)),
                  pl.BlockSpec((tm, HEAD_W), lambda bi, i: (i, 0))],
        out_specs=[t_spec(MIX_W), col_spec(0), t_spec(DA_HEADS * DA_VT_ROWS)],
        out_shape=[jax.ShapeDtypeStruct((b, MIX_W, n), BF16),
                   jax.ShapeDtypeStruct((b, n, MIX_W), BF16),
                   jax.ShapeDtypeStruct((b, DA_HEADS * DA_VT_ROWS, n), BF16)],
        compiler_params=_cparams(2, VMEM_LIMIT),
        name="rope",
    )(pxa, pxa, pxa, cos, sin)


def _na_plan(rows, r_q):
    kh = min(NA_WIN_H, rows)
    key_rows = r_q + kh - 1
    assert rows % r_q == 0 and rows >= key_rows
    patterns, var, ustart = [], [], []
    for j in range(rows // r_q):
        r = r_q * j + np.arange(r_q)
        rs = np.clip(r - kh // 2, 0, rows - kh)
        u = int(np.clip(rs.min(), 0, rows - key_rows))
        kr = u + np.arange(key_rows)
        valid = (kr[None, :] >= rs[:, None]) & (kr[None, :] < rs[:, None] + kh)
        assert (valid.sum(1) == kh).all()
        dr0 = u - r + (NA_WIN_H - 1)
        key = (valid.tobytes(), dr0.tobytes())
        for v, (k2, _, _) in enumerate(patterns):
            if k2 == key:
                break
        else:
            v = len(patterns)
            patterns.append((key, valid, dr0))
        var.append(v)
        ustart.append(u * GRID_W)
    valid = np.stack([p[1] for p in patterns])
    dr0 = np.stack([p[2] for p in patterns])
    return np.asarray(var, np.int32), np.asarray(ustart, np.int32), valid, dr0


def _na_bias_tables(rpb, valid, dr0):
    col = np.arange(GRID_W)
    cs = np.clip(col - NA_WIN_W // 2, 0, GRID_W - NA_WIN_W)
    cmask = (col[None, :] >= cs[:, None]) & (col[None, :] < cs[:, None] + NA_WIN_W)
    dc = np.clip(col[None, :] - col[:, None], -(NA_WIN_W - 1), NA_WIN_W - 1) + (NA_WIN_W - 1)
    heads, n_dr, n_dc = rpb.shape
    v, r_q, key_rows = valid.shape
    pick = (dc.reshape(1, -1) == np.arange(n_dc)[:, None]).astype(np.float32)
    by_col = jnp.dot(rpb.astype(F32).reshape(heads * n_dr, n_dc), pick,
                     precision=lax.Precision.HIGHEST).reshape(heads, n_dr, GRID_W, GRID_W)
    padded = jnp.pad(by_col, ((0, 0), (key_rows, key_rows), (0, 0), (0, 0)))
    tab = jnp.stack([lax.slice_in_dim(padded, int(d) + key_rows, int(d) + 2 * key_rows, axis=1)
                     for d in dr0.reshape(-1)], axis=1)
    tab = tab.reshape(heads, v, r_q, key_rows, GRID_W, GRID_W).transpose(0, 1, 2, 4, 3, 5)
    mask = valid[:, :, None, :, None] & cmask[None, None, :, None, :]
    tab = jnp.where(mask[None], tab, NEG_INF)
    return tab.reshape(heads, v, r_q * GRID_W, key_rows * GRID_W)


def _na_kernel(var_ref, u_ref, q_ref, k_ref, v_ref, kc_ref, vc_ref, bias_ref, o_ref, *, n_lat, kw, scale):
    j = pl.program_id(2)
    nt = (((1,), (1,)), ((), ()))

    def one_head(hh, local):
        cols = slice(hh * HEAD_W, (hh + 1) * HEAD_W)
        q = q_ref[:, cols]
        vc = vc_ref[:, cols]
        s_ctx = lax.dot_general(q, kc_ref[:, cols], nt, preferred_element_type=F32) * scale
        m = jnp.max(s_ctx, axis=-1, keepdims=True)
        if local:
            u = pl.multiple_of(u_ref[j], GRID_W)
            s = lax.dot_general(q, k_ref[pl.ds(u, kw), cols], nt, preferred_element_type=F32) * scale + bias_ref[hh]
            m = jnp.maximum(m, jnp.max(s, axis=-1, keepdims=True))
            p = jnp.exp(s - m)
        pc = jnp.exp(s_ctx - m)
        l = jnp.sum(pc, axis=-1, keepdims=True)
        o = jnp.dot(pc.astype(BF16), vc, preferred_element_type=F32)
        if local:
            l += jnp.sum(p, axis=-1, keepdims=True)
            o += jnp.dot(p.astype(BF16), v_ref[pl.ds(u, kw), cols], preferred_element_type=F32)
        o_ref[:, cols] = (o / l).astype(o_ref.dtype)

    @pl.when(j < n_lat)
    def _():
        for hh in range(NA_HEADS_PER_STEP):
            one_head(hh, True)

    @pl.when(j >= n_lat)
    def _():
        for hh in range(NA_HEADS_PER_STEP):
            one_head(hh, False)


def _na_call(pxa, rpb, s_len, c_len):
    b, n, _ = pxa.shape
    rows = s_len // GRID_W
    tq = NA_Q_ROWS * GRID_W
    hps = NA_HEADS_PER_STEP
    assert s_len % tq == 0 and c_len % tq == 0 and s_len % c_len == 0 and NA_HEADS % hps == 0
    var, ustart, valid, dr0 = _na_plan(rows, NA_Q_ROWS)
    n_lat, n_q = s_len // tq, n // tq
    kw = valid.shape[2] * GRID_W
    tabs = _na_bias_tables(rpb, valid, dr0)
    pad = np.zeros(n_q - n_lat, np.int32)
    var = jnp.asarray(np.concatenate([var, pad]))
    ustart = jnp.asarray(np.concatenate([ustart, pad]))
    ctx_blk = s_len // c_len
    wide = hps * HEAD_W
    k_blk, v_blk = NA_HEADS // hps, 2 * NA_HEADS // hps
    grid_spec = pltpu.PrefetchScalarGridSpec(
        num_scalar_prefetch=2,
        grid=(b, NA_HEADS // hps, n_q),
        in_specs=[pl.BlockSpec((None, tq, wide), lambda bi, h, j, vr, ur: (bi, j, h)),
                  pl.BlockSpec((None, s_len, wide), lambda bi, h, j, vr, ur: (bi, 0, k_blk + h)),
                  pl.BlockSpec((None, s_len, wide), lambda bi, h, j, vr, ur: (bi, 0, v_blk + h)),
                  pl.BlockSpec((None, c_len, wide), lambda bi, h, j, vr, ur: (bi, ctx_blk, k_blk + h)),
                  pl.BlockSpec((None, c_len, wide), lambda bi, h, j, vr, ur: (bi, ctx_blk, v_blk + h)),
                  pl.BlockSpec((hps, None, tq, kw), lambda bi, h, j, vr, ur: (h, vr[j], 0, 0))],
        out_specs=pl.BlockSpec((None, tq, wide), lambda bi, h, j, vr, ur: (bi, j, h)),
    )
    return pl.pallas_call(
        functools.partial(_na_kernel, n_lat=n_lat, kw=kw, scale=NA_DIM ** -0.5),
        grid_spec=grid_spec,
        out_shape=jax.ShapeDtypeStruct((b, n, MIX_W), BF16),
        compiler_params=_cparams(3, VMEM_LIMIT),
        name="nbr_attn",
    )(var, ustart, pxa, pxa, pxa, pxa, pxa, tabs)


def _da_kernel(qt_ref, k_ref, vt_ref, lam_ref, g_ref, o_ref, sa_ref, sb_ref, *, s_len, c_len, tq, tk, lam_init):
    qi = pl.program_id(2)
    n = s_len + c_len
    feat = lax.broadcasted_iota(jnp.int32, (2 * DA_DIM, tq), 0)

    def q_tile(idx):
        start = pl.multiple_of(idx * tq, tq)
        qt = qt_ref[:, pl.ds(start, tq)]
        zero = jnp.zeros_like(qt)
        return jnp.concatenate([jnp.where(feat < DA_DIM, qt, zero), jnp.where(feat >= DA_DIM, qt, zero)], axis=1)

    q2 = q_tile(qi)

    def scores(k_blk, q):
        return jnp.dot(k_blk, q, preferred_element_type=F32)

    def update(s, vt_blk, carry):
        m, acc = carry
        m_new = jnp.maximum(m, jnp.max(s, axis=0, keepdims=True))
        alpha = jnp.exp2(m - m_new)
        p = jnp.exp2(s - m_new)
        acc_new = alpha * acc + jnp.dot(vt_blk, p.astype(BF16), preferred_element_type=F32)
        return m_new, acc_new

    def init():
        return jnp.full((1, 2 * tq), NEG_INF, F32), jnp.zeros((DA_VT_ROWS, 2 * tq), F32)

    def finalize(carry):
        lv = lam_ref[...]
        t1 = jnp.sum(lv[0:1, :] * lv[1:2, :], axis=-1, keepdims=True)
        t2 = jnp.sum(lv[2:3, :] * lv[3:4, :], axis=-1, keepdims=True)
        lam = jnp.exp(t1) - jnp.exp(t2) + lam_init
        _, acc = carry
        o = acc[:DA_VDIM, :] / acc[DA_VDIM:DA_VDIM + 1, :]
        o = o[:, :tq] - lam * o[:, tq:]
        y = o * lax.rsqrt(jnp.mean(o * o, axis=0, keepdims=True) + EPS) * g_ref[...]
        o_ref[...] = (y * (1.0 - lam_init)).T.astype(o_ref.dtype)

    n_steps = n // tk
    full_pairs = (n_steps - 1) // 2
    carry_first = n_steps % 2 == 0

    @pl.when(qi < s_len // tq)
    def _():
        def scores_into(dst_ref, step, q):
            start = pl.multiple_of(step * tk, tk)
            dst_ref[...] = scores(k_ref[pl.ds(start, tk), :], q)

        def update_from(src_ref, step, stats):
            start = pl.multiple_of(step * tk, tk)
            return update(src_ref[...], vt_ref[:, pl.ds(start, tk)], stats)

        def pair(j, stats):
            c0 = 2 * j
            scores_into(sb_ref, c0 + 1, q2)
            stats = update_from(sa_ref, c0, stats)
            scores_into(sa_ref, c0 + 2, q2)
            return update_from(sb_ref, c0 + 1, stats)

        if carry_first:
            @pl.when(qi == 0)
            def _():
                scores_into(sa_ref, 0, q2)
        else:
            scores_into(sa_ref, 0, q2)
        stats = lax.fori_loop(0, full_pairs, pair, init())
        c0 = 2 * full_pairs
        if carry_first:
            scores_into(sb_ref, c0 + 1, q2)
            stats = update_from(sa_ref, c0, stats)
            scores_into(sa_ref, 0, q_tile(qi + 1))
            stats = update_from(sb_ref, c0 + 1, stats)
        else:
            stats = update_from(sa_ref, c0, stats)
        finalize(stats)

    @pl.when(qi >= s_len // tq)
    def _():
        finalize(update(scores(k_ref[s_len:n, :], q2), vt_ref[:, s_len:n], init()))


def _da_call(qt, kr, vt, lam_vecs, subln_g, s_len, c_len, layer):
    b, n, _ = kr.shape
    tq = _pick(math.gcd(s_len, c_len), (256, 128))
    tk = _pick(n, DA_KEY_CHUNKS)
    lam_init = 0.8 - 0.6 * math.exp(-0.3 * layer)
    return pl.pallas_call(
        functools.partial(_da_kernel, s_len=s_len, c_len=c_len, tq=tq, tk=tk, lam_init=lam_init),
        grid=(b, DA_HEADS, n // tq),
        in_specs=[pl.BlockSpec((None, HEAD_W, n), lambda bi, h, i: (bi, h, 0)),
                  pl.BlockSpec((None, n, HEAD_W), lambda bi, h, i: (bi, 0, h)),
                  pl.BlockSpec((None, DA_VT_ROWS, n), lambda bi, h, i: (bi, h, 0)),
                  pl.BlockSpec((4, DA_DIM), lambda bi, h, i: (0, 0)),
                  pl.BlockSpec((DA_VDIM, 1), lambda bi, h, i: (0, 0))],
        out_specs=pl.BlockSpec((None, tq, HEAD_W), lambda bi, h, i: (bi, i, h)),
        out_shape=jax.ShapeDtypeStruct((b, n, MIX_W), BF16),
        scratch_shapes=[pltpu.VMEM((tk, 2 * tq), F32), pltpu.VMEM((tk, 2 * tq), F32)],
        compiler_params=_cparams(3, VMEM_LIMIT),
        name="diff_attn",
    )(qt, kr, vt, lam_vecs, subln_g.reshape(DA_VDIM, 1))


POOL_HALO = 8


def _pool_kernel(prev_ref, cur_ref, next_ref, w_ref, sc_ref, o_ref, buf_ref, *, s_len, c_len, tm):
    i = pl.program_id(1)
    n_lat = s_len // tm
    n_all = (s_len + c_len) // tm
    seq_start = (i == 0) | (i == n_lat)
    seq_end = (i == n_lat - 1) | (i == n_all - 1)
    buf_ref[0:POOL_HALO, :] = jnp.where(seq_start, 0.0, prev_ref[...])
    buf_ref[POOL_HALO:POOL_HALO + tm, :] = cur_ref[...]
    buf_ref[POOL_HALO + tm:, :] = jnp.where(seq_end, 0.0, next_ref[...])
    in_ctx = i >= n_lat
    seq_len = jnp.where(in_ctx, c_len, s_len)
    pos = i * tm - jnp.where(in_ctx, s_len, 0) + lax.broadcasted_iota(jnp.int32, (tm, 1), 0)
    for g, win in enumerate(POOL_WINDOWS):
        half = win // 2
        cols = slice(g * POOL_WIDTH, (g + 1) * POOL_WIDTH)
        tot = buf_ref[POOL_HALO - half:POOL_HALO - half + tm, cols]
        for d in range(-half + 1, half):
            tot = tot + buf_ref[POOL_HALO + d:POOL_HALO + d + tm, cols]
        cnt = jnp.clip(pos + half, 0, seq_len) - jnp.clip(pos - half, 0, seq_len)
        resid = tot / cnt.astype(F32) - cur_ref[:, cols]
        y = jnp.dot(resid.astype(BF16), w_ref[g].astype(BF16), preferred_element_type=F32)
        o_ref[:, cols] = (y * sc_ref[:, cols]).astype(o_ref.dtype)


def _pool_call(pxb, pool_w, pool_scale, s_len, c_len):
    b, n, _ = pxb.shape
    tm = _pick(math.gcd(s_len, c_len), (ROW_TILE, 128))
    hb = tm // POOL_HALO
    last = n // POOL_HALO - 1
    return pl.pallas_call(
        functools.partial(_pool_kernel, s_len=s_len, c_len=c_len, tm=tm),
        grid=(b, n // tm),
        in_specs=[pl.BlockSpec((None, POOL_HALO, MIX_W), lambda bi, i: (bi, jnp.maximum(i * hb - 1, 0), 0)),
                  pl.BlockSpec((None, tm, MIX_W), lambda bi, i: (bi, i, 0)),
                  pl.BlockSpec((None, POOL_HALO, MIX_W), lambda bi, i: (bi, jnp.minimum((i + 1) * hb, last), 0)),
                  pl.BlockSpec((POOL_GROUPS, POOL_WIDTH, POOL_WIDTH), lambda bi, i: (0, 0, 0)),
                  pl.BlockSpec((1, MIX_W), lambda bi, i: (0, 0))],
        out_specs=pl.BlockSpec((None, tm, MIX_W), lambda bi, i: (bi, i, 0)),
        out_shape=jax.ShapeDtypeStruct((b, n, MIX_W), BF16),
        scratch_shapes=[pltpu.VMEM((tm + 2 * POOL_HALO, MIX_W), F32)],
        compiler_params=_cparams(2, VMEM_LIMIT),
        name="pool",
    )(pxb, pxb, pxb, pool_w, pool_scale.reshape(1, MIX_W))


def _sg_kernel(u_ref, v_ref, g_ref, w_ref, b_ref, o_ref, *, tm):
    v = jax.nn.gelu(v_ref[...])
    vn = (v * lax.rsqrt(jnp.mean(v * v, axis=-1, keepdims=True) + EPS) * g_ref[...]).astype(BF16)
    bias = b_ref[...]
    for c in range(tm // SG_CHUNK):
        rows = slice(c * SG_CHUNK, (c + 1) * SG_CHUNK)
        for g in range(SG_GROUPS):
            cols = slice(g * SG_WIDTH, (g + 1) * SG_WIDTH)
            mixed = jnp.dot(w_ref[g].astype(BF16), vn[rows, cols], preferred_element_type=F32)
            mixed = mixed + bias[:, g:g + 1]
            o_ref[rows, cols] = (jax.nn.gelu(u_ref[rows, cols]) * mixed).astype(o_ref.dtype)


def _sg_call(pxb, sg_norm_g, sg_w, sg_b):
    b, n, _ = pxb.shape
    tm = _pick(n, (ROW_TILE, 128))
    return pl.pallas_call(
        functools.partial(_sg_kernel, tm=tm),
        grid=(b, n // tm),
        in_specs=[pl.BlockSpec((None, tm, MIX_W), lambda bi, i: (bi, i, 1)),
                  pl.BlockSpec((None, tm, MIX_W), lambda bi, i: (bi, i, 2)),
                  pl.BlockSpec((1, MIX_W), lambda bi, i: (0, 0)),
                  pl.BlockSpec((SG_GROUPS, SG_CHUNK, SG_CHUNK), lambda bi, i: (0, 0, 0)),
                  pl.BlockSpec((SG_CHUNK, SG_GROUPS), lambda bi, i: (0, 0))],
        out_specs=pl.BlockSpec((None, tm, MIX_W), lambda bi, i: (bi, i, 0)),
        out_shape=jax.ShapeDtypeStruct((b, n, MIX_W), BF16),
        compiler_params=_cparams(2, VMEM_LIMIT),
        name="spatial_gate",
    )(pxb, pxb, sg_norm_g.reshape(1, MIX_W), sg_w, sg_b.T)


def _first_argmax(vals):
    best, idx = vals[0], jnp.zeros(vals[0].shape, jnp.int32)
    for k in range(1, len(vals)):
        take = vals[k] > best
        best = jnp.where(take, vals[k], best)
        idx = jnp.where(take, k, idx)
    return idx, best


def _out_router_kernel(ma_ref, mb_ref, mc_ref, md_ref, w_ref, x_ref, gx_ref, gc_ref,
                       g_ref, shx_ref, scx_ref, shc_ref, scc_ref, rw_ref, rb_ref,
                       x1_ref, h_ref, ri_ref, rf_ref, cnt_ref, run_ref, *, s_len, tm, n_exp):
    bi, i = pl.program_id(0), pl.program_id(1)

    @pl.when((bi == 0) & (i == 0))
    def _():
        run_ref[...] = jnp.zeros_like(run_ref)

    mix = jnp.concatenate([ma_ref[...], mb_ref[...], mc_ref[...], md_ref[...]], axis=1)
    acc = jnp.dot(mix, w_ref[...], preferred_element_type=F32)
    is_ctx = _row_is_ctx(i, tm, s_len)
    x1 = x_ref[...] + jnp.where(is_ctx, gc_ref[...], gx_ref[...]) * acc
    x1_ref[...] = x1
    h = _norm_mod(x1, g_ref[...], shx_ref[...], scx_ref[...], shc_ref[...], scc_ref[...], is_ctx)
    h_ref[...] = h.astype(h_ref.dtype)

    h_hi = h.astype(BF16)
    h_lo = (h - h_hi.astype(F32)).astype(BF16)
    part = jnp.dot(h_hi, rw_ref[0], preferred_element_type=F32)
    part += jnp.dot(h_lo, rw_ref[1], preferred_element_type=F32)
    part_t = part.T
    logits = part_t[0:n_exp, :] + part_t[n_exp:2 * n_exp, :]
    scores = jax.nn.sigmoid(logits)
    biased = scores + rb_ref[...]
    b_rows = [biased[e:e + 1, :] for e in range(n_exp)]
    s_rows = [scores[e:e + 1, :] for e in range(n_exp)]
    epg = EXPERTS_PER_GROUP
    group_scores = []
    for g in range(N_EXPERT_GROUPS):
        r = b_rows[g * epg:(g + 1) * epg]
        pair = None
        for a in range(epg):
            for c in range(a + 1, epg):
                t = r[a] + r[c]
                pair = t if pair is None else jnp.maximum(pair, t)
        group_scores.append(pair)
    group, _ = _first_argmax(group_scores)
    in_group = []
    for k in range(epg):
        sel = b_rows[k]
        for g in range(1, N_EXPERT_GROUPS):
            sel = jnp.where(group == g, b_rows[g * epg + k], sel)
        in_group.append(sel)
    loc0, _ = _first_argmax(in_group)
    rest = [jnp.where(loc0 == k, -jnp.inf, in_group[k]) for k in range(epg)]
    loc1, _ = _first_argmax(rest)
    e0 = group * epg + loc0
    e1 = group * epg + loc1
    g0 = jnp.zeros_like(s_rows[0])
    g1 = jnp.zeros_like(s_rows[0])
    for e in range(n_exp):
        g0 = jnp.where(e0 == e, s_rows[e], g0)
        g1 = jnp.where(e1 == e, s_rows[e], g1)
    tot = g0 + g1

    eidx = lax.broadcasted_iota(jnp.int32, (n_exp, tm), 0)
    hit0 = eidx == e0
    hit1 = eidx == e1
    onehot = jnp.where(hit0 | hit1, 1.0, 0.0)
    before = lax.broadcasted_iota(jnp.int32, (tm, tm), 0) < lax.broadcasted_iota(jnp.int32, (tm, tm), 1)
    tri = jnp.where(before, 1.0, 0.0).astype(BF16)
    run = run_ref[:, 0:1]
    rank = jnp.dot(onehot.astype(BF16), tri, preferred_element_type=F32) + run
    rank0 = jnp.sum(jnp.where(hit0, rank, 0.0), axis=0, keepdims=True)
    rank1 = jnp.sum(jnp.where(hit1, rank, 0.0), axis=0, keepdims=True)
    new_run = run + jnp.sum(onehot, axis=1, keepdims=True)
    run_ref[...] = jnp.broadcast_to(new_run, run_ref.shape)
    cnt_ref[...] = jnp.broadcast_to(new_run, cnt_ref.shape).astype(jnp.int32)

    zi = jnp.zeros((4, tm), jnp.int32)
    ri_ref[...] = jnp.concatenate([e0, e1, rank0.astype(jnp.int32), rank1.astype(jnp.int32), zi], axis=0)
    zf = jnp.zeros((6, tm), F32)
    rf_ref[...] = jnp.concatenate([g0 / tot, g1 / tot, zf], axis=0)


def _router_slabs(router_w):
    d, n_exp = router_w.shape
    w_hi = router_w.astype(BF16)
    w_lo = (router_w - w_hi.astype(F32)).astype(BF16)
    zeros = lambda cols: jnp.zeros((d, cols), BF16)
    return jnp.stack([jnp.concatenate([w_hi, w_lo, zeros(HEAD_W - 2 * n_exp)], axis=1),
                      jnp.concatenate([w_hi, zeros(HEAD_W - n_exp)], axis=1)])


def _out_router_call(mixes, w_out_bf, layer, xc, g2, mod3, router_w, router_b, s_len):
    b, n, d = xc.shape
    n_exp = router_w.shape[1]
    tm = _pick(n, (ROW_TILE, 128))
    per_b = n // tm
    mx = lambda k: pl.BlockSpec((None, 1, d), lambda bi, i: (bi, 0, k))
    mc = lambda k: pl.BlockSpec((None, 1, d), lambda bi, i: (b, 0, k))
    mix_spec = pl.BlockSpec((None, tm, MIX_W), lambda bi, i: (bi, i, 0))
    row_spec = pl.BlockSpec((None, tm, d), lambda bi, i: (bi, i, 0))
    tok_spec = pl.BlockSpec((8, tm), lambda bi, i: (0, bi * per_b + i))
    return pl.pallas_call(
        functools.partial(_out_router_kernel, s_len=s_len, tm=tm, n_exp=n_exp),
        grid=(b, per_b),
        in_specs=[mix_spec, mix_spec, mix_spec, mix_spec,
                  pl.BlockSpec((None, 4 * MIX_W, d), lambda bi, i: (layer, 0, 0)),
                  row_spec, mx(2), mc(2),
                  pl.BlockSpec((1, d), lambda bi, i: (0, 0)),
                  mx(3), mx(4), mc(3), mc(4),
                  pl.BlockSpec((2, d, HEAD_W), lambda bi, i: (0, 0, 0)),
                  pl.BlockSpec((n_exp, 1), lambda bi, i: (0, 0))],
        out_specs=[row_spec, row_spec, tok_spec, tok_spec,
                   pl.BlockSpec((n_exp, HEAD_W), lambda bi, i: (0, 0))],
        out_shape=[jax.ShapeDtypeStruct((b, n, d), F32),
                   jax.ShapeDtypeStruct((b, n, d), F32),
                   jax.ShapeDtypeStruct((8, b * n), jnp.int32),
                   jax.ShapeDtypeStruct((8, b * n), F32),
                   jax.ShapeDtypeStruct((n_exp, HEAD_W), jnp.int32)],
        scratch_shapes=[pltpu.VMEM((n_exp, HEAD_W), F32)],
        compiler_params=_cparams(2, VMEM_LIMIT),
        name="out_router",
    )(*mixes, w_out_bf, xc, mod3, mod3, g2.reshape(1, d), mod3, mod3, mod3, mod3,
      _router_slabs(router_w), router_b.reshape(n_exp, 1))


def _row_copy(src_ref, src_row, dst_ref, dst_row, sem):
    return pltpu.make_async_copy(src_ref.at[pl.ds(src_row, 1)], dst_ref.at[pl.ds(dst_row, 1)], sem)


def _dispatch_kernel(p0_ref, p1_ref, h_ref, xs_in_ref, xs_ref, sem, *, tm):
    del xs_in_ref
    base = pl.program_id(0) * tm

    for r in range(tm):
        _row_copy(h_ref, r, xs_ref, p0_ref[base + r], sem).start()
        _row_copy(h_ref, r, xs_ref, p1_ref[base + r], sem).start()

    tile_copy = pltpu.make_async_copy(h_ref, xs_ref.at[pl.ds(0, tm)], sem)
    tile_copy.wait()
    tile_copy.wait()


def _dispatch_call(h2, pos0, pos1, n_pad):
    t, d = h2.shape
    tm = _pick(t, (ROW_TILE, 128))
    grid_spec = pltpu.PrefetchScalarGridSpec(
        num_scalar_prefetch=2,
        grid=(t // tm,),
        in_specs=[pl.BlockSpec((tm, d), lambda i, p0, p1: (i, 0)),
                  pl.BlockSpec(memory_space=pl.ANY)],
        out_specs=pl.BlockSpec(memory_space=pl.ANY),
        scratch_shapes=[pltpu.SemaphoreType.DMA],
    )
    return pl.pallas_call(
        functools.partial(_dispatch_kernel, tm=tm),
        grid_spec=grid_spec,
        out_shape=jax.ShapeDtypeStruct((n_pad, d), F32),
        input_output_aliases={3: 0},
        compiler_params=_cparams(1, VMEM_LIMIT),
        name="dispatch",
    )(pos0, pos1, h2, jnp.zeros((n_pad, d), F32))


def _ffn_kernel(be_ref, nb_ref, x_ref, wg_ref, wu_ref, wd_ref, y_ref):
    blk = pl.program_id(0)

    @pl.when(blk < nb_ref[0])
    def _():
        x = x_ref[...].astype(BF16)
        gate = jnp.dot(x, wg_ref[...], preferred_element_type=F32)
        up = jnp.dot(x, wu_ref[...], preferred_element_type=F32)
        hid = (jax.nn.silu(gate) * up).astype(BF16)
        y_ref[...] = jnp.dot(hid, wd_ref[...], preferred_element_type=F32)

    @pl.when(blk >= nb_ref[0])
    def _():
        y_ref[...] = jnp.zeros_like(y_ref)


def _ffn_call(xs, blk_expert, n_used, wg, wu, wd, layer):
    n_pad, d = xs.shape
    d_exp = wg.shape[3]
    grid_spec = pltpu.PrefetchScalarGridSpec(
        num_scalar_prefetch=2,
        grid=(n_pad // MOE_BLOCK,),
        in_specs=[pl.BlockSpec((MOE_BLOCK, d), lambda i, be, nb: (i, 0)),
                  pl.BlockSpec((None, None, d, d_exp), lambda i, be, nb: (layer, be[i], 0, 0)),
                  pl.BlockSpec((None, None, d, d_exp), lambda i, be, nb: (layer, be[i], 0, 0)),
                  pl.BlockSpec((None, None, d_exp, d), lambda i, be, nb: (layer, be[i], 0, 0))],
        out_specs=pl.BlockSpec((MOE_BLOCK, d), lambda i, be, nb: (i, 0)),
    )
    return pl.pallas_call(
        _ffn_kernel,
        grid_spec=grid_spec,
        out_shape=jax.ShapeDtypeStruct((n_pad, d), F32),
        compiler_params=_cparams(1, VMEM_LIMIT),
        name="expert_ffn",
    )(blk_expert, n_used, xs, wg, wu, wd)


def _combine_kernel(p0_ref, p1_ref, ys_ref, x_ref, gt_ref, gx_ref, gc_ref, *rest, s_len, n_len, tm, with_next):
    if with_next:
        g_ref, shx_ref, scx_ref, shc_ref, scc_ref, o_ref, h_ref, y0_ref, y1_ref, sem = rest
    else:
        o_ref, y0_ref, y1_ref, sem = rest
    i = pl.program_id(0)
    last = pl.num_programs(0) - 1

    def start_tile(tile, slot):
        base = tile * tm
        for r in range(tm):
            _row_copy(ys_ref, p0_ref[base + r], y0_ref.at[slot], r, sem.at[slot]).start()
            _row_copy(ys_ref, p1_ref[base + r], y1_ref.at[slot], r, sem.at[slot]).start()

    def wait_tile(slot):
        pltpu.make_async_copy(ys_ref.at[pl.ds(0, tm)], y0_ref.at[slot], sem.at[slot]).wait()
        pltpu.make_async_copy(ys_ref.at[pl.ds(0, tm)], y1_ref.at[slot], sem.at[slot]).wait()

    def combine(slot):
        row = ((i * tm) % n_len) + lax.broadcasted_iota(jnp.int32, (tm, 1), 0)
        is_ctx = row >= s_len
        gate = jnp.where(is_ctx, gc_ref[...], gx_ref[...])
        gt = gt_ref[...]
        y = gt[:, 0:1] * y0_ref[slot] + gt[:, 1:2] * y1_ref[slot]
        x_new = x_ref[...] + gate * y
        o_ref[...] = x_new
        if with_next:
            h = _norm_mod(x_new, g_ref[...], shx_ref[...], scx_ref[...], shc_ref[...], scc_ref[...], is_ctx)
            h_ref[...] = h.astype(h_ref.dtype)

    @pl.when(i == 0)
    def _():
        start_tile(0, 0)

    for slot in (0, 1):
        @pl.when(i % 2 == slot)
        def _(slot=slot):
            wait_tile(slot)
            start_tile(jnp.minimum(i + 1, last), 1 - slot)
            combine(slot)

            @pl.when(i == last)
            def _():
                wait_tile(1 - slot)


def _combine_call(ys, pos0, pos1, x1, gates, mod3, s_len, k_gate, next_norm=None):
    b, n, d = x1.shape
    t = b * n
    tm = _pick(n, (ROW_TILE, 128))
    row_spec = pl.BlockSpec((tm, d), lambda i, p0, p1: (i, 0))
    mx = lambda k: pl.BlockSpec((None, 1, d), lambda i, p0, p1: ((i * tm) // n, 0, k))
    mc = lambda k: pl.BlockSpec((None, 1, d), lambda i, p0, p1: (b, 0, k))
    in_specs = [pl.BlockSpec(memory_space=pl.ANY), row_spec,
                pl.BlockSpec((tm, TOP_K), lambda i, p0, p1: (i, 0)), mx(k_gate), mc(k_gate)]
    args = [pos0, pos1, ys, x1.reshape(t, d), gates, mod3, mod3]
    out_specs, out_shape = [row_spec], [jax.ShapeDtypeStruct((t, d), F32)]
    if next_norm is not None:
        g_next, mod_next = next_norm
        in_specs += [pl.BlockSpec((1, d), lambda i, p0, p1: (0, 0)), mx(0), mx(1), mc(0), mc(1)]
        args += [g_next.reshape(1, d), mod_next, mod_next, mod_next, mod_next]
        out_specs.append(row_spec)
        out_shape.append(jax.ShapeDtypeStruct((t, d), BF16))
    grid_spec = pltpu.PrefetchScalarGridSpec(
        num_scalar_prefetch=2,
        grid=(t // tm,),
        in_specs=in_specs,
        out_specs=out_specs,
        scratch_shapes=[pltpu.VMEM((2, tm, d), F32), pltpu.VMEM((2, tm, d), F32), pltpu.SemaphoreType.DMA((2,))],
    )
    outs = pl.pallas_call(
        functools.partial(_combine_kernel, s_len=s_len, n_len=n, tm=tm, with_next=next_norm is not None),
        grid_spec=grid_spec,
        out_shape=out_shape,
        compiler_params=_cparams(1, VMEM_LIMIT),
        name="combine",
    )(*args)
    return outs[0].reshape(b, n, d), (outs[1] if next_norm is not None else None)


def _moe(x1, h2, ri, rf, cnt, mod3, wg, wu, wd, layer, s_len, next_norm):
    b, n, d = x1.shape
    t = b * n
    n_exp = cnt.shape[0]
    counts = cnt[:, 0]
    n_blk = (counts + MOE_BLOCK - 1) // MOE_BLOCK
    blk_end = jnp.cumsum(n_blk)
    starts = (blk_end - n_blk) * MOE_BLOCK
    pos0 = (starts[ri[0]] + ri[2]).astype(jnp.int32)
    pos1 = (starts[ri[1]] + ri[3]).astype(jnp.int32)
    n_pad = t * TOP_K + n_exp * MOE_BLOCK
    n_used = blk_end[-1:].astype(jnp.int32)
    blk_ids = jnp.minimum(jnp.arange(n_pad // MOE_BLOCK, dtype=jnp.int32), n_used[0] - 1)
    blk_expert = jnp.sum(blk_ids[:, None] >= blk_end[None, :], axis=1).astype(jnp.int32)
    xs = _dispatch_call(h2.reshape(t, d), pos0, pos1, n_pad)
    ys = _ffn_call(xs, blk_expert, n_used, wg, wu, wd, layer)
    return _combine_call(ys, pos0, pos1, x1, rf[:TOP_K].T, mod3, s_len, 5, next_norm)


def _final_norm_kernel(x_ref, g_ref, o_ref):
    x = x_ref[...]
    o_ref[...] = x * lax.rsqrt(jnp.mean(x * x, axis=-1, keepdims=True) + EPS) * g_ref[...]


def _final_norm(xc, g, s_len):
    b, _, d = xc.shape
    tm = _pick(s_len, (ROW_TILE, 128))
    return pl.pallas_call(
        _final_norm_kernel,
        grid=(b, s_len // tm),
        in_specs=[pl.BlockSpec((None, tm, d), lambda bi, i: (bi, i, 0)),
                  pl.BlockSpec((1, d), lambda bi, i: (0, 0))],
        out_specs=pl.BlockSpec((None, tm, d), lambda bi, i: (bi, i, 0)),
        out_shape=jax.ShapeDtypeStruct((b, s_len, d), F32),
        compiler_params=_cparams(2, VMEM_LIMIT),
        name="final_norm",
    )(xc, g.reshape(1, d))


def kernel(x, c, ctx, c_ctx, w_ada, b_ada, g_norm1, g_norm2, w_in, w_out, na_rpb, da_lambda, da_subln_g,
           pool_w, pool_scale, sg_norm_g, sg_w, sg_b, router_w, router_b, exp_w_gate, exp_w_up, exp_w_down,
           g_final):
    b, s_len, d = x.shape
    c_len = ctx.shape[1]
    n = s_len + c_len
    depth = w_ada.shape[0]
    assert b + 1 <= 8 and w_in.shape[2] == ATT_COLS + REST_COLS and w_out.shape[1] == 4 * MIX_W

    cond = jnp.zeros((8, d), F32).at[:b].set(c).at[b].set(c_ctx)
    mod_all = _ada_all(cond, w_ada, b_ada)
    cos, sin = _rope_tables(s_len, c_len)
    xc = jnp.concatenate([x, ctx], axis=1)
    wg, wu, wd = exp_w_gate.astype(BF16), exp_w_up.astype(BF16), exp_w_down.astype(BF16)
    w_out_bf = w_out.astype(BF16)
    mods = [mod_all[l].reshape(8, 1, N_MOD * d) for l in range(depth)]

    h1 = _norm_mod_call(xc, g_norm1[0], mods[0], s_len, 0, 1).reshape(b * n, d)
    for l in range(depth):
        mod3 = mods[l]
        pxa = _matmul_cols(h1, w_in, l, 0, ATT_COLS, BF16, 1024).reshape(b, n, ATT_COLS)
        pxb = _matmul_cols(h1, w_in, l, ATT_COLS, REST_COLS, F32, 768).reshape(b, n, REST_COLS)
        qt, kr, vt = _rope_call(pxa, cos, sin)
        mix_a = _na_call(pxa, na_rpb[l], s_len, c_len)
        mix_b = _da_call(qt, kr, vt, da_lambda[l], da_subln_g[l], s_len, c_len, l)
        mix_c = _pool_call(pxb, pool_w[l], pool_scale[l], s_len, c_len)
        mix_d = _sg_call(pxb, sg_norm_g[l], sg_w[l], sg_b[l])
        x1, h2, ri, rf, cnt = _out_router_call((mix_a, mix_b, mix_c, mix_d), w_out_bf, l, xc, g_norm2[l], mod3,
                                               router_w, router_b, s_len)
        next_norm = (g_norm1[l + 1], mods[l + 1]) if l + 1 < depth else None
        xc, h1 = _moe(x1, h2, ri, rf, cnt, mod3, wg, wu, wd, l, s_len, next_norm)
    return _final_norm(xc, g_final, s_len)
```

```python
import functools
import math

import numpy as np
import jax
import jax.numpy as jnp
from jax import lax
from jax.experimental import pallas as pl
from jax.experimental.pallas import tpu as pltpu

GRID_W = 64
NA_HEADS = 4
NA_DIM = 128
NA_WIN_H = 8
NA_WIN_W = 16
DA_HEADS = 4
DA_DIM = 64
DA_VDIM = 2 * DA_DIM
ROPE_BASE = 10000.0
POOL_GROUPS = 4
POOL_WIDTH = 128
POOL_WINDOWS = (2, 4, 8, 16)
SG_GROUPS = 4
SG_WIDTH = 128
SG_CHUNK = 128
N_EXPERT_GROUPS = 4
EXPERTS_PER_GROUP = 4
TOP_K = 2
N_MOD = 6
EPS = 1e-6
NEG_INF = -1e30
LOG2_E = math.log2(math.e)

HEAD_W = 128
MIX_W = 512
ATT_COLS = 6 * MIX_W
REST_COLS = 3 * MIX_W
MOE_BLOCK = 256
NA_Q_ROWS = 4
NA_HEADS_PER_STEP = 4
DA_HEADS_PER_STEP = 2
ROW_TILE = 256
DA_KEY_CHUNKS = (1408, 768, 512, 256, 128)
DA_VT_ROWS = DA_VDIM + 16
VMEM_LIMIT = 56 * 1024 * 1024

F32 = jnp.float32
BF16 = jnp.bfloat16


def _cparams(n_axes, vmem=None):
    return pltpu.CompilerParams(dimension_semantics=("arbitrary",) * n_axes,
                                vmem_limit_bytes=vmem)


def _pick(n, candidates):
    for c in candidates:
        if n % c == 0:
            return c
    raise ValueError(f"no tile in {candidates} divides {n}")


def _ada_kernel(c_ref, w_ref, b_ref, o_ref):
    cs = jax.nn.silu(c_ref[...]).astype(BF16)
    o_ref[...] = jnp.dot(cs, w_ref[...].astype(BF16), preferred_element_type=F32) + b_ref[...]


def _ada_all(cond, w_ada, b_ada):
    depth, d, n6 = w_ada.shape
    tn = _pick(n6, (1024, 512, 256, 128))
    return pl.pallas_call(
        _ada_kernel,
        grid=(depth, n6 // tn),
        in_specs=[pl.BlockSpec((8, d), lambda l, j: (0, 0)),
                  pl.BlockSpec((None, d, tn), lambda l, j: (l, 0, j)),
                  pl.BlockSpec((None, 1, tn), lambda l, j: (l, 0, j))],
        out_specs=pl.BlockSpec((None, 8, tn), lambda l, j: (l, 0, j)),
        out_shape=jax.ShapeDtypeStruct((depth, 8, n6), F32),
        compiler_params=_cparams(2, VMEM_LIMIT),
        name="adaln",
    )(cond, w_ada, b_ada.reshape(depth, 1, n6))


def _row_is_ctx(i, tm, s_len):
    row = i * tm + lax.broadcasted_iota(jnp.int32, (tm, 1), 0)
    return row >= s_len


def _norm_mod(x, g, shx, scx, shc, scc, is_ctx):
    y = x * lax.rsqrt(jnp.mean(x * x, axis=-1, keepdims=True) + EPS) * g
    shift = jnp.where(is_ctx, shc, shx)
    scale = jnp.where(is_ctx, scc, scx)
    return y * (1.0 + scale) + shift


def _norm_mod_kernel(x_ref, g_ref, shx_ref, scx_ref, shc_ref, scc_ref, o_ref, *, s_len, tm):
    is_ctx = _row_is_ctx(pl.program_id(1), tm, s_len)
    h = _norm_mod(x_ref[...], g_ref[...], shx_ref[...], scx_ref[...], shc_ref[...], scc_ref[...], is_ctx)
    o_ref[...] = h.astype(o_ref.dtype)


def _norm_mod_call(xc, g, mod3, s_len, k_shift, k_scale):
    b, n, d = xc.shape
    tm = _pick(n, (ROW_TILE, 128))
    mx = lambda k: pl.BlockSpec((None, 1, d), lambda bi, i: (bi, 0, k))
    mc = lambda k: pl.BlockSpec((None, 1, d), lambda bi, i: (b, 0, k))
    return pl.pallas_call(
        functools.partial(_norm_mod_kernel, s_len=s_len, tm=tm),
        grid=(b, n // tm),
        in_specs=[pl.BlockSpec((None, tm, d), lambda bi, i: (bi, i, 0)),
                  pl.BlockSpec((1, d), lambda bi, i: (0, 0)),
                  mx(k_shift), mx(k_scale), mc(k_shift), mc(k_scale)],
        out_specs=pl.BlockSpec((None, tm, d), lambda bi, i: (bi, i, 0)),
        out_shape=jax.ShapeDtypeStruct((b, n, d), BF16),
        compiler_params=_cparams(2, VMEM_LIMIT),
        name="norm_mod",
    )(xc, g.reshape(1, d), mod3, mod3, mod3, mod3)


def _mm_kernel(a_ref, w_ref, o_ref, wbf_ref):
    @pl.when(pl.program_id(1) == 0)
    def _():
        wbf_ref[...] = w_ref[...].astype(BF16)

    o_ref[...] = jnp.dot(a_ref[...], wbf_ref[...], preferred_element_type=F32).astype(o_ref.dtype)


def _matmul_cols(a, w, layer, col0, ncols, out_dtype, tn):
    m, k = a.shape
    tm = _pick(m, (768, 512, 384, 256, 128))
    off = col0 // tn
    return pl.pallas_call(
        _mm_kernel,
        grid=(ncols // tn, m // tm),
        in_specs=[pl.BlockSpec((tm, k), lambda j, i: (i, 0)),
                  pl.BlockSpec((None, k, tn), lambda j, i: (layer, 0, j + off))],
        out_specs=pl.BlockSpec((tm, tn), lambda j, i: (i, j)),
        out_shape=jax.ShapeDtypeStruct((m, ncols), out_dtype),
        scratch_shapes=[pltpu.VMEM((k, tn), BF16)],
        compiler_params=_cparams(2, VMEM_LIMIT),
        name="in_proj",
    )(a, w)


def _rope_tables(s_len, c_len):
    t = jnp.arange(s_len)
    row = (t // GRID_W).astype(F32)
    col = (t % GRID_W).astype(F32)
    n_freq = DA_DIM // 4
    inv_freq = 1.0 / (ROPE_BASE ** (jnp.arange(n_freq, dtype=F32) / n_freq))
    ang = jnp.concatenate([row[:, None] * inv_freq, col[:, None] * inv_freq], axis=-1)
    cos = jnp.concatenate([jnp.cos(ang), jnp.ones((c_len, DA_DIM // 2), F32)], axis=0)
    sin = jnp.concatenate([jnp.sin(ang), jnp.zeros((c_len, DA_DIM // 2), F32)], axis=0)
    return jnp.tile(cos, (1, 4)), jnp.tile(jnp.concatenate([-sin, sin], axis=-1), (1, 2))


def _rope_kernel(q_ref, k_ref, v_ref, c_ref, s_ref, qt_ref, kr_ref, vt_ref, *, tm, q_scale):
    cos = c_ref[...]
    sin = s_ref[...]
    lane = lax.broadcasted_iota(jnp.int32, (tm, HEAD_W), 1)
    first_half = (lane % DA_DIM) < (DA_DIM // 2)

    def rotate(x):
        swapped = jnp.where(first_half, pltpu.roll(x, HEAD_W - DA_DIM // 2, 1), pltpu.roll(x, DA_DIM // 2, 1))
        return x * cos + swapped * sin

    for h in range(DA_HEADS):
        cols = slice(h * HEAD_W, (h + 1) * HEAD_W)
        q = rotate(q_ref[:, cols].astype(F32)) * q_scale
        qt_ref[cols, :] = q.T.astype(qt_ref.dtype)
        kr_ref[:, cols] = rotate(k_ref[:, cols].astype(F32)).astype(kr_ref.dtype)
        vt_ref[h * DA_VT_ROWS:h * DA_VT_ROWS + DA_VDIM, :] = v_ref[:, cols].astype(F32).T.astype(vt_ref.dtype)
        pad = lax.broadcasted_iota(jnp.int32, (DA_VT_ROWS - DA_VDIM, tm), 0)
        vt_ref[h * DA_VT_ROWS + DA_VDIM:(h + 1) * DA_VT_ROWS, :] = jnp.where(pad == 0, 1.0, 0.0).astype(vt_ref.dtype)


def _rope_call(pxa, cos, sin):
    b, n, _ = pxa.shape
    tm = _pick(n, (ROW_TILE, 128))
    col_spec = lambda k: pl.BlockSpec((None, tm, MIX_W), lambda bi, i: (bi, i, k))
    t_spec = lambda rows: pl.BlockSpec((None, rows, tm), lambda bi, i: (bi, 0, i))
    return pl.pallas_call(
        functools.partial(_rope_kernel, tm=tm, q_scale=DA_DIM ** -0.5 * LOG2_E),
        grid=(b, n // tm),
        in_specs=[col_spec(3), col_spec(4), col_spec(5),
                  pl.BlockSpec((tm, HEAD_W), lambda bi, i: (i, 0)),
                  pl.BlockSpec((tm, HEAD_W), lambda bi, i: (i, 0))],
        out_specs=[t_spec(MIX_W), col_spec(0), t_spec(DA_HEADS * DA_VT_ROWS)],
        out_shape=[jax.ShapeDtypeStruct((b, MIX_W, n), BF16),
                   jax.ShapeDtypeStruct((b, n, MIX_W), BF16),
                   jax.ShapeDtypeStruct((b, DA_HEADS * DA_VT_ROWS, n), BF16)],
        compiler_params=_cparams(2, VMEM_LIMIT),
        name="rope",
    )(pxa, pxa, pxa, cos, sin)


def _na_plan(rows, r_q):
    kh = min(NA_WIN_H, rows)
    key_rows = r_q + kh - 1
    assert rows % r_q == 0 and rows >= key_rows
    patterns, var, ustart = [], [], []
    for j in range(rows // r_q):
        r = r_q * j + np.arange(r_q)
        rs = np.clip(r - kh // 2, 0, rows - kh)
        u = int(np.clip(rs.min(), 0, rows - key_rows))
        kr = u + np.arange(key_rows)
        valid = (kr[None, :] >= rs[:, None]) & (kr[None, :] < rs[:, None] + kh)
        assert (valid.sum(1) == kh).all()
        dr0 = u - r + (NA_WIN_H - 1)
        key = (valid.tobytes(), dr0.tobytes())
        for v, (k2, _, _) in enumerate(patterns):
            if k2 == key:
                break
        else:
            v = len(patterns)
            patterns.append((key, valid, dr0))
        var.append(v)
        ustart.append(u * GRID_W)
    valid = np.stack([p[1] for p in patterns])
    dr0 = np.stack([p[2] for p in patterns])
    return np.asarray(var, np.int32), np.asarray(ustart, np.int32), valid, dr0


def _na_bias_tables(rpb, valid, dr0):
    col = np.arange(GRID_W)
    cs = np.clip(col - NA_WIN_W // 2, 0, GRID_W - NA_WIN_W)
    cmask = (col[None, :] >= cs[:, None]) & (col[None, :] < cs[:, None] + NA_WIN_W)
    dc = np.clip(col[None, :] - col[:, None], -(NA_WIN_W - 1), NA_WIN_W - 1) + (NA_WIN_W - 1)
    heads, n_dr, n_dc = rpb.shape
    v, r_q, key_rows = valid.shape
    pick = (dc.reshape(1, -1) == np.arange(n_dc)[:, None]).astype(np.float32)
    by_col = jnp.dot(rpb.astype(F32).reshape(heads * n_dr, n_dc), pick,
                     precision=lax.Precision.HIGHEST).reshape(heads, n_dr, GRID_W, GRID_W)
    padded = jnp.pad(by_col, ((0, 0), (key_rows, key_rows), (0, 0), (0, 0)))
    tab = jnp.stack([lax.slice_in_dim(padded, int(d) + key_rows, int(d) + 2 * key_rows, axis=1)
                     for d in dr0.reshape(-1)], axis=1)
    tab = tab.reshape(heads, v, r_q, key_rows, GRID_W, GRID_W).transpose(0, 1, 2, 4, 3, 5)
    mask = valid[:, :, None, :, None] & cmask[None, None, :, None, :]
    tab = jnp.where(mask[None], tab, NEG_INF)
    return tab.reshape(heads, v, r_q * GRID_W, key_rows * GRID_W)


def _na_kernel(var_ref, u_ref, q_ref, k_ref, v_ref, kc_ref, vc_ref, bias_ref, o_ref, *, n_lat, kw, scale):
    j = pl.program_id(2)
    nt = (((1,), (1,)), ((), ()))

    def one_head(hh, local):
        cols = slice(hh * HEAD_W, (hh + 1) * HEAD_W)
        q = q_ref[:, cols]
        vc = vc_ref[:, cols]
        s_ctx = lax.dot_general(q, kc_ref[:, cols], nt, preferred_element_type=F32) * scale
        m = jnp.max(s_ctx, axis=-1, keepdims=True)
        if local:
            u = pl.multiple_of(u_ref[j], GRID_W)
            s = lax.dot_general(q, k_ref[pl.ds(u, kw), cols], nt, preferred_element_type=F32) * scale + bias_ref[hh]
            m = jnp.maximum(m, jnp.max(s, axis=-1, keepdims=True))
            p = jnp.exp(s - m)
        pc = jnp.exp(s_ctx - m)
        l = jnp.sum(pc, axis=-1, keepdims=True)
        o = jnp.dot(pc.astype(BF16), vc, preferred_element_type=F32)
        if local:
            l += jnp.sum(p, axis=-1, keepdims=True)
            o += jnp.dot(p.astype(BF16), v_ref[pl.ds(u, kw), cols], preferred_element_type=F32)
        o_ref[:, cols] = (o / l).astype(o_ref.dtype)

    @pl.when(j < n_lat)
    def _():
        for hh in range(NA_HEADS_PER_STEP):
            one_head(hh, True)

    @pl.when(j >= n_lat)
    def _():
        for hh in range(NA_HEADS_PER_STEP):
            one_head(hh, False)


def _na_call(pxa, rpb, s_len, c_len):
    b, n, _ = pxa.shape
    rows = s_len // GRID_W
    tq = NA_Q_ROWS * GRID_W
    hps = NA_HEADS_PER_STEP
    assert s_len % tq == 0 and c_len % tq == 0 and s_len % c_len == 0 and NA_HEADS % hps == 0
    var, ustart, valid, dr0 = _na_plan(rows, NA_Q_ROWS)
    n_lat, n_q = s_len // tq, n // tq
    kw = valid.shape[2] * GRID_W
    tabs = _na_bias_tables(rpb, valid, dr0)
    pad = np.zeros(n_q - n_lat, np.int32)
    var = jnp.asarray(np.concatenate([var, pad]))
    ustart = jnp.asarray(np.concatenate([ustart, pad]))
    ctx_blk = s_len // c_len
    wide = hps * HEAD_W
    k_blk, v_blk = NA_HEADS // hps, 2 * NA_HEADS // hps
    grid_spec = pltpu.PrefetchScalarGridSpec(
        num_scalar_prefetch=2,
        grid=(b, NA_HEADS // hps, n_q),
        in_specs=[pl.BlockSpec((None, tq, wide), lambda bi, h, j, vr, ur: (bi, j, h)),
                  pl.BlockSpec((None, s_len, wide), lambda bi, h, j, vr, ur: (bi, 0, k_blk + h)),
                  pl.BlockSpec((None, s_len, wide), lambda bi, h, j, vr, ur: (bi, 0, v_blk + h)),
                  pl.BlockSpec((None, c_len, wide), lambda bi, h, j, vr, ur: (bi, ctx_blk, k_blk + h)),
                  pl.BlockSpec((None, c_len, wide), lambda bi, h, j, vr, ur: (bi, ctx_blk, v_blk + h)),
                  pl.BlockSpec((hps, None, tq, kw), lambda bi, h, j, vr, ur: (h, vr[j], 0, 0))],
        out_specs=pl.BlockSpec((None, tq, wide), lambda bi, h, j, vr, ur: (bi, j, h)),
    )
    return pl.pallas_call(
        functools.partial(_na_kernel, n_lat=n_lat, kw=kw, scale=NA_DIM ** -0.5),
        grid_spec=grid_spec,
        out_shape=jax.ShapeDtypeStruct((b, n, MIX_W), BF16),
        compiler_params=_cparams(3, VMEM_LIMIT),
        name="nbr_attn",
    )(var, ustart, pxa, pxa, pxa, pxa, pxa, tabs)


def _da_kernel(qt_ref, k_ref, vt_ref, lam_ref, g_ref, o_ref, *score_refs, s_len, c_len, tq, tk, lam_init, heads):
    sa_refs, sb_refs = score_refs[0::2], score_refs[1::2]
    qi = pl.program_id(2)
    n = s_len + c_len
    feat = lax.broadcasted_iota(jnp.int32, (2 * DA_DIM, tq), 0)

    def q_tiles(idx):
        start = pl.multiple_of(idx * tq, tq)
        out = []
        for hh in range(heads):
            qt = qt_ref[hh * HEAD_W:(hh + 1) * HEAD_W, pl.ds(start, tq)]
            zero = jnp.zeros_like(qt)
            out.append(jnp.concatenate([jnp.where(feat < DA_DIM, qt, zero), jnp.where(feat >= DA_DIM, qt, zero)],
                                       axis=1))
        return out

    q2 = q_tiles(qi)

    def k_rows(hh, rows):
        return k_ref[rows, hh * HEAD_W:(hh + 1) * HEAD_W]

    def vt_cols(hh, cols):
        return vt_ref[hh * DA_VT_ROWS:(hh + 1) * DA_VT_ROWS, cols]

    def scores(k_blk, q):
        return jnp.dot(k_blk, q, preferred_element_type=F32)

    def update(s, vt_blk, carry):
        m, acc = carry
        m_new = jnp.maximum(m, jnp.max(s, axis=0, keepdims=True))
        alpha = jnp.exp2(m - m_new)
        p = jnp.exp2(s - m_new)
        acc_new = alpha * acc + jnp.dot(vt_blk, p.astype(BF16), preferred_element_type=F32)
        return m_new, acc_new

    def init():
        one = (jnp.full((1, 2 * tq), NEG_INF, F32), jnp.zeros((DA_VT_ROWS, 2 * tq), F32))
        return tuple(one for _ in range(heads))

    def finalize(stats):
        lv = lam_ref[...]
        t1 = jnp.sum(lv[0:1, :] * lv[1:2, :], axis=-1, keepdims=True)
        t2 = jnp.sum(lv[2:3, :] * lv[3:4, :], axis=-1, keepdims=True)
        lam = jnp.exp(t1) - jnp.exp(t2) + lam_init
        for hh in range(heads):
            _, acc = stats[hh]
            o = acc[:DA_VDIM, :] / acc[DA_VDIM:DA_VDIM + 1, :]
            o = o[:, :tq] - lam * o[:, tq:]
            y = o * lax.rsqrt(jnp.mean(o * o, axis=0, keepdims=True) + EPS) * g_ref[...]
            o_ref[:, hh * HEAD_W:(hh + 1) * HEAD_W] = (y * (1.0 - lam_init)).T.astype(o_ref.dtype)

    n_steps = n // tk
    full_pairs = (n_steps - 1) // 2
    carry_first = n_steps % 2 == 0

    @pl.when(qi < s_len // tq)
    def _():
        def scores_into(dst_refs, step, qs):
            rows = pl.ds(pl.multiple_of(step * tk, tk), tk)
            for hh in range(heads):
                dst_refs[hh][...] = scores(k_rows(hh, rows), qs[hh])

        def update_from(src_refs, step, stats):
            cols = pl.ds(pl.multiple_of(step * tk, tk), tk)
            return tuple(update(src_refs[hh][...], vt_cols(hh, cols), stats[hh]) for hh in range(heads))

        def pair(j, stats):
            c0 = 2 * j
            scores_into(sb_refs, c0 + 1, q2)
            stats = update_from(sa_refs, c0, stats)
            scores_into(sa_refs, c0 + 2, q2)
            return update_from(sb_refs, c0 + 1, stats)

        if carry_first:
            @pl.when(qi == 0)
            def _():
                scores_into(sa_refs, 0, q2)
        else:
            scores_into(sa_refs, 0, q2)
        stats = lax.fori_loop(0, full_pairs, pair, init())
        c0 = 2 * full_pairs
        if carry_first:
            scores_into(sb_refs, c0 + 1, q2)
            stats = update_from(sa_refs, c0, stats)
            scores_into(sa_refs, 0, q_tiles(qi + 1))
            stats = update_from(sb_refs, c0 + 1, stats)
        else:
            stats = update_from(sa_refs, c0, stats)
        finalize(stats)

    @pl.when(qi >= s_len // tq)
    def _():
        rows, stats = slice(s_len, n), init()
        finalize(tuple(update(scores(k_rows(hh, rows), q2[hh]), vt_cols(hh, rows), stats[hh])
                       for hh in range(heads)))


def _da_call(qt, kr, vt, lam_vecs, subln_g, s_len, c_len, layer):
    b, n, _ = kr.shape
    tq = _pick(math.gcd(s_len, c_len), (256, 128))
    tk = _pick(n, DA_KEY_CHUNKS)
    hps = DA_HEADS_PER_STEP
    assert DA_HEADS % hps == 0
    lam_init = 0.8 - 0.6 * math.exp(-0.3 * layer)
    return pl.pallas_call(
        functools.partial(_da_kernel, s_len=s_len, c_len=c_len, tq=tq, tk=tk, lam_init=lam_init, heads=hps),
        grid=(b, DA_HEADS // hps, n // tq),
        in_specs=[pl.BlockSpec((None, hps * HEAD_W, n), lambda bi, h, i: (bi, h, 0)),
                  pl.BlockSpec((None, n, hps * HEAD_W), lambda bi, h, i: (bi, 0, h)),
                  pl.BlockSpec((None, hps * DA_VT_ROWS, n), lambda bi, h, i: (bi, h, 0)),
                  pl.BlockSpec((4, DA_DIM), lambda bi, h, i: (0, 0)),
                  pl.BlockSpec((DA_VDIM, 1), lambda bi, h, i: (0, 0))],
        out_specs=pl.BlockSpec((None, tq, hps * HEAD_W), lambda bi, h, i: (bi, i, h)),
        out_shape=jax.ShapeDtypeStruct((b, n, MIX_W), BF16),
        scratch_shapes=[pltpu.VMEM((tk, 2 * tq), F32)] * (2 * hps),
        compiler_params=_cparams(3, VMEM_LIMIT),
        name="diff_attn",
    )(qt, kr, vt, lam_vecs, subln_g.reshape(DA_VDIM, 1))


POOL_HALO = 8


def _pool_kernel(prev_ref, cur_ref, next_ref, w_ref, sc_ref, o_ref, buf_ref, *, s_len, c_len, tm):
    i = pl.program_id(1)
    n_lat = s_len // tm
    n_all = (s_len + c_len) // tm
    seq_start = (i == 0) | (i == n_lat)
    seq_end = (i == n_lat - 1) | (i == n_all - 1)
    buf_ref[0:POOL_HALO, :] = jnp.where(seq_start, 0.0, prev_ref[...])
    buf_ref[POOL_HALO:POOL_HALO + tm, :] = cur_ref[...]
    buf_ref[POOL_HALO + tm:, :] = jnp.where(seq_end, 0.0, next_ref[...])
    in_ctx = i >= n_lat
    seq_len = jnp.where(in_ctx, c_len, s_len)
    pos = i * tm - jnp.where(in_ctx, s_len, 0) + lax.broadcasted_iota(jnp.int32, (tm, 1), 0)
    for g, win in enumerate(POOL_WINDOWS):
        half = win // 2
        cols = slice(g * POOL_WIDTH, (g + 1) * POOL_WIDTH)
        tot = buf_ref[POOL_HALO - half:POOL_HALO - half + tm, cols]
        for d in range(-half + 1, half):
            tot = tot + buf_ref[POOL_HALO + d:POOL_HALO + d + tm, cols]
        cnt = jnp.clip(pos + half, 0, seq_len) - jnp.clip(pos - half, 0, seq_len)
        resid = tot / cnt.astype(F32) - cur_ref[:, cols]
        y = jnp.dot(resid.astype(BF16), w_ref[g].astype(BF16), preferred_element_type=F32)
        o_ref[:, cols] = (y * sc_ref[:, cols]).astype(o_ref.dtype)


def _pool_call(pxb, pool_w, pool_scale, s_len, c_len):
    b, n, _ = pxb.shape
    tm = _pick(math.gcd(s_len, c_len), (ROW_TILE, 128))
    hb = tm // POOL_HALO
    last = n // POOL_HALO - 1
    return pl.pallas_call(
        functools.partial(_pool_kernel, s_len=s_len, c_len=c_len, tm=tm),
        grid=(b, n // tm),
        in_specs=[pl.BlockSpec((None, POOL_HALO, MIX_W), lambda bi, i: (bi, jnp.maximum(i * hb - 1, 0), 0)),
                  pl.BlockSpec((None, tm, MIX_W), lambda bi, i: (bi, i, 0)),
                  pl.BlockSpec((None, POOL_HALO, MIX_W), lambda bi, i: (bi, jnp.minimum((i + 1) * hb, last), 0)),
                  pl.BlockSpec((POOL_GROUPS, POOL_WIDTH, POOL_WIDTH), lambda bi, i: (0, 0, 0)),
                  pl.BlockSpec((1, MIX_W), lambda bi, i: (0, 0))],
        out_specs=pl.BlockSpec((None, tm, MIX_W), lambda bi, i: (bi, i, 0)),
        out_shape=jax.ShapeDtypeStruct((b, n, MIX_W), BF16),
        scratch_shapes=[pltpu.VMEM((tm + 2 * POOL_HALO, MIX_W), F32)],
        compiler_params=_cparams(2, VMEM_LIMIT),
        name="pool",
    )(pxb, pxb, pxb, pool_w, pool_scale.reshape(1, MIX_W))


def _sg_kernel(u_ref, v_ref, g_ref, w_ref, b_ref, o_ref, *, tm):
    v = jax.nn.gelu(v_ref[...])
    vn = (v * lax.rsqrt(jnp.mean(v * v, axis=-1, keepdims=True) + EPS) * g_ref[...]).astype(BF16)
    bias = b_ref[...]
    for c in range(tm // SG_CHUNK):
        rows = slice(c * SG_CHUNK, (c + 1) * SG_CHUNK)
        for g in range(SG_GROUPS):
            cols = slice(g * SG_WIDTH, (g + 1) * SG_WIDTH)
            mixed = jnp.dot(w_ref[g].astype(BF16), vn[rows, cols], preferred_element_type=F32)
            mixed = mixed + bias[:, g:g + 1]
            o_ref[rows, cols] = (jax.nn.gelu(u_ref[rows, cols]) * mixed).astype(o_ref.dtype)


def _sg_call(pxb, sg_norm_g, sg_w, sg_b):
    b, n, _ = pxb.shape
    tm = _pick(n, (ROW_TILE, 128))
    return pl.pallas_call(
        functools.partial(_sg_kernel, tm=tm),
        grid=(b, n // tm),
        in_specs=[pl.BlockSpec((None, tm, MIX_W), lambda bi, i: (bi, i, 1)),
                  pl.BlockSpec((None, tm, MIX_W), lambda bi, i: (bi, i, 2)),
                  pl.BlockSpec((1, MIX_W), lambda bi, i: (0, 0)),
                  pl.BlockSpec((SG_GROUPS, SG_CHUNK, SG_CHUNK), lambda bi, i: (0, 0, 0)),
                  pl.BlockSpec((SG_CHUNK, SG_GROUPS), lambda bi, i: (0, 0))],
        out_specs=pl.BlockSpec((None, tm, MIX_W), lambda bi, i: (bi, i, 0)),
        out_shape=jax.ShapeDtypeStruct((b, n, MIX_W), BF16),
        compiler_params=_cparams(2, VMEM_LIMIT),
        name="spatial_gate",
    )(pxb, pxb, sg_norm_g.reshape(1, MIX_W), sg_w, sg_b.T)


def _pack_bf16_pairs(x):
    w = x.shape[1] // 2
    hi = lax.bitcast_convert_type(x[:, :w].astype(BF16).astype(F32), jnp.uint32)
    lo = lax.bitcast_convert_type(x[:, w:].astype(BF16).astype(F32), jnp.uint32)
    return hi | (lo >> 16)


def _unpack_bf16_pairs(u):
    hi = lax.bitcast_convert_type(u & jnp.uint32(0xFFFF0000), F32).astype(BF16)
    lo = lax.bitcast_convert_type(u << 16, F32).astype(BF16)
    return jnp.concatenate([hi, lo], axis=1)


def _first_argmax(vals):
    best, idx = vals[0], jnp.zeros(vals[0].shape, jnp.int32)
    for k in range(1, len(vals)):
        take = vals[k] > best
        best = jnp.where(take, vals[k], best)
        idx = jnp.where(take, k, idx)
    return idx, best


def _out_router_kernel(ma_ref, mb_ref, mc_ref, md_ref, w_ref, x_ref, gx_ref, gc_ref,
                       g_ref, shx_ref, scx_ref, shc_ref, scc_ref, rw_ref, rb_ref,
                       x1_ref, h_ref, ri_ref, rf_ref, cnt_ref, run_ref, *, s_len, tm, n_exp):
    bi, i = pl.program_id(0), pl.program_id(1)

    @pl.when((bi == 0) & (i == 0))
    def _():
        run_ref[...] = jnp.zeros_like(run_ref)

    mix = jnp.concatenate([ma_ref[...], mb_ref[...], mc_ref[...], md_ref[...]], axis=1)
    acc = jnp.dot(mix, w_ref[...], preferred_element_type=F32)
    is_ctx = _row_is_ctx(i, tm, s_len)
    x1 = x_ref[...] + jnp.where(is_ctx, gc_ref[...], gx_ref[...]) * acc
    x1_ref[...] = x1
    h = _norm_mod(x1, g_ref[...], shx_ref[...], scx_ref[...], shc_ref[...], scc_ref[...], is_ctx)
    h_ref[...] = _pack_bf16_pairs(h)

    h_hi = h.astype(BF16)
    h_lo = (h - h_hi.astype(F32)).astype(BF16)
    part = jnp.dot(h_hi, rw_ref[0], preferred_element_type=F32)
    part += jnp.dot(h_lo, rw_ref[1], preferred_element_type=F32)
    part_t = part.T
    logits = part_t[0:n_exp, :] + part_t[n_exp:2 * n_exp, :]
    scores = jax.nn.sigmoid(logits)
    biased = scores + rb_ref[...]
    b_rows = [biased[e:e + 1, :] for e in range(n_exp)]
    s_rows = [scores[e:e + 1, :] for e in range(n_exp)]
    epg = EXPERTS_PER_GROUP
    group_scores = []
    for g in range(N_EXPERT_GROUPS):
        r = b_rows[g * epg:(g + 1) * epg]
        pair = None
        for a in range(epg):
            for c in range(a + 1, epg):
                t = r[a] + r[c]
                pair = t if pair is None else jnp.maximum(pair, t)
        group_scores.append(pair)
    group, _ = _first_argmax(group_scores)
    in_group = []
    for k in range(epg):
        sel = b_rows[k]
        for g in range(1, N_EXPERT_GROUPS):
            sel = jnp.where(group == g, b_rows[g * epg + k], sel)
        in_group.append(sel)
    loc0, _ = _first_argmax(in_group)
    rest = [jnp.where(loc0 == k, -jnp.inf, in_group[k]) for k in range(epg)]
    loc1, _ = _first_argmax(rest)
    e0 = group * epg + loc0
    e1 = group * epg + loc1
    g0 = jnp.zeros_like(s_rows[0])
    g1 = jnp.zeros_like(s_rows[0])
    for e in range(n_exp):
        g0 = jnp.where(e0 == e, s_rows[e], g0)
        g1 = jnp.where(e1 == e, s_rows[e], g1)
    tot = g0 + g1

    eidx = lax.broadcasted_iota(jnp.int32, (n_exp, tm), 0)
    hit0 = eidx == e0
    hit1 = eidx == e1
    onehot = jnp.where(hit0 | hit1, 1.0, 0.0)
    before = lax.broadcasted_iota(jnp.int32, (tm, tm), 0) < lax.broadcasted_iota(jnp.int32, (tm, tm), 1)
    tri = jnp.where(before, 1.0, 0.0).astype(BF16)
    run = run_ref[:, 0:1]
    rank = jnp.dot(onehot.astype(BF16), tri, preferred_element_type=F32) + run
    rank0 = jnp.sum(jnp.where(hit0, rank, 0.0), axis=0, keepdims=True)
    rank1 = jnp.sum(jnp.where(hit1, rank, 0.0), axis=0, keepdims=True)
    new_run = run + jnp.sum(onehot, axis=1, keepdims=True)
    run_ref[...] = jnp.broadcast_to(new_run, run_ref.shape)
    cnt_ref[...] = jnp.broadcast_to(new_run, cnt_ref.shape).astype(jnp.int32)

    zi = jnp.zeros((4, tm), jnp.int32)
    ri_ref[...] = jnp.concatenate([e0, e1, rank0.astype(jnp.int32), rank1.astype(jnp.int32), zi], axis=0)
    zf = jnp.zeros((6, tm), F32)
    rf_ref[...] = jnp.concatenate([g0 / tot, g1 / tot, zf], axis=0)


def _router_slabs(router_w):
    d, n_exp = router_w.shape
    w_hi = router_w.astype(BF16)
    w_lo = (router_w - w_hi.astype(F32)).astype(BF16)
    zeros = lambda cols: jnp.zeros((d, cols), BF16)
    return jnp.stack([jnp.concatenate([w_hi, w_lo, zeros(HEAD_W - 2 * n_exp)], axis=1),
                      jnp.concatenate([w_hi, zeros(HEAD_W - n_exp)], axis=1)])


def _out_router_call(mixes, w_out_bf, layer, xc, g2, mod3, router_w, router_b, s_len):
    b, n, d = xc.shape
    n_exp = router_w.shape[1]
    tm = _pick(n, (ROW_TILE, 128))
    per_b = n // tm
    mx = lambda k: pl.BlockSpec((None, 1, d), lambda bi, i: (bi, 0, k))
    mc = lambda k: pl.BlockSpec((None, 1, d), lambda bi, i: (b, 0, k))
    mix_spec = pl.BlockSpec((None, tm, MIX_W), lambda bi, i: (bi, i, 0))
    row_spec = pl.BlockSpec((None, tm, d), lambda bi, i: (bi, i, 0))
    tok_spec = pl.BlockSpec((8, tm), lambda bi, i: (0, bi * per_b + i))
    return pl.pallas_call(
        functools.partial(_out_router_kernel, s_len=s_len, tm=tm, n_exp=n_exp),
        grid=(b, per_b),
        in_specs=[mix_spec, mix_spec, mix_spec, mix_spec,
                  pl.BlockSpec((None, 4 * MIX_W, d), lambda bi, i: (layer, 0, 0)),
                  row_spec, mx(2), mc(2),
                  pl.BlockSpec((1, d), lambda bi, i: (0, 0)),
                  mx(3), mx(4), mc(3), mc(4),
                  pl.BlockSpec((2, d, HEAD_W), lambda bi, i: (0, 0, 0)),
                  pl.BlockSpec((n_exp, 1), lambda bi, i: (0, 0))],
        out_specs=[row_spec, pl.BlockSpec((None, tm, d // 2), lambda bi, i: (bi, i, 0)), tok_spec, tok_spec,
                   pl.BlockSpec((n_exp, HEAD_W), lambda bi, i: (0, 0))],
        out_shape=[jax.ShapeDtypeStruct((b, n, d), F32),
                   jax.ShapeDtypeStruct((b, n, d // 2), jnp.uint32),
                   jax.ShapeDtypeStruct((8, b * n), jnp.int32),
                   jax.ShapeDtypeStruct((8, b * n), F32),
                   jax.ShapeDtypeStruct((n_exp, HEAD_W), jnp.int32)],
        scratch_shapes=[pltpu.VMEM((n_exp, HEAD_W), F32)],
        compiler_params=_cparams(2, VMEM_LIMIT),
        name="out_router",
    )(*mixes, w_out_bf, xc, mod3, mod3, g2.reshape(1, d), mod3, mod3, mod3, mod3,
      _router_slabs(router_w), router_b.reshape(n_exp, 1))


def _row_copy(src_ref, src_row, dst_ref, dst_row, sem):
    return pltpu.make_async_copy(src_ref.at[pl.ds(src_row, 1)], dst_ref.at[pl.ds(dst_row, 1)], sem)


def _dispatch_kernel(p0_ref, p1_ref, h_ref, xs_in_ref, xs_ref, sem, *, tm):
    del xs_in_ref
    base = pl.program_id(0) * tm

    for r in range(tm):
        _row_copy(h_ref, r, xs_ref, p0_ref[base + r], sem).start()
        _row_copy(h_ref, r, xs_ref, p1_ref[base + r], sem).start()

    tile_copy = pltpu.make_async_copy(h_ref, xs_ref.at[pl.ds(0, tm)], sem)
    tile_copy.wait()
    tile_copy.wait()


def _dispatch_call(h2, pos0, pos1, n_pad):
    t, d = h2.shape
    tm = _pick(t, (ROW_TILE, 128))
    grid_spec = pltpu.PrefetchScalarGridSpec(
        num_scalar_prefetch=2,
        grid=(t // tm,),
        in_specs=[pl.BlockSpec((tm, d), lambda i, p0, p1: (i, 0)),
                  pl.BlockSpec(memory_space=pl.ANY)],
        out_specs=pl.BlockSpec(memory_space=pl.ANY),
        scratch_shapes=[pltpu.SemaphoreType.DMA],
    )
    return pl.pallas_call(
        functools.partial(_dispatch_kernel, tm=tm),
        grid_spec=grid_spec,
        out_shape=jax.ShapeDtypeStruct((n_pad, d), h2.dtype),
        input_output_aliases={3: 0},
        compiler_params=_cparams(1, VMEM_LIMIT),
        name="dispatch",
    )(pos0, pos1, h2, jnp.zeros((n_pad, d), h2.dtype))


def _ffn_kernel(be_ref, nb_ref, x_ref, wg_ref, wu_ref, wd_ref, y_ref):
    blk = pl.program_id(0)

    @pl.when(blk < nb_ref[0])
    def _():
        x = _unpack_bf16_pairs(x_ref[...])
        gate = jnp.dot(x, wg_ref[...], preferred_element_type=F32)
        up = jnp.dot(x, wu_ref[...], preferred_element_type=F32)
        hid = (jax.nn.silu(gate) * up).astype(BF16)
        y_ref[...] = jnp.dot(hid, wd_ref[...], preferred_element_type=F32)

    @pl.when(blk >= nb_ref[0])
    def _():
        y_ref[...] = jnp.zeros_like(y_ref)


def _ffn_call(xs, blk_expert, n_used, wg, wu, wd, layer):
    n_pad, half = xs.shape
    d = 2 * half
    d_exp = wg.shape[3]
    grid_spec = pltpu.PrefetchScalarGridSpec(
        num_scalar_prefetch=2,
        grid=(n_pad // MOE_BLOCK,),
        in_specs=[pl.BlockSpec((MOE_BLOCK, half), lambda i, be, nb: (i, 0)),
                  pl.BlockSpec((None, None, d, d_exp), lambda i, be, nb: (layer, be[i], 0, 0)),
                  pl.BlockSpec((None, None, d, d_exp), lambda i, be, nb: (layer, be[i], 0, 0)),
                  pl.BlockSpec((None, None, d_exp, d), lambda i, be, nb: (layer, be[i], 0, 0))],
        out_specs=pl.BlockSpec((MOE_BLOCK, d), lambda i, be, nb: (i, 0)),
    )
    return pl.pallas_call(
        _ffn_kernel,
        grid_spec=grid_spec,
        out_shape=jax.ShapeDtypeStruct((n_pad, d), F32),
        compiler_params=_cparams(1, VMEM_LIMIT),
        name="expert_ffn",
    )(blk_expert, n_used, xs, wg, wu, wd)


def _combine_kernel(p0_ref, p1_ref, ys_ref, x_ref, gt_ref, gx_ref, gc_ref, *rest, s_len, n_len, tm, with_next):
    if with_next:
        g_ref, shx_ref, scx_ref, shc_ref, scc_ref, o_ref, h_ref, y0_ref, y1_ref, sem = rest
    else:
        o_ref, y0_ref, y1_ref, sem = rest
    i = pl.program_id(0)
    last = pl.num_programs(0) - 1

    def start_tile(tile, slot):
        base = tile * tm
        for r in range(tm):
            _row_copy(ys_ref, p0_ref[base + r], y0_ref.at[slot], r, sem.at[slot]).start()
            _row_copy(ys_ref, p1_ref[base + r], y1_ref.at[slot], r, sem.at[slot]).start()

    def wait_tile(slot):
        pltpu.make_async_copy(ys_ref.at[pl.ds(0, tm)], y0_ref.at[slot], sem.at[slot]).wait()
        pltpu.make_async_copy(ys_ref.at[pl.ds(0, tm)], y1_ref.at[slot], sem.at[slot]).wait()

    def combine(slot):
        row = ((i * tm) % n_len) + lax.broadcasted_iota(jnp.int32, (tm, 1), 0)
        is_ctx = row >= s_len
        gate = jnp.where(is_ctx, gc_ref[...], gx_ref[...])
        gt = gt_ref[...]
        y = gt[:, 0:1] * y0_ref[slot] + gt[:, 1:2] * y1_ref[slot]
        x_new = x_ref[...] + gate * y
        o_ref[...] = x_new
        if with_next:
            h = _norm_mod(x_new, g_ref[...], shx_ref[...], scx_ref[...], shc_ref[...], scc_ref[...], is_ctx)
            h_ref[...] = h.astype(h_ref.dtype)

    @pl.when(i == 0)
    def _():
        start_tile(0, 0)

    for slot in (0, 1):
        @pl.when(i % 2 == slot)
        def _(slot=slot):
            wait_tile(slot)
            start_tile(jnp.minimum(i + 1, last), 1 - slot)
            combine(slot)

            @pl.when(i == last)
            def _():
                wait_tile(1 - slot)


def _combine_call(ys, pos0, pos1, x1, gates, mod3, s_len, k_gate, next_norm=None):
    b, n, d = x1.shape
    t = b * n
    tm = _pick(n, (ROW_TILE, 128))
    row_spec = pl.BlockSpec((tm, d), lambda i, p0, p1: (i, 0))
    mx = lambda k: pl.BlockSpec((None, 1, d), lambda i, p0, p1: ((i * tm) // n, 0, k))
    mc = lambda k: pl.BlockSpec((None, 1, d), lambda i, p0, p1: (b, 0, k))
    in_specs = [pl.BlockSpec(memory_space=pl.ANY), row_spec,
                pl.BlockSpec((tm, TOP_K), lambda i, p0, p1: (i, 0)), mx(k_gate), mc(k_gate)]
    args = [pos0, pos1, ys, x1.reshape(t, d), gates, mod3, mod3]
    out_specs, out_shape = [row_spec], [jax.ShapeDtypeStruct((t, d), F32)]
    if next_norm is not None:
        g_next, mod_next = next_norm
        in_specs += [pl.BlockSpec((1, d), lambda i, p0, p1: (0, 0)), mx(0), mx(1), mc(0), mc(1)]
        args += [g_next.reshape(1, d), mod_next, mod_next, mod_next, mod_next]
        out_specs.append(row_spec)
        out_shape.append(jax.ShapeDtypeStruct((t, d), BF16))
    grid_spec = pltpu.PrefetchScalarGridSpec(
        num_scalar_prefetch=2,
        grid=(t // tm,),
        in_specs=in_specs,
        out_specs=out_specs,
        scratch_shapes=[pltpu.VMEM((2, tm, d), F32), pltpu.VMEM((2, tm, d), F32), pltpu.SemaphoreType.DMA((2,))],
    )
    outs = pl.pallas_call(
        functools.partial(_combine_kernel, s_len=s_len, n_len=n, tm=tm, with_next=next_norm is not None),
        grid_spec=grid_spec,
        out_shape=out_shape,
        compiler_params=_cparams(1, VMEM_LIMIT),
        name="combine",
    )(*args)
    return outs[0].reshape(b, n, d), (outs[1] if next_norm is not None else None)


def _moe(x1, h2, ri, rf, cnt, mod3, wg, wu, wd, layer, s_len, next_norm):
    b, n, d = x1.shape
    t = b * n
    n_exp = cnt.shape[0]
    counts = cnt[:, 0]
    n_blk = (counts + MOE_BLOCK - 1) // MOE_BLOCK
    blk_end = jnp.cumsum(n_blk)
    starts = (blk_end - n_blk) * MOE_BLOCK
    pos0 = (starts[ri[0]] + ri[2]).astype(jnp.int32)
    pos1 = (starts[ri[1]] + ri[3]).astype(jnp.int32)
    n_pad = t * TOP_K + n_exp * MOE_BLOCK
    n_used = blk_end[-1:].astype(jnp.int32)
    blk_ids = jnp.minimum(jnp.arange(n_pad // MOE_BLOCK, dtype=jnp.int32), n_used[0] - 1)
    blk_expert = jnp.sum(blk_ids[:, None] >= blk_end[None, :], axis=1).astype(jnp.int32)
    xs = _dispatch_call(h2.reshape(t, d // 2), pos0, pos1, n_pad)
    ys = _ffn_call(xs, blk_expert, n_used, wg, wu, wd, layer)
    return _combine_call(ys, pos0, pos1, x1, rf[:TOP_K].T, mod3, s_len, 5, next_norm)


def _final_norm_kernel(x_ref, g_ref, o_ref):
    x = x_ref[...]
    o_ref[...] = x * lax.rsqrt(jnp.mean(x * x, axis=-1, keepdims=True) + EPS) * g_ref[...]


def _final_norm(xc, g, s_len):
    b, _, d = xc.shape
    tm = _pick(s_len, (ROW_TILE, 128))
    return pl.pallas_call(
        _final_norm_kernel,
        grid=(b, s_len // tm),
        in_specs=[pl.BlockSpec((None, tm, d), lambda bi, i: (bi, i, 0)),
                  pl.BlockSpec((1, d), lambda bi, i: (0, 0))],
        out_specs=pl.BlockSpec((None, tm, d), lambda bi, i: (bi, i, 0)),
        out_shape=jax.ShapeDtypeStruct((b, s_len, d), F32),
        compiler_params=_cparams(2, VMEM_LIMIT),
        name="final_norm",
    )(xc, g.reshape(1, d))


def kernel(x, c, ctx, c_ctx, w_ada, b_ada, g_norm1, g_norm2, w_in, w_out, na_rpb, da_lambda, da_subln_g,
           pool_w, pool_scale, sg_norm_g, sg_w, sg_b, router_w, router_b, exp_w_gate, exp_w_up, exp_w_down,
           g_final):
    b, s_len, d = x.shape
    c_len = ctx.shape[1]
    n = s_len + c_len
    depth = w_ada.shape[0]
    assert b + 1 <= 8 and w_in.shape[2] == ATT_COLS + REST_COLS and w_out.shape[1] == 4 * MIX_W

    cond = jnp.zeros((8, d), F32).at[:b].set(c).at[b].set(c_ctx)
    mod_all = _ada_all(cond, w_ada, b_ada)
    cos, sin = _rope_tables(s_len, c_len)
    xc = jnp.concatenate([x, ctx], axis=1)
    wg, wu, wd = exp_w_gate.astype(BF16), exp_w_up.astype(BF16), exp_w_down.astype(BF16)
    w_out_bf = w_out.astype(BF16)
    mods = [mod_all[l].reshape(8, 1, N_MOD * d) for l in range(depth)]

    h1 = _norm_mod_call(xc, g_norm1[0], mods[0], s_len, 0, 1).reshape(b * n, d)
    for l in range(depth):
        mod3 = mods[l]
        pxa = _matmul_cols(h1, w_in, l, 0, ATT_COLS, BF16, 1024).reshape(b, n, ATT_COLS)
        pxb = _matmul_cols(h1, w_in, l, ATT_COLS, REST_COLS, F32, 768).reshape(b, n, REST_COLS)
        qt, kr, vt = _rope_call(pxa, cos, sin)
        mix_a = _na_call(pxa, na_rpb[l], s_len, c_len)
        mix_b = _da_call(qt, kr, vt, da_lambda[l], da_subln_g[l], s_len, c_len, l)
        mix_c = _pool_call(pxb, pool_w[l], pool_scale[l], s_len, c_len)
        mix_d = _sg_call(pxb, sg_norm_g[l], sg_w[l], sg_b[l])
        x1, h2, ri, rf, cnt = _out_router_call((mix_a, mix_b, mix_c, mix_d), w_out_bf, l, xc, g_norm2[l], mod3,
                                               router_w, router_b, s_len)
        next_norm = (g_norm1[l + 1], mods[l + 1]) if l + 1 < depth else None
        xc, h1 = _moe(x1, h2, ri, rf, cnt, mod3, wg, wu, wd, l, s_len, next_norm)
    return _final_norm(xc, g_final, s_len)
```

```python
import functools
import math

import numpy as np
import jax
import jax.numpy as jnp
from jax import lax
from jax.experimental import pallas as pl
from jax.experimental.pallas import tpu as pltpu

GRID_W = 64
NA_HEADS = 4
NA_DIM = 128
NA_WIN_H = 8
NA_WIN_W = 16
DA_HEADS = 4
DA_DIM = 64
DA_VDIM = 2 * DA_DIM
ROPE_BASE = 10000.0
POOL_GROUPS = 4
POOL_WIDTH = 128
POOL_WINDOWS = (2, 4, 8, 16)
SG_GROUPS = 4
SG_WIDTH = 128
SG_CHUNK = 128
N_EXPERT_GROUPS = 4
EXPERTS_PER_GROUP = 4
TOP_K = 2
N_MOD = 6
EPS = 1e-6
NEG_INF = -1e30
LOG2_E = math.log2(math.e)

HEAD_W = 128
MIX_W = 512
ATT_COLS = 6 * MIX_W
REST_COLS = 3 * MIX_W
MOE_BLOCK = 256
NA_Q_ROWS = 4
NA_HEADS_PER_STEP = 4
DA_HEADS_PER_STEP = 2
ROW_TILE = 256
DA_KEY_CHUNKS = (1408, 768, 512, 256, 128)
DA_VT_ROWS = DA_VDIM + 16
VMEM_LIMIT = 56 * 1024 * 1024

F32 = jnp.float32
BF16 = jnp.bfloat16


def _cparams(n_axes, vmem=None):
    return pltpu.CompilerParams(dimension_semantics=("arbitrary",) * n_axes,
                                vmem_limit_bytes=vmem)


def _pick(n, candidates):
    for c in candidates:
        if n % c == 0:
            return c
    raise ValueError(f"no tile in {candidates} divides {n}")


def _ada_kernel(c_ref, w_ref, b_ref, o_ref):
    cs = jax.nn.silu(c_ref[...]).astype(BF16)
    o_ref[...] = jnp.dot(cs, w_ref[...].astype(BF16), preferred_element_type=F32) + b_ref[...]


def _ada_all(cond, w_ada, b_ada):
    depth, d, n6 = w_ada.shape
    tn = _pick(n6, (1024, 512, 256, 128))
    return pl.pallas_call(
        _ada_kernel,
        grid=(depth, n6 // tn),
        in_specs=[pl.BlockSpec((8, d), lambda l, j: (0, 0)),
                  pl.BlockSpec((None, d, tn), lambda l, j: (l, 0, j)),
                  pl.BlockSpec((None, 1, tn), lambda l, j: (l, 0, j))],
        out_specs=pl.BlockSpec((None, 8, tn), lambda l, j: (l, 0, j)),
        out_shape=jax.ShapeDtypeStruct((depth, 8, n6), F32),
        compiler_params=_cparams(2, VMEM_LIMIT),
        name="adaln",
    )(cond, w_ada, b_ada.reshape(depth, 1, n6))


def _row_is_ctx(i, tm, s_len):
    row = i * tm + lax.broadcasted_iota(jnp.int32, (tm, 1), 0)
    return row >= s_len


def _norm_mod(x, g, shx, scx, shc, scc, is_ctx):
    y = x * lax.rsqrt(jnp.mean(x * x, axis=-1, keepdims=True) + EPS) * g
    shift = jnp.where(is_ctx, shc, shx)
    scale = jnp.where(is_ctx, scc, scx)
    return y * (1.0 + scale) + shift


def _norm_mod_kernel(x_ref, g_ref, shx_ref, scx_ref, shc_ref, scc_ref, o_ref, *, s_len, tm):
    is_ctx = _row_is_ctx(pl.program_id(1), tm, s_len)
    h = _norm_mod(x_ref[...], g_ref[...], shx_ref[...], scx_ref[...], shc_ref[...], scc_ref[...], is_ctx)
    o_ref[...] = h.astype(o_ref.dtype)


def _norm_mod_call(xc, g, mod3, s_len, k_shift, k_scale):
    b, n, d = xc.shape
    tm = _pick(n, (ROW_TILE, 128))
    mx = lambda k: pl.BlockSpec((None, 1, d), lambda bi, i: (bi, 0, k))
    mc = lambda k: pl.BlockSpec((None, 1, d), lambda bi, i: (b, 0, k))
    return pl.pallas_call(
        functools.partial(_norm_mod_kernel, s_len=s_len, tm=tm),
        grid=(b, n // tm),
        in_specs=[pl.BlockSpec((None, tm, d), lambda bi, i: (bi, i, 0)),
                  pl.BlockSpec((1, d), lambda bi, i: (0, 0)),
                  mx(k_shift), mx(k_scale), mc(k_shift), mc(k_scale)],
        out_specs=pl.BlockSpec((None, tm, d), lambda bi, i: (bi, i, 0)),
        out_shape=jax.ShapeDtypeStruct((b, n, d), BF16),
        compiler_params=_cparams(2, VMEM_LIMIT),
        name="norm_mod",
    )(xc, g.reshape(1, d), mod3, mod3, mod3, mod3)


def _mm_kernel(a_ref, w_ref, o_ref, wbf_ref):
    @pl.when(pl.program_id(1) == 0)
    def _():
        wbf_ref[...] = w_ref[...].astype(BF16)

    o_ref[...] = jnp.dot(a_ref[...], wbf_ref[...], preferred_element_type=F32).astype(o_ref.dtype)


def _matmul_cols(a, w, layer, col0, ncols, out_dtype, tn):
    m, k = a.shape
    tm = _pick(m, (768, 512, 384, 256, 128))
    off = col0 // tn
    return pl.pallas_call(
        _mm_kernel,
        grid=(ncols // tn, m // tm),
        in_specs=[pl.BlockSpec((tm, k), lambda j, i: (i, 0)),
                  pl.BlockSpec((None, k, tn), lambda j, i: (layer, 0, j + off))],
        out_specs=pl.BlockSpec((tm, tn), lambda j, i: (i, j)),
        out_shape=jax.ShapeDtypeStruct((m, ncols), out_dtype),
        scratch_shapes=[pltpu.VMEM((k, tn), BF16)],
        compiler_params=_cparams(2, VMEM_LIMIT),
        name="in_proj",
    )(a, w)


def _rope_tables(s_len, c_len):
    t = jnp.arange(s_len)
    row = (t // GRID_W).astype(F32)
    col = (t % GRID_W).astype(F32)
    n_freq = DA_DIM // 4
    inv_freq = 1.0 / (ROPE_BASE ** (jnp.arange(n_freq, dtype=F32) / n_freq))
    ang = jnp.concatenate([row[:, None] * inv_freq, col[:, None] * inv_freq], axis=-1)
    cos = jnp.concatenate([jnp.cos(ang), jnp.ones((c_len, DA_DIM // 2), F32)], axis=0)
    sin = jnp.concatenate([jnp.sin(ang), jnp.zeros((c_len, DA_DIM // 2), F32)], axis=0)
    return jnp.tile(cos, (1, 4)), jnp.tile(jnp.concatenate([-sin, sin], axis=-1), (1, 2))


def _rope_kernel(q_ref, k_ref, v_ref, c_ref, s_ref, qt_ref, kr_ref, vt_ref, *, tm, q_scale):
    cos = c_ref[...]
    sin = s_ref[...]
    lane = lax.broadcasted_iota(jnp.int32, (tm, HEAD_W), 1)
    first_half = (lane % DA_DIM) < (DA_DIM // 2)

    def rotate(x):
        swapped = jnp.where(first_half, pltpu.roll(x, HEAD_W - DA_DIM // 2, 1), pltpu.roll(x, DA_DIM // 2, 1))
        return x * cos + swapped * sin

    for h in range(DA_HEADS):
        cols = slice(h * HEAD_W, (h + 1) * HEAD_W)
        q = rotate(q_ref[:, cols].astype(F32)) * q_scale
        qt_ref[cols, :] = q.T.astype(qt_ref.dtype)
        kr_ref[:, cols] = rotate(k_ref[:, cols].astype(F32)).astype(kr_ref.dtype)
        vt_ref[h * DA_VT_ROWS:h * DA_VT_ROWS + DA_VDIM, :] = v_ref[:, cols].astype(F32).T.astype(vt_ref.dtype)
        pad = lax.broadcasted_iota(jnp.int32, (DA_VT_ROWS - DA_VDIM, tm), 0)
        vt_ref[h * DA_VT_ROWS + DA_VDIM:(h + 1) * DA_VT_ROWS, :] = jnp.where(pad == 0, 1.0, 0.0).astype(vt_ref.dtype)


def _rope_call(pxa, cos, sin):
    b, n, _ = pxa.shape
    tm = _pick(n, (ROW_TILE, 128))
    col_spec = lambda k: pl.BlockSpec((None, tm, MIX_W), lambda bi, i: (bi, i, k))
    t_spec = lambda rows: pl.BlockSpec((None, rows, tm), lambda bi, i: (bi, 0, i))
    return pl.pallas_call(
        functools.partial(_rope_kernel, tm=tm, q_scale=DA_DIM ** -0.5 * LOG2_E),
        grid=(b, n // tm),
        in_specs=[col_spec(3), col_spec(4), col_spec(5),
                  pl.BlockSpec((tm, HEAD_W), lambda bi, i: (i, 0)),
                  pl.BlockSpec((tm, HEAD_W), lambda bi, i: (i, 0))],
        out_specs=[t_spec(MIX_W), col_spec(0), t_spec(DA_HEADS * DA_VT_ROWS)],
        out_shape=[jax.ShapeDtypeStruct((b, MIX_W, n), BF16),
                   jax.ShapeDtypeStruct((b, n, MIX_W), BF16),
                   jax.ShapeDtypeStruct((b, DA_HEADS * DA_VT_ROWS, n), BF16)],
        compiler_params=_cparams(2, VMEM_LIMIT),
        name="rope",
    )(pxa, pxa, pxa, cos, sin)


def _na_plan(rows, r_q):
    kh = min(NA_WIN_H, rows)
    key_rows = r_q + kh - 1
    assert rows % r_q == 0 and rows >= key_rows
    patterns, var, ustart = [], [], []
    for j in range(rows // r_q):
        r = r_q * j + np.arange(r_q)
        rs = np.clip(r - kh // 2, 0, rows - kh)
        u = int(np.clip(rs.min(), 0, rows - key_rows))
        kr = u + np.arange(key_rows)
        valid = (kr[None, :] >= rs[:, None]) & (kr[None, :] < rs[:, None] + kh)
        assert (valid.sum(1) == kh).all()
        dr0 = u - r + (NA_WIN_H - 1)
        key = (valid.tobytes(), dr0.tobytes())
        for v, (k2, _, _) in enumerate(patterns):
            if k2 == key:
                break
        else:
            v = len(patterns)
            patterns.append((key, valid, dr0))
        var.append(v)
        ustart.append(u * GRID_W)
    valid = np.stack([p[1] for p in patterns])
    dr0 = np.stack([p[2] for p in patterns])
    return np.asarray(var, np.int32), np.asarray(ustart, np.int32), valid, dr0


def _na_bias_tables(rpb, valid, dr0):
    col = np.arange(GRID_W)
    cs = np.clip(col - NA_WIN_W // 2, 0, GRID_W - NA_WIN_W)
    cmask = (col[None, :] >= cs[:, None]) & (col[None, :] < cs[:, None] + NA_WIN_W)
    dc = np.clip(col[None, :] - col[:, None], -(NA_WIN_W - 1), NA_WIN_W - 1) + (NA_WIN_W - 1)
    heads, n_dr, n_dc = rpb.shape
    v, r_q, key_rows = valid.shape
    pick = (dc.reshape(1, -1) == np.arange(n_dc)[:, None]).astype(np.float32)
    by_col = jnp.dot(rpb.astype(F32).reshape(heads * n_dr, n_dc), pick,
                     precision=lax.Precision.HIGHEST).reshape(heads, n_dr, GRID_W, GRID_W)
    padded = jnp.pad(by_col, ((0, 0), (key_rows, key_rows), (0, 0), (0, 0)))
    tab = jnp.stack([lax.slice_in_dim(padded, int(d) + key_rows, int(d) + 2 * key_rows, axis=1)
                     for d in dr0.reshape(-1)], axis=1)
    tab = tab.reshape(heads, v, r_q, key_rows, GRID_W, GRID_W).transpose(0, 1, 2, 4, 3, 5)
    mask = valid[:, :, None, :, None] & cmask[None, None, :, None, :]
    tab = jnp.where(mask[None], tab, NEG_INF)
    return tab.reshape(heads, v, r_q * GRID_W, key_rows * GRID_W)


def _na_kernel(var_ref, u_ref, q_ref, k_ref, v_ref, kc_ref, vc_ref, bias_ref, o_ref, *, n_lat, kw, scale):
    j = pl.program_id(2)
    nt = (((1,), (1,)), ((), ()))

    def one_head(hh, local):
        cols = slice(hh * HEAD_W, (hh + 1) * HEAD_W)
        q = q_ref[:, cols]
        vc = vc_ref[:, cols]
        s_ctx = lax.dot_general(q, kc_ref[:, cols], nt, preferred_element_type=F32) * scale
        m = jnp.max(s_ctx, axis=-1, keepdims=True)
        if local:
            u = pl.multiple_of(u_ref[j], GRID_W)
            s = lax.dot_general(q, k_ref[pl.ds(u, kw), cols], nt, preferred_element_type=F32) * scale + bias_ref[hh]
            m = jnp.maximum(m, jnp.max(s, axis=-1, keepdims=True))
            p = jnp.exp(s - m)
        pc = jnp.exp(s_ctx - m)
        l = jnp.sum(pc, axis=-1, keepdims=True)
        o = jnp.dot(pc.astype(BF16), vc, preferred_element_type=F32)
        if local:
            l += jnp.sum(p, axis=-1, keepdims=True)
            o += jnp.dot(p.astype(BF16), v_ref[pl.ds(u, kw), cols], preferred_element_type=F32)
        o_ref[:, cols] = (o / l).astype(o_ref.dtype)

    @pl.when(j < n_lat)
    def _():
        for hh in range(NA_HEADS_PER_STEP):
            one_head(hh, True)

    @pl.when(j >= n_lat)
    def _():
        for hh in range(NA_HEADS_PER_STEP):
            one_head(hh, False)


def _na_call(pxa, rpb, s_len, c_len):
    b, n, _ = pxa.shape
    rows = s_len // GRID_W
    tq = NA_Q_ROWS * GRID_W
    hps = NA_HEADS_PER_STEP
    assert s_len % tq == 0 and c_len % tq == 0 and s_len % c_len == 0 and NA_HEADS % hps == 0
    var, ustart, valid, dr0 = _na_plan(rows, NA_Q_ROWS)
    n_lat, n_q = s_len // tq, n // tq
    kw = valid.shape[2] * GRID_W
    tabs = _na_bias_tables(rpb, valid, dr0)
    pad = np.zeros(n_q - n_lat, np.int32)
    var = jnp.asarray(np.concatenate([var, pad]))
    ustart = jnp.asarray(np.concatenate([ustart, pad]))
    ctx_blk = s_len // c_len
    wide = hps * HEAD_W
    k_blk, v_blk = NA_HEADS // hps, 2 * NA_HEADS // hps
    grid_spec = pltpu.PrefetchScalarGridSpec(
        num_scalar_prefetch=2,
        grid=(b, NA_HEADS // hps, n_q),
        in_specs=[pl.BlockSpec((None, tq, wide), lambda bi, h, j, vr, ur: (bi, j, h)),
                  pl.BlockSpec((None, s_len, wide), lambda bi, h, j, vr, ur: (bi, 0, k_blk + h)),
                  pl.BlockSpec((None, s_len, wide), lambda bi, h, j, vr, ur: (bi, 0, v_blk + h)),
                  pl.BlockSpec((None, c_len, wide), lambda bi, h, j, vr, ur: (bi, ctx_blk, k_blk + h)),
                  pl.BlockSpec((None, c_len, wide), lambda bi, h, j, vr, ur: (bi, ctx_blk, v_blk + h)),
                  pl.BlockSpec((hps, None, tq, kw), lambda bi, h, j, vr, ur: (h, vr[j], 0, 0))],
        out_specs=pl.BlockSpec((None, tq, wide), lambda bi, h, j, vr, ur: (bi, j, h)),
    )
    return pl.pallas_call(
        functools.partial(_na_kernel, n_lat=n_lat, kw=kw, scale=NA_DIM ** -0.5),
        grid_spec=grid_spec,
        out_shape=jax.ShapeDtypeStruct((b, n, MIX_W), BF16),
        compiler_params=_cparams(3, VMEM_LIMIT),
        name="nbr_attn",
    )(var, ustart, pxa, pxa, pxa, pxa, pxa, tabs)


def _cast_expert_weights(step, n_steps, layer, srcs, dsts, in_g, in_u, in_d, out_g, out_u, out_d, sem_in, sem_out):
    ins, outs = (in_g, in_u, in_d), (out_g, out_u, out_d)
    n_chunks = srcs[0].shape[1] // in_g.shape[1]
    cur = jnp.minimum(step, n_chunks - 1)
    nxt = jnp.minimum(step + 1, n_chunks - 1)
    slot = step % 2
    other = 1 - slot

    def in_copy(t, chunk, sl):
        rows = ins[t].shape[1]
        src = srcs[t].at[layer, pl.ds(pl.multiple_of(chunk * rows, rows), rows)]
        return pltpu.make_async_copy(src, ins[t].at[sl], sem_in.at[sl])

    def out_copy(t, chunk, sl):
        rows = outs[t].shape[1]
        dst = dsts[t].at[pl.ds(pl.multiple_of(chunk * rows, rows), rows)]
        return pltpu.make_async_copy(outs[t].at[sl], dst, sem_out.at[sl])

    @pl.when(step == 0)
    def _():
        for t in range(3):
            in_copy(t, cur, slot).start()

    for t in range(3):
        in_copy(t, nxt, other).start()
    for t in range(3):
        in_copy(t, cur, slot).wait()

    @pl.when(step > 0)
    def _():
        for t in range(3):
            out_copy(t, cur, other).wait()

    for t in range(3):
        outs[t][slot] = ins[t][slot].astype(BF16)
    for t in range(3):
        out_copy(t, cur, slot).start()

    @pl.when(step == n_steps - 1)
    def _():
        for t in range(3):
            out_copy(t, cur, slot).wait()
            in_copy(t, nxt, other).wait()


def _da_kernel(qt_ref, k_ref, vt_ref, lam_ref, g_ref, wg_ref, wu_ref, wd_ref, o_ref, wgb_ref, wub_ref, wdb_ref,
               *scratch, s_len, c_len, tq, tk, lam_init, heads, layer):
    score_refs, cast_refs = scratch[:2 * heads], scratch[2 * heads:]
    sa_refs, sb_refs = score_refs[0::2], score_refs[1::2]
    qi = pl.program_id(2)
    step = (pl.program_id(0) * pl.num_programs(1) + pl.program_id(1)) * pl.num_programs(2) + qi
    n_steps = pl.num_programs(0) * pl.num_programs(1) * pl.num_programs(2)
    _cast_expert_weights(step, n_steps, layer, (wg_ref, wu_ref, wd_ref), (wgb_ref, wub_ref, wdb_ref), *cast_refs)
    n = s_len + c_len
    feat = lax.broadcasted_iota(jnp.int32, (2 * DA_DIM, tq), 0)

    def q_tiles(idx):
        start = pl.multiple_of(idx * tq, tq)
        out = []
        for hh in range(heads):
            qt = qt_ref[hh * HEAD_W:(hh + 1) * HEAD_W, pl.ds(start, tq)]
            zero = jnp.zeros_like(qt)
            out.append(jnp.concatenate([jnp.where(feat < DA_DIM, qt, zero), jnp.where(feat >= DA_DIM, qt, zero)],
                                       axis=1))
        return out

    q2 = q_tiles(qi)

    def k_rows(hh, rows):
        return k_ref[rows, hh * HEAD_W:(hh + 1) * HEAD_W]

    def vt_cols(hh, cols):
        return vt_ref[hh * DA_VT_ROWS:(hh + 1) * DA_VT_ROWS, cols]

    def scores(k_blk, q):
        return jnp.dot(k_blk, q, preferred_element_type=F32)

    def update(s, vt_blk, carry):
        m, acc = carry
        m_new = jnp.maximum(m, jnp.max(s, axis=0, keepdims=True))
        alpha = jnp.exp2(m - m_new)
        p = jnp.exp2(s - m_new)
        acc_new = alpha * acc + jnp.dot(vt_blk, p.astype(BF16), preferred_element_type=F32)
        return m_new, acc_new

    def init():
        one = (jnp.full((1, 2 * tq), NEG_INF, F32), jnp.zeros((DA_VT_ROWS, 2 * tq), F32))
        return tuple(one for _ in range(heads))

    def finalize(stats):
        lv = lam_ref[...]
        t1 = jnp.sum(lv[0:1, :] * lv[1:2, :], axis=-1, keepdims=True)
        t2 = jnp.sum(lv[2:3, :] * lv[3:4, :], axis=-1, keepdims=True)
        lam = jnp.exp(t1) - jnp.exp(t2) + lam_init
        for hh in range(heads):
            _, acc = stats[hh]
            o = acc[:DA_VDIM, :] / acc[DA_VDIM:DA_VDIM + 1, :]
            o = o[:, :tq] - lam * o[:, tq:]
            y = o * lax.rsqrt(jnp.mean(o * o, axis=0, keepdims=True) + EPS) * g_ref[...]
            o_ref[:, hh * HEAD_W:(hh + 1) * HEAD_W] = (y * (1.0 - lam_init)).T.astype(o_ref.dtype)

    k_steps = n // tk
    full_pairs = (k_steps - 1) // 2
    carry_first = k_steps % 2 == 0

    @pl.when(qi < s_len // tq)
    def _():
        def scores_into(dst_refs, step, qs):
            rows = pl.ds(pl.multiple_of(step * tk, tk), tk)
            for hh in range(heads):
                dst_refs[hh][...] = scores(k_rows(hh, rows), qs[hh])

        def update_from(src_refs, step, stats):
            cols = pl.ds(pl.multiple_of(step * tk, tk), tk)
            return tuple(update(src_refs[hh][...], vt_cols(hh, cols), stats[hh]) for hh in range(heads))

        def pair(j, stats):
            c0 = 2 * j
            scores_into(sb_refs, c0 + 1, q2)
            stats = update_from(sa_refs, c0, stats)
            scores_into(sa_refs, c0 + 2, q2)
            return update_from(sb_refs, c0 + 1, stats)

        if carry_first:
            @pl.when(qi == 0)
            def _():
                scores_into(sa_refs, 0, q2)
        else:
            scores_into(sa_refs, 0, q2)
        stats = lax.fori_loop(0, full_pairs, pair, init())
        c0 = 2 * full_pairs
        if carry_first:
            scores_into(sb_refs, c0 + 1, q2)
            stats = update_from(sa_refs, c0, stats)
            scores_into(sa_refs, 0, q_tiles(qi + 1))
            stats = update_from(sb_refs, c0 + 1, stats)
        else:
            stats = update_from(sa_refs, c0, stats)
        finalize(stats)

    @pl.when(qi >= s_len // tq)
    def _():
        rows, stats = slice(s_len, n), init()
        finalize(tuple(update(scores(k_rows(hh, rows), q2[hh]), vt_cols(hh, rows), stats[hh])
                       for hh in range(heads)))


def _da_call(qt, kr, vt, lam_vecs, subln_g, exp_w, s_len, c_len, layer):
    b, n, _ = kr.shape
    tq = _pick(math.gcd(s_len, c_len), (256, 128))
    tk = _pick(n, DA_KEY_CHUNKS)
    hps = DA_HEADS_PER_STEP
    assert DA_HEADS % hps == 0
    lam_init = 0.8 - 0.6 * math.exp(-0.3 * layer)
    grid = (b, DA_HEADS // hps, n // tq)
    n_chunks = 1 << (math.prod(grid).bit_length() - 1)
    flat = [w.reshape(w.shape[0], w.shape[1] * w.shape[2], w.shape[3]) for w in exp_w]
    chunk = [(w.shape[1] // n_chunks, w.shape[2]) for w in flat]
    assert all(w.shape[1] % n_chunks == 0 and r % 16 == 0 for w, (r, _) in zip(flat, chunk))
    once = pl.Buffered(1)
    any_spec = pl.BlockSpec(memory_space=pl.ANY)
    outs = pl.pallas_call(
        functools.partial(_da_kernel, s_len=s_len, c_len=c_len, tq=tq, tk=tk, lam_init=lam_init, heads=hps,
                          layer=layer),
        grid=grid,
        in_specs=[pl.BlockSpec((None, hps * HEAD_W, n), lambda bi, h, i: (bi, h, 0), pipeline_mode=once),
                  pl.BlockSpec((None, n, hps * HEAD_W), lambda bi, h, i: (bi, 0, h), pipeline_mode=once),
                  pl.BlockSpec((None, hps * DA_VT_ROWS, n), lambda bi, h, i: (bi, h, 0), pipeline_mode=once),
                  pl.BlockSpec((4, DA_DIM), lambda bi, h, i: (0, 0)),
                  pl.BlockSpec((DA_VDIM, 1), lambda bi, h, i: (0, 0)),
                  any_spec, any_spec, any_spec],
        out_specs=[pl.BlockSpec((None, tq, hps * HEAD_W), lambda bi, h, i: (bi, i, h)),
                   any_spec, any_spec, any_spec],
        out_shape=[jax.ShapeDtypeStruct((b, n, MIX_W), BF16)]
                  + [jax.ShapeDtypeStruct(w.shape[1:], BF16) for w in flat],
        scratch_shapes=[pltpu.VMEM((tk, 2 * tq), F32)] * (2 * hps)
                       + [pltpu.VMEM((2,) + c, F32) for c in chunk]
                       + [pltpu.VMEM((2,) + c, BF16) for c in chunk]
                       + [pltpu.SemaphoreType.DMA((2,)), pltpu.SemaphoreType.DMA((2,))],
        compiler_params=_cparams(3, VMEM_LIMIT),
        name="diff_attn",
    )(qt, kr, vt, lam_vecs, subln_g.reshape(DA_VDIM, 1), *flat)
    return outs[0], tuple(o.reshape(w.shape[1:]) for o, w in zip(outs[1:], exp_w))


POOL_HALO = 8


def _pool_kernel(prev_ref, cur_ref, next_ref, w_ref, sc_ref, o_ref, buf_ref, *, s_len, c_len, tm):
    i = pl.program_id(1)
    n_lat = s_len // tm
    n_all = (s_len + c_len) // tm
    seq_start = (i == 0) | (i == n_lat)
    seq_end = (i == n_lat - 1) | (i == n_all - 1)
    buf_ref[0:POOL_HALO, :] = jnp.where(seq_start, 0.0, prev_ref[...])
    buf_ref[POOL_HALO:POOL_HALO + tm, :] = cur_ref[...]
    buf_ref[POOL_HALO + tm:, :] = jnp.where(seq_end, 0.0, next_ref[...])
    in_ctx = i >= n_lat
    seq_len = jnp.where(in_ctx, c_len, s_len)
    pos = i * tm - jnp.where(in_ctx, s_len, 0) + lax.broadcasted_iota(jnp.int32, (tm, 1), 0)
    for g, win in enumerate(POOL_WINDOWS):
        half = win // 2
        cols = slice(g * POOL_WIDTH, (g + 1) * POOL_WIDTH)
        tot = buf_ref[POOL_HALO - half:POOL_HALO - half + tm, cols]
        for d in range(-half + 1, half):
            tot = tot + buf_ref[POOL_HALO + d:POOL_HALO + d + tm, cols]
        cnt = jnp.clip(pos + half, 0, seq_len) - jnp.clip(pos - half, 0, seq_len)
        resid = tot / cnt.astype(F32) - cur_ref[:, cols]
        y = jnp.dot(resid.astype(BF16), w_ref[g].astype(BF16), preferred_element_type=F32)
        o_ref[:, cols] = (y * sc_ref[:, cols]).astype(o_ref.dtype)


def _pool_call(pxb, pool_w, pool_scale, s_len, c_len):
    b, n, _ = pxb.shape
    tm = _pick(math.gcd(s_len, c_len), (ROW_TILE, 128))
    hb = tm // POOL_HALO
    last = n // POOL_HALO - 1
    return pl.pallas_call(
        functools.partial(_pool_kernel, s_len=s_len, c_len=c_len, tm=tm),
        grid=(b, n // tm),
        in_specs=[pl.BlockSpec((None, POOL_HALO, MIX_W), lambda bi, i: (bi, jnp.maximum(i * hb - 1, 0), 0)),
                  pl.BlockSpec((None, tm, MIX_W), lambda bi, i: (bi, i, 0)),
                  pl.BlockSpec((None, POOL_HALO, MIX_W), lambda bi, i: (bi, jnp.minimum((i + 1) * hb, last), 0)),
                  pl.BlockSpec((POOL_GROUPS, POOL_WIDTH, POOL_WIDTH), lambda bi, i: (0, 0, 0)),
                  pl.BlockSpec((1, MIX_W), lambda bi, i: (0, 0))],
        out_specs=pl.BlockSpec((None, tm, MIX_W), lambda bi, i: (bi, i, 0)),
        out_shape=jax.ShapeDtypeStruct((b, n, MIX_W), BF16),
        scratch_shapes=[pltpu.VMEM((tm + 2 * POOL_HALO, MIX_W), F32)],
        compiler_params=_cparams(2, VMEM_LIMIT),
        name="pool",
    )(pxb, pxb, pxb, pool_w, pool_scale.reshape(1, MIX_W))


def _sg_kernel(u_ref, v_ref, g_ref, w_ref, b_ref, o_ref, *, tm):
    v = jax.nn.gelu(v_ref[...])
    vn = (v * lax.rsqrt(jnp.mean(v * v, axis=-1, keepdims=True) + EPS) * g_ref[...]).astype(BF16)
    bias = b_ref[...]
    for c in range(tm // SG_CHUNK):
        rows = slice(c * SG_CHUNK, (c + 1) * SG_CHUNK)
        for g in range(SG_GROUPS):
            cols = slice(g * SG_WIDTH, (g + 1) * SG_WIDTH)
            mixed = jnp.dot(w_ref[g].astype(BF16), vn[rows, cols], preferred_element_type=F32)
            mixed = mixed + bias[:, g:g + 1]
            o_ref[rows, cols] = (jax.nn.gelu(u_ref[rows, cols]) * mixed).astype(o_ref.dtype)


def _sg_call(pxb, sg_norm_g, sg_w, sg_b):
    b, n, _ = pxb.shape
    tm = _pick(n, (ROW_TILE, 128))
    return pl.pallas_call(
        functools.partial(_sg_kernel, tm=tm),
        grid=(b, n // tm),
        in_specs=[pl.BlockSpec((None, tm, MIX_W), lambda bi, i: (bi, i, 1)),
                  pl.BlockSpec((None, tm, MIX_W), lambda bi, i: (bi, i, 2)),
                  pl.BlockSpec((1, MIX_W), lambda bi, i: (0, 0)),
                  pl.BlockSpec((SG_GROUPS, SG_CHUNK, SG_CHUNK), lambda bi, i: (0, 0, 0)),
                  pl.BlockSpec((SG_CHUNK, SG_GROUPS), lambda bi, i: (0, 0))],
        out_specs=pl.BlockSpec((None, tm, MIX_W), lambda bi, i: (bi, i, 0)),
        out_shape=jax.ShapeDtypeStruct((b, n, MIX_W), BF16),
        compiler_params=_cparams(2, VMEM_LIMIT),
        name="spatial_gate",
    )(pxb, pxb, sg_norm_g.reshape(1, MIX_W), sg_w, sg_b.T)


def _pack_bf16_pairs(x):
    w = x.shape[1] // 2
    hi = lax.bitcast_convert_type(x[:, :w].astype(BF16).astype(F32), jnp.uint32)
    lo = lax.bitcast_convert_type(x[:, w:].astype(BF16).astype(F32), jnp.uint32)
    return hi | (lo >> 16)


def _unpack_bf16_pairs(u):
    hi = lax.bitcast_convert_type(u & jnp.uint32(0xFFFF0000), F32).astype(BF16)
    lo = lax.bitcast_convert_type(u << 16, F32).astype(BF16)
    return jnp.concatenate([hi, lo], axis=1)


def _first_argmax(vals):
    best, idx = vals[0], jnp.zeros(vals[0].shape, jnp.int32)
    for k in range(1, len(vals)):
        take = vals[k] > best
        best = jnp.where(take, vals[k], best)
        idx = jnp.where(take, k, idx)
    return idx, best


def _out_router_kernel(ma_ref, mb_ref, mc_ref, md_ref, w_ref, x_ref, gx_ref, gc_ref,
                       g_ref, shx_ref, scx_ref, shc_ref, scc_ref, rw_ref, rb_ref,
                       x1_ref, h_ref, ri_ref, rf_ref, cnt_ref, run_ref, *, s_len, tm, n_exp):
    bi, i = pl.program_id(0), pl.program_id(1)

    @pl.when((bi == 0) & (i == 0))
    def _():
        run_ref[...] = jnp.zeros_like(run_ref)

    mix = jnp.concatenate([ma_ref[...], mb_ref[...], mc_ref[...], md_ref[...]], axis=1)
    acc = jnp.dot(mix, w_ref[...], preferred_element_type=F32)
    is_ctx = _row_is_ctx(i, tm, s_len)
    x1 = x_ref[...] + jnp.where(is_ctx, gc_ref[...], gx_ref[...]) * acc
    x1_ref[...] = x1
    h = _norm_mod(x1, g_ref[...], shx_ref[...], scx_ref[...], shc_ref[...], scc_ref[...], is_ctx)
    h_ref[...] = _pack_bf16_pairs(h)

    h_hi = h.astype(BF16)
    h_lo = (h - h_hi.astype(F32)).astype(BF16)
    part = jnp.dot(h_hi, rw_ref[0], preferred_element_type=F32)
    part += jnp.dot(h_lo, rw_ref[1], preferred_element_type=F32)
    part_t = part.T
    logits = part_t[0:n_exp, :] + part_t[n_exp:2 * n_exp, :]
    scores = jax.nn.sigmoid(logits)
    biased = scores + rb_ref[...]
    b_rows = [biased[e:e + 1, :] for e in range(n_exp)]
    s_rows = [scores[e:e + 1, :] for e in range(n_exp)]
    epg = EXPERTS_PER_GROUP
    group_scores = []
    for g in range(N_EXPERT_GROUPS):
        r = b_rows[g * epg:(g + 1) * epg]
        pair = None
        for a in range(epg):
            for c in range(a + 1, epg):
                t = r[a] + r[c]
                pair = t if pair is None else jnp.maximum(pair, t)
        group_scores.append(pair)
    group, _ = _first_argmax(group_scores)
    in_group = []
    for k in range(epg):
        sel = b_rows[k]
        for g in range(1, N_EXPERT_GROUPS):
            sel = jnp.where(group == g, b_rows[g * epg + k], sel)
        in_group.append(sel)
    loc0, _ = _first_argmax(in_group)
    rest = [jnp.where(loc0 == k, -jnp.inf, in_group[k]) for k in range(epg)]
    loc1, _ = _first_argmax(rest)
    e0 = group * epg + loc0
    e1 = group * epg + loc1
    g0 = jnp.zeros_like(s_rows[0])
    g1 = jnp.zeros_like(s_rows[0])
    for e in range(n_exp):
        g0 = jnp.where(e0 == e, s_rows[e], g0)
        g1 = jnp.where(e1 == e, s_rows[e], g1)
    tot = g0 + g1

    eidx = lax.broadcasted_iota(jnp.int32, (n_exp, tm), 0)
    hit0 = eidx == e0
    hit1 = eidx == e1
    onehot = jnp.where(hit0 | hit1, 1.0, 0.0)
    before = lax.broadcasted_iota(jnp.int32, (tm, tm), 0) < lax.broadcasted_iota(jnp.int32, (tm, tm), 1)
    tri = jnp.where(before, 1.0, 0.0).astype(BF16)
    run = run_ref[:, 0:1]
    rank = jnp.dot(onehot.astype(BF16), tri, preferred_element_type=F32) + run
    rank0 = jnp.sum(jnp.where(hit0, rank, 0.0), axis=0, keepdims=True)
    rank1 = jnp.sum(jnp.where(hit1, rank, 0.0), axis=0, keepdims=True)
    new_run = run + jnp.sum(onehot, axis=1, keepdims=True)
    run_ref[...] = jnp.broadcast_to(new_run, run_ref.shape)
    cnt_ref[...] = jnp.broadcast_to(new_run, cnt_ref.shape).astype(jnp.int32)

    zi = jnp.zeros((4, tm), jnp.int32)
    ri_ref[...] = jnp.concatenate([e0, e1, rank0.astype(jnp.int32), rank1.astype(jnp.int32), zi], axis=0)
    zf = jnp.zeros((6, tm), F32)
    rf_ref[...] = jnp.concatenate([g0 / tot, g1 / tot, zf], axis=0)


def _router_slabs(router_w):
    d, n_exp = router_w.shape
    w_hi = router_w.astype(BF16)
    w_lo = (router_w - w_hi.astype(F32)).astype(BF16)
    zeros = lambda cols: jnp.zeros((d, cols), BF16)
    return jnp.stack([jnp.concatenate([w_hi, w_lo, zeros(HEAD_W - 2 * n_exp)], axis=1),
                      jnp.concatenate([w_hi, zeros(HEAD_W - n_exp)], axis=1)])


def _out_router_call(mixes, w_out_bf, layer, xc, g2, mod3, router_w, router_b, s_len):
    b, n, d = xc.shape
    n_exp = router_w.shape[1]
    tm = _pick(n, (ROW_TILE, 128))
    per_b = n // tm
    mx = lambda k: pl.BlockSpec((None, 1, d), lambda bi, i: (bi, 0, k))
    mc = lambda k: pl.BlockSpec((None, 1, d), lambda bi, i: (b, 0, k))
    mix_spec = pl.BlockSpec((None, tm, MIX_W), lambda bi, i: (bi, i, 0))
    row_spec = pl.BlockSpec((None, tm, d), lambda bi, i: (bi, i, 0))
    tok_spec = pl.BlockSpec((8, tm), lambda bi, i: (0, bi * per_b + i))
    return pl.pallas_call(
        functools.partial(_out_router_kernel, s_len=s_len, tm=tm, n_exp=n_exp),
        grid=(b, per_b),
        in_specs=[mix_spec, mix_spec, mix_spec, mix_spec,
                  pl.BlockSpec((None, 4 * MIX_W, d), lambda bi, i: (layer, 0, 0)),
                  row_spec, mx(2), mc(2),
                  pl.BlockSpec((1, d), lambda bi, i: (0, 0)),
                  mx(3), mx(4), mc(3), mc(4),
                  pl.BlockSpec((2, d, HEAD_W), lambda bi, i: (0, 0, 0)),
                  pl.BlockSpec((n_exp, 1), lambda bi, i: (0, 0))],
        out_specs=[row_spec, pl.BlockSpec((None, tm, d // 2), lambda bi, i: (bi, i, 0)), tok_spec, tok_spec,
                   pl.BlockSpec((n_exp, HEAD_W), lambda bi, i: (0, 0))],
        out_shape=[jax.ShapeDtypeStruct((b, n, d), F32),
                   jax.ShapeDtypeStruct((b, n, d // 2), jnp.uint32),
                   jax.ShapeDtypeStruct((8, b * n), jnp.int32),
                   jax.ShapeDtypeStruct((8, b * n), F32),
                   jax.ShapeDtypeStruct((n_exp, HEAD_W), jnp.int32)],
        scratch_shapes=[pltpu.VMEM((n_exp, HEAD_W), F32)],
        compiler_params=_cparams(2, VMEM_LIMIT),
        name="out_router",
    )(*mixes, w_out_bf, xc, mod3, mod3, g2.reshape(1, d), mod3, mod3, mod3, mod3,
      _router_slabs(router_w), router_b.reshape(n_exp, 1))


def _row_copy(src_ref, src_row, dst_ref, dst_row, sem):
    return pltpu.make_async_copy(src_ref.at[pl.ds(src_row, 1)], dst_ref.at[pl.ds(dst_row, 1)], sem)


def _dispatch_kernel(p0_ref, p1_ref, h_ref, xs_in_ref, xs_ref, sem, *, tm):
    del xs_in_ref
    base = pl.program_id(0) * tm

    for r in range(tm):
        _row_copy(h_ref, r, xs_ref, p0_ref[base + r], sem).start()
        _row_copy(h_ref, r, xs_ref, p1_ref[base + r], sem).start()

    tile_copy = pltpu.make_async_copy(h_ref, xs_ref.at[pl.ds(0, tm)], sem)
    tile_copy.wait()
    tile_copy.wait()


def _dispatch_call(h2, pos0, pos1, n_pad):
    t, d = h2.shape
    tm = _pick(t, (ROW_TILE, 128))
    grid_spec = pltpu.PrefetchScalarGridSpec(
        num_scalar_prefetch=2,
        grid=(t // tm,),
        in_specs=[pl.BlockSpec((tm, d), lambda i, p0, p1: (i, 0)),
                  pl.BlockSpec(memory_space=pl.ANY)],
        out_specs=pl.BlockSpec(memory_space=pl.ANY),
        scratch_shapes=[pltpu.SemaphoreType.DMA],
    )
    return pl.pallas_call(
        functools.partial(_dispatch_kernel, tm=tm),
        grid_spec=grid_spec,
        out_shape=jax.ShapeDtypeStruct((n_pad, d), h2.dtype),
        input_output_aliases={3: 0},
        compiler_params=_cparams(1, VMEM_LIMIT),
        name="dispatch",
    )(pos0, pos1, h2, jnp.zeros((n_pad, d), h2.dtype))


def _ffn_kernel(be_ref, nb_ref, x_ref, wg_ref, wu_ref, wd_ref, y_ref):
    blk = pl.program_id(0)

    @pl.when(blk < nb_ref[0])
    def _():
        x = _unpack_bf16_pairs(x_ref[...])
        gate = jnp.dot(x, wg_ref[...], preferred_element_type=F32)
        up = jnp.dot(x, wu_ref[...], preferred_element_type=F32)
        hid = (jax.nn.silu(gate) * up).astype(BF16)
        y_ref[...] = jnp.dot(hid, wd_ref[...], preferred_element_type=F32)

    @pl.when(blk >= nb_ref[0])
    def _():
        y_ref[...] = jnp.zeros_like(y_ref)


def _ffn_call(xs, blk_expert, n_used, wg, wu, wd):
    n_pad, half = xs.shape
    d = 2 * half
    d_exp = wg.shape[2]
    grid_spec = pltpu.PrefetchScalarGridSpec(
        num_scalar_prefetch=2,
        grid=(n_pad // MOE_BLOCK,),
        in_specs=[pl.BlockSpec((MOE_BLOCK, half), lambda i, be, nb: (i, 0)),
                  pl.BlockSpec((None, d, d_exp), lambda i, be, nb: (be[i], 0, 0)),
                  pl.BlockSpec((None, d, d_exp), lambda i, be, nb: (be[i], 0, 0)),
                  pl.BlockSpec((None, d_exp, d), lambda i, be, nb: (be[i], 0, 0))],
        out_specs=pl.BlockSpec((MOE_BLOCK, d), lambda i, be, nb: (i, 0)),
    )
    return pl.pallas_call(
        _ffn_kernel,
        grid_spec=grid_spec,
        out_shape=jax.ShapeDtypeStruct((n_pad, d), F32),
        compiler_params=_cparams(1, VMEM_LIMIT),
        name="expert_ffn",
    )(blk_expert, n_used, xs, wg, wu, wd)


def _combine_kernel(p0_ref, p1_ref, ys_ref, x_ref, gt_ref, gx_ref, gc_ref, *rest, s_len, n_len, tm, with_next):
    if with_next:
        g_ref, shx_ref, scx_ref, shc_ref, scc_ref, o_ref, h_ref, y0_ref, y1_ref, sem = rest
    else:
        o_ref, y0_ref, y1_ref, sem = rest
    i = pl.program_id(0)
    last = pl.num_programs(0) - 1

    def start_tile(tile, slot):
        base = tile * tm
        for r in range(tm):
            _row_copy(ys_ref, p0_ref[base + r], y0_ref.at[slot], r, sem.at[slot]).start()
            _row_copy(ys_ref, p1_ref[base + r], y1_ref.at[slot], r, sem.at[slot]).start()

    def wait_tile(slot):
        pltpu.make_async_copy(ys_ref.at[pl.ds(0, tm)], y0_ref.at[slot], sem.at[slot]).wait()
        pltpu.make_async_copy(ys_ref.at[pl.ds(0, tm)], y1_ref.at[slot], sem.at[slot]).wait()

    def combine(slot):
        row = ((i * tm) % n_len) + lax.broadcasted_iota(jnp.int32, (tm, 1), 0)
        is_ctx = row >= s_len
        gate = jnp.where(is_ctx, gc_ref[...], gx_ref[...])
        gt = gt_ref[...]
        y = gt[:, 0:1] * y0_ref[slot] + gt[:, 1:2] * y1_ref[slot]
        x_new = x_ref[...] + gate * y
        o_ref[...] = x_new
        if with_next:
            h = _norm_mod(x_new, g_ref[...], shx_ref[...], scx_ref[...], shc_ref[...], scc_ref[...], is_ctx)
            h_ref[...] = h.astype(h_ref.dtype)

    @pl.when(i == 0)
    def _():
        start_tile(0, 0)

    for slot in (0, 1):
        @pl.when(i % 2 == slot)
        def _(slot=slot):
            wait_tile(slot)
            start_tile(jnp.minimum(i + 1, last), 1 - slot)
            combine(slot)

            @pl.when(i == last)
            def _():
                wait_tile(1 - slot)


def _combine_call(ys, pos0, pos1, x1, gates, mod3, s_len, k_gate, next_norm=None):
    b, n, d = x1.shape
    t = b * n
    tm = _pick(n, (ROW_TILE, 128))
    row_spec = pl.BlockSpec((tm, d), lambda i, p0, p1: (i, 0))
    mx = lambda k: pl.BlockSpec((None, 1, d), lambda i, p0, p1: ((i * tm) // n, 0, k))
    mc = lambda k: pl.BlockSpec((None, 1, d), lambda i, p0, p1: (b, 0, k))
    in_specs = [pl.BlockSpec(memory_space=pl.ANY), row_spec,
                pl.BlockSpec((tm, TOP_K), lambda i, p0, p1: (i, 0)), mx(k_gate), mc(k_gate)]
    args = [pos0, pos1, ys, x1.reshape(t, d), gates, mod3, mod3]
    out_specs, out_shape = [row_spec], [jax.ShapeDtypeStruct((t, d), F32)]
    if next_norm is not None:
        g_next, mod_next = next_norm
        in_specs += [pl.BlockSpec((1, d), lambda i, p0, p1: (0, 0)), mx(0), mx(1), mc(0), mc(1)]
        args += [g_next.reshape(1, d), mod_next, mod_next, mod_next, mod_next]
        out_specs.append(row_spec)
        out_shape.append(jax.ShapeDtypeStruct((t, d), BF16))
    grid_spec = pltpu.PrefetchScalarGridSpec(
        num_scalar_prefetch=2,
        grid=(t // tm,),
        in_specs=in_specs,
        out_specs=out_specs,
        scratch_shapes=[pltpu.VMEM((2, tm, d), F32), pltpu.VMEM((2, tm, d), F32), pltpu.SemaphoreType.DMA((2,))],
    )
    outs = pl.pallas_call(
        functools.partial(_combine_kernel, s_len=s_len, n_len=n, tm=tm, with_next=next_norm is not None),
        grid_spec=grid_spec,
        out_shape=out_shape,
        compiler_params=_cparams(1, VMEM_LIMIT),
        name="combine",
    )(*args)
    return outs[0].reshape(b, n, d), (outs[1] if next_norm is not None else None)


def _moe(x1, h2, ri, rf, cnt, mod3, wg, wu, wd, s_len, next_norm):
    b, n, d = x1.shape
    t = b * n
    n_exp = cnt.shape[0]
    counts = cnt[:, 0]
    n_blk = (counts + MOE_BLOCK - 1) // MOE_BLOCK
    blk_end = jnp.cumsum(n_blk)
    starts = (blk_end - n_blk) * MOE_BLOCK
    pos0 = (starts[ri[0]] + ri[2]).astype(jnp.int32)
    pos1 = (starts[ri[1]] + ri[3]).astype(jnp.int32)
    n_pad = t * TOP_K + n_exp * MOE_BLOCK
    n_used = blk_end[-1:].astype(jnp.int32)
    blk_ids = jnp.minimum(jnp.arange(n_pad // MOE_BLOCK, dtype=jnp.int32), n_used[0] - 1)
    blk_expert = jnp.sum(blk_ids[:, None] >= blk_end[None, :], axis=1).astype(jnp.int32)
    xs = _dispatch_call(h2.reshape(t, d // 2), pos0, pos1, n_pad)
    ys = _ffn_call(xs, blk_expert, n_used, wg, wu, wd)
    return _combine_call(ys, pos0, pos1, x1, rf[:TOP_K].T, mod3, s_len, 5, next_norm)


def _final_norm_kernel(x_ref, g_ref, o_ref):
    x = x_ref[...]
    o_ref[...] = x * lax.rsqrt(jnp.mean(x * x, axis=-1, keepdims=True) + EPS) * g_ref[...]


def _final_norm(xc, g, s_len):
    b, _, d = xc.shape
    tm = _pick(s_len, (ROW_TILE, 128))
    return pl.pallas_call(
        _final_norm_kernel,
        grid=(b, s_len // tm),
        in_specs=[pl.BlockSpec((None, tm, d), lambda bi, i: (bi, i, 0)),
                  pl.BlockSpec((1, d), lambda bi, i: (0, 0))],
        out_specs=pl.BlockSpec((None, tm, d), lambda bi, i: (bi, i, 0)),
        out_shape=jax.ShapeDtypeStruct((b, s_len, d), F32),
        compiler_params=_cparams(2, VMEM_LIMIT),
        name="final_norm",
    )(xc, g.reshape(1, d))


def kernel(x, c, ctx, c_ctx, w_ada, b_ada, g_norm1, g_norm2, w_in, w_out, na_rpb, da_lambda, da_subln_g,
           pool_w, pool_scale, sg_norm_g, sg_w, sg_b, router_w, router_b, exp_w_gate, exp_w_up, exp_w_down,
           g_final):
    b, s_len, d = x.shape
    c_len = ctx.shape[1]
    n = s_len + c_len
    depth = w_ada.shape[0]
    assert b + 1 <= 8 and w_in.shape[2] == ATT_COLS + REST_COLS and w_out.shape[1] == 4 * MIX_W

    cond = jnp.zeros((8, d), F32).at[:b].set(c).at[b].set(c_ctx)
    mod_all = _ada_all(cond, w_ada, b_ada)
    cos, sin = _rope_tables(s_len, c_len)
    xc = jnp.concatenate([x, ctx], axis=1)
    w_out_bf = w_out.astype(BF16)
    mods = [mod_all[l].reshape(8, 1, N_MOD * d) for l in range(depth)]

    h1 = _norm_mod_call(xc, g_norm1[0], mods[0], s_len, 0, 1).reshape(b * n, d)
    for l in range(depth):
        mod3 = mods[l]
        pxa = _matmul_cols(h1, w_in, l, 0, ATT_COLS, BF16, 1024).reshape(b, n, ATT_COLS)
        pxb = _matmul_cols(h1, w_in, l, ATT_COLS, REST_COLS, F32, 768).reshape(b, n, REST_COLS)
        qt, kr, vt = _rope_call(pxa, cos, sin)
        mix_a = _na_call(pxa, na_rpb[l], s_len, c_len)
        mix_b, (wg, wu, wd) = _da_call(qt, kr, vt, da_lambda[l], da_subln_g[l],
                                       (exp_w_gate, exp_w_up, exp_w_down), s_len, c_len, l)
        mix_c = _pool_call(pxb, pool_w[l], pool_scale[l], s_len, c_len)
        mix_d = _sg_call(pxb, sg_norm_g[l], sg_w[l], sg_b[l])
        x1, h2, ri, rf, cnt = _out_router_call((mix_a, mix_b, mix_c, mix_d), w_out_bf, l, xc, g_norm2[l], mod3,
                                               router_w, router_b, s_len)
        next_norm = (g_norm1[l + 1], mods[l + 1]) if l + 1 < depth else None
        xc, h1 = _moe(x1, h2, ri, rf, cnt, mod3, wg, wu, wd, s_len, next_norm)
    return _final_norm(xc, g_final, s_len)
```

```python
import functools
import math

import numpy as np
import jax
import jax.numpy as jnp
from jax import lax
from jax.experimental import pallas as pl
from jax.experimental.pallas import tpu as pltpu

GRID_W = 64
NA_HEADS = 4
NA_DIM = 128
NA_WIN_H = 8
NA_WIN_W = 16
DA_HEADS = 4
DA_DIM = 64
DA_VDIM = 2 * DA_DIM
ROPE_BASE = 10000.0
POOL_GROUPS = 4
POOL_WIDTH = 128
POOL_WINDOWS = (2, 4, 8, 16)
SG_GROUPS = 4
SG_WIDTH = 128
SG_CHUNK = 128
N_EXPERT_GROUPS = 4
EXPERTS_PER_GROUP = 4
TOP_K = 2
N_MOD = 6
EPS = 1e-6
NEG_INF = -1e30
LOG2_E = math.log2(math.e)

HEAD_W = 128
MIX_W = 512
ATT_COLS = 6 * MIX_W
REST_COLS = 3 * MIX_W
MOE_BLOCK = 256
NA_Q_ROWS = 4
NA_HEADS_PER_STEP = 4
DA_HEADS_PER_STEP = 2
ROW_TILE = 256
DA_KEY_CHUNKS = (1408, 768, 512, 256, 128)
DA_VT_ROWS = DA_VDIM + 16
VMEM_LIMIT = 56 * 1024 * 1024

F32 = jnp.float32
BF16 = jnp.bfloat16


def _cparams(n_axes, vmem=None):
    return pltpu.CompilerParams(dimension_semantics=("arbitrary",) * n_axes,
                                vmem_limit_bytes=vmem)


def _pick(n, candidates):
    for c in candidates:
        if n % c == 0:
            return c
    raise ValueError(f"no tile in {candidates} divides {n}")


def _ada_kernel(c_ref, w_ref, b_ref, o_ref):
    cs = jax.nn.silu(c_ref[...]).astype(BF16)
    o_ref[...] = jnp.dot(cs, w_ref[...].astype(BF16), preferred_element_type=F32) + b_ref[...]


def _ada_all(cond, w_ada, b_ada):
    depth, d, n6 = w_ada.shape
    tn = _pick(n6, (1024, 512, 256, 128))
    return pl.pallas_call(
        _ada_kernel,
        grid=(depth, n6 // tn),
        in_specs=[pl.BlockSpec((8, d), lambda l, j: (0, 0)),
                  pl.BlockSpec((None, d, tn), lambda l, j: (l, 0, j)),
                  pl.BlockSpec((None, 1, tn), lambda l, j: (l, 0, j))],
        out_specs=pl.BlockSpec((None, 8, tn), lambda l, j: (l, 0, j)),
        out_shape=jax.ShapeDtypeStruct((depth, 8, n6), F32),
        compiler_params=_cparams(2, VMEM_LIMIT),
        name="adaln",
    )(cond, w_ada, b_ada.reshape(depth, 1, n6))


def _row_is_ctx(i, tm, s_len):
    row = i * tm + lax.broadcasted_iota(jnp.int32, (tm, 1), 0)
    return row >= s_len


def _norm_mod(x, g, shx, scx, shc, scc, is_ctx):
    y = x * lax.rsqrt(jnp.mean(x * x, axis=-1, keepdims=True) + EPS) * g
    shift = jnp.where(is_ctx, shc, shx)
    scale = jnp.where(is_ctx, scc, scx)
    return y * (1.0 + scale) + shift


def _join_norm_kernel(x_ref, c_ref, g_ref, shx_ref, scx_ref, shc_ref, scc_ref, xc_ref, h_ref, *, s_len, tm):
    i = pl.program_id(1)
    is_ctx = _row_is_ctx(i, tm, s_len)
    x = jnp.where(i < s_len // tm, x_ref[...], c_ref[...])
    xc_ref[...] = x
    h = _norm_mod(x, g_ref[...], shx_ref[...], scx_ref[...], shc_ref[...], scc_ref[...], is_ctx)
    h_ref[...] = h.astype(h_ref.dtype)


def _join_norm_call(x, ctx, g, mod3):
    b, s_len, d = x.shape
    c_len = ctx.shape[1]
    n = s_len + c_len
    tm = _pick(math.gcd(s_len, c_len), (ROW_TILE, 128))
    n_lat = s_len // tm
    mx = lambda k: pl.BlockSpec((None, 1, d), lambda bi, i: (bi, 0, k))
    mc = lambda k: pl.BlockSpec((None, 1, d), lambda bi, i: (b, 0, k))
    row_spec = pl.BlockSpec((None, tm, d), lambda bi, i: (bi, i, 0))
    return pl.pallas_call(
        functools.partial(_join_norm_kernel, s_len=s_len, tm=tm),
        grid=(b, n // tm),
        in_specs=[pl.BlockSpec((None, tm, d), lambda bi, i: (bi, jnp.minimum(i, n_lat - 1), 0)),
                  pl.BlockSpec((None, tm, d), lambda bi, i: (bi, jnp.maximum(i - n_lat, 0), 0)),
                  pl.BlockSpec((1, d), lambda bi, i: (0, 0)),
                  mx(0), mx(1), mc(0), mc(1)],
        out_specs=[row_spec, row_spec],
        out_shape=[jax.ShapeDtypeStruct((b, n, d), F32), jax.ShapeDtypeStruct((b, n, d), BF16)],
        compiler_params=_cparams(2, VMEM_LIMIT),
        name="join_norm",
    )(x, ctx, g.reshape(1, d), mod3, mod3, mod3, mod3)


def _mm_kernel(a_ref, w_ref, o_ref, wbf_ref):
    @pl.when(pl.program_id(1) == 0)
    def _():
        wbf_ref[...] = w_ref[...].astype(BF16)

    o_ref[...] = jnp.dot(a_ref[...], wbf_ref[...], preferred_element_type=F32).astype(o_ref.dtype)


def _matmul_cols(a, w, layer, col0, ncols, out_dtype, tn):
    m, k = a.shape
    tm = _pick(m, (768, 512, 384, 256, 128))
    off = col0 // tn
    return pl.pallas_call(
        _mm_kernel,
        grid=(ncols // tn, m // tm),
        in_specs=[pl.BlockSpec((tm, k), lambda j, i: (i, 0)),
                  pl.BlockSpec((None, k, tn), lambda j, i: (layer, 0, j + off))],
        out_specs=pl.BlockSpec((tm, tn), lambda j, i: (i, j)),
        out_shape=jax.ShapeDtypeStruct((m, ncols), out_dtype),
        scratch_shapes=[pltpu.VMEM((k, tn), BF16)],
        compiler_params=_cparams(2, VMEM_LIMIT),
        name="in_proj",
    )(a, w)


def _rope_tables(s_len, c_len):
    t = jnp.arange(s_len)
    row = (t // GRID_W).astype(F32)
    col = (t % GRID_W).astype(F32)
    n_freq = DA_DIM // 4
    inv_freq = 1.0 / (ROPE_BASE ** (jnp.arange(n_freq, dtype=F32) / n_freq))
    ang = jnp.concatenate([row[:, None] * inv_freq, col[:, None] * inv_freq], axis=-1)
    cos = jnp.concatenate([jnp.cos(ang), jnp.ones((c_len, DA_DIM // 2), F32)], axis=0)
    sin = jnp.concatenate([jnp.sin(ang), jnp.zeros((c_len, DA_DIM // 2), F32)], axis=0)
    return jnp.tile(cos, (1, 4)), jnp.tile(jnp.concatenate([-sin, sin], axis=-1), (1, 2))


def _rope_kernel(q_ref, k_ref, v_ref, c_ref, s_ref, qt_ref, kr_ref, vt_ref, *, tm, q_scale):
    cos = c_ref[...]
    sin = s_ref[...]
    lane = lax.broadcasted_iota(jnp.int32, (tm, HEAD_W), 1)
    first_half = (lane % DA_DIM) < (DA_DIM // 2)

    def rotate(x):
        swapped = jnp.where(first_half, pltpu.roll(x, HEAD_W - DA_DIM // 2, 1), pltpu.roll(x, DA_DIM // 2, 1))
        return x * cos + swapped * sin

    for h in range(DA_HEADS):
        cols = slice(h * HEAD_W, (h + 1) * HEAD_W)
        q = rotate(q_ref[:, cols].astype(F32)) * q_scale
        qt_ref[cols, :] = q.T.astype(qt_ref.dtype)
        kr_ref[:, cols] = rotate(k_ref[:, cols].astype(F32)).astype(kr_ref.dtype)
        vt_ref[h * DA_VT_ROWS:h * DA_VT_ROWS + DA_VDIM, :] = v_ref[:, cols].astype(F32).T.astype(vt_ref.dtype)
        pad = lax.broadcasted_iota(jnp.int32, (DA_VT_ROWS - DA_VDIM, tm), 0)
        vt_ref[h * DA_VT_ROWS + DA_VDIM:(h + 1) * DA_VT_ROWS, :] = jnp.where(pad == 0, 1.0, 0.0).astype(vt_ref.dtype)


def _rope_call(pxa, cos, sin):
    b, n, _ = pxa.shape
    tm = _pick(n, (ROW_TILE, 128))
    col_spec = lambda k: pl.BlockSpec((None, tm, MIX_W), lambda bi, i: (bi, i, k))
    t_spec = lambda rows: pl.BlockSpec((None, rows, tm), lambda bi, i: (bi, 0, i))
    return pl.pallas_call(
        functools.partial(_rope_kernel, tm=tm, q_scale=DA_DIM ** -0.5 * LOG2_E),
        grid=(b, n // tm),
        in_specs=[col_spec(3), col_spec(4), col_spec(5),
                  pl.BlockSpec((tm, HEAD_W), lambda bi, i: (i, 0)),
                  pl.BlockSpec((tm, HEAD_W), lambda bi, i: (i, 0))],
        out_specs=[t_spec(MIX_W), col_spec(0), t_spec(DA_HEADS * DA_VT_ROWS)],
        out_shape=[jax.ShapeDtypeStruct((b, MIX_W, n), BF16),
                   jax.ShapeDtypeStruct((b, n, MIX_W), BF16),
                   jax.ShapeDtypeStruct((b, DA_HEADS * DA_VT_ROWS, n), BF16)],
        compiler_params=_cparams(2, VMEM_LIMIT),
        name="rope",
    )(pxa, pxa, pxa, cos, sin)


def _na_plan(rows, r_q):
    kh = min(NA_WIN_H, rows)
    key_rows = r_q + kh - 1
    assert rows % r_q == 0 and rows >= key_rows
    patterns, var, ustart = [], [], []
    for j in range(rows // r_q):
        r = r_q * j + np.arange(r_q)
        rs = np.clip(r - kh // 2, 0, rows - kh)
        u = int(np.clip(rs.min(), 0, rows - key_rows))
        kr = u + np.arange(key_rows)
        valid = (kr[None, :] >= rs[:, None]) & (kr[None, :] < rs[:, None] + kh)
        assert (valid.sum(1) == kh).all()
        dr0 = u - r + (NA_WIN_H - 1)
        key = (valid.tobytes(), dr0.tobytes())
        for v, (k2, _, _) in enumerate(patterns):
            if k2 == key:
                break
        else:
            v = len(patterns)
            patterns.append((key, valid, dr0))
        var.append(v)
        ustart.append(u * GRID_W)
    valid = np.stack([p[1] for p in patterns])
    dr0 = np.stack([p[2] for p in patterns])
    return np.asarray(var, np.int32), np.asarray(ustart, np.int32), valid, dr0


def _na_bias_tables(rpb, valid, dr0):
    col = np.arange(GRID_W)
    cs = np.clip(col - NA_WIN_W // 2, 0, GRID_W - NA_WIN_W)
    cmask = (col[None, :] >= cs[:, None]) & (col[None, :] < cs[:, None] + NA_WIN_W)
    dc = np.clip(col[None, :] - col[:, None], -(NA_WIN_W - 1), NA_WIN_W - 1) + (NA_WIN_W - 1)
    heads, n_dr, n_dc = rpb.shape
    v, r_q, key_rows = valid.shape
    pick = (dc.reshape(1, -1) == np.arange(n_dc)[:, None]).astype(np.float32)
    by_col = jnp.dot(rpb.astype(F32).reshape(heads * n_dr, n_dc), pick,
                     precision=lax.Precision.HIGHEST).reshape(heads, n_dr, GRID_W, GRID_W)
    padded = jnp.pad(by_col, ((0, 0), (key_rows, key_rows), (0, 0), (0, 0)))
    tab = jnp.stack([lax.slice_in_dim(padded, int(d) + key_rows, int(d) + 2 * key_rows, axis=1)
                     for d in dr0.reshape(-1)], axis=1)
    tab = tab.reshape(heads, v, r_q, key_rows, GRID_W, GRID_W).transpose(0, 1, 2, 4, 3, 5)
    mask = valid[:, :, None, :, None] & cmask[None, None, :, None, :]
    tab = jnp.where(mask[None], tab, NEG_INF)
    return tab.reshape(heads, v, r_q * GRID_W, key_rows * GRID_W)


def _na_kernel(var_ref, u_ref, q_ref, k_ref, v_ref, kc_ref, vc_ref, bias_ref, o_ref, *, n_lat, kw, scale):
    j = pl.program_id(2)
    nt = (((1,), (1,)), ((), ()))

    def one_head(hh, local):
        cols = slice(hh * HEAD_W, (hh + 1) * HEAD_W)
        q = q_ref[:, cols]
        vc = vc_ref[:, cols]
        s_ctx = lax.dot_general(q, kc_ref[:, cols], nt, preferred_element_type=F32) * scale
        m = jnp.max(s_ctx, axis=-1, keepdims=True)
        if local:
            u = pl.multiple_of(u_ref[j], GRID_W)
            s = lax.dot_general(q, k_ref[pl.ds(u, kw), cols], nt, preferred_element_type=F32) * scale + bias_ref[hh]
            m = jnp.maximum(m, jnp.max(s, axis=-1, keepdims=True))
            p = jnp.exp(s - m)
        pc = jnp.exp(s_ctx - m)
        l = jnp.sum(pc, axis=-1, keepdims=True)
        o = jnp.dot(pc.astype(BF16), vc, preferred_element_type=F32)
        if local:
            l += jnp.sum(p, axis=-1, keepdims=True)
            o += jnp.dot(p.astype(BF16), v_ref[pl.ds(u, kw), cols], preferred_element_type=F32)
        o_ref[:, cols] = (o / l).astype(o_ref.dtype)

    @pl.when(j < n_lat)
    def _():
        for hh in range(NA_HEADS_PER_STEP):
            one_head(hh, True)

    @pl.when(j >= n_lat)
    def _():
        for hh in range(NA_HEADS_PER_STEP):
            one_head(hh, False)


def _na_call(pxa, rpb, s_len, c_len):
    b, n, _ = pxa.shape
    rows = s_len // GRID_W
    tq = NA_Q_ROWS * GRID_W
    hps = NA_HEADS_PER_STEP
    assert s_len % tq == 0 and c_len % tq == 0 and s_len % c_len == 0 and NA_HEADS % hps == 0
    var, ustart, valid, dr0 = _na_plan(rows, NA_Q_ROWS)
    n_lat, n_q = s_len // tq, n // tq
    kw = valid.shape[2] * GRID_W
    tabs = _na_bias_tables(rpb, valid, dr0)
    pad = np.zeros(n_q - n_lat, np.int32)
    var = jnp.asarray(np.concatenate([var, pad]))
    ustart = jnp.asarray(np.concatenate([ustart, pad]))
    ctx_blk = s_len // c_len
    wide = hps * HEAD_W
    k_blk, v_blk = NA_HEADS // hps, 2 * NA_HEADS // hps
    grid_spec = pltpu.PrefetchScalarGridSpec(
        num_scalar_prefetch=2,
        grid=(b, NA_HEADS // hps, n_q),
        in_specs=[pl.BlockSpec((None, tq, wide), lambda bi, h, j, vr, ur: (bi, j, h)),
                  pl.BlockSpec((None, s_len, wide), lambda bi, h, j, vr, ur: (bi, 0, k_blk + h)),
                  pl.BlockSpec((None, s_len, wide), lambda bi, h, j, vr, ur: (bi, 0, v_blk + h)),
                  pl.BlockSpec((None, c_len, wide), lambda bi, h, j, vr, ur: (bi, ctx_blk, k_blk + h)),
                  pl.BlockSpec((None, c_len, wide), lambda bi, h, j, vr, ur: (bi, ctx_blk, v_blk + h)),
                  pl.BlockSpec((hps, None, tq, kw), lambda bi, h, j, vr, ur: (h, vr[j], 0, 0))],
        out_specs=pl.BlockSpec((None, tq, wide), lambda bi, h, j, vr, ur: (bi, j, h)),
    )
    return pl.pallas_call(
        functools.partial(_na_kernel, n_lat=n_lat, kw=kw, scale=NA_DIM ** -0.5),
        grid_spec=grid_spec,
        out_shape=jax.ShapeDtypeStruct((b, n, MIX_W), BF16),
        compiler_params=_cparams(3, VMEM_LIMIT),
        name="nbr_attn",
    )(var, ustart, pxa, pxa, pxa, pxa, pxa, tabs)


def _cast_expert_weights(step, n_steps, layer, srcs, dsts, in_g, in_u, in_d, out_g, out_u, out_d, sem_in, sem_out):
    ins, outs = (in_g, in_u, in_d), (out_g, out_u, out_d)
    n_chunks = srcs[0].shape[1] // in_g.shape[1]
    cur = jnp.minimum(step, n_chunks - 1)
    nxt = jnp.minimum(step + 1, n_chunks - 1)
    slot = step % 2
    other = 1 - slot

    def in_copy(t, chunk, sl):
        rows = ins[t].shape[1]
        src = srcs[t].at[layer, pl.ds(pl.multiple_of(chunk * rows, rows), rows)]
        return pltpu.make_async_copy(src, ins[t].at[sl], sem_in.at[sl])

    def out_copy(t, chunk, sl):
        rows = outs[t].shape[1]
        dst = dsts[t].at[pl.ds(pl.multiple_of(chunk * rows, rows), rows)]
        return pltpu.make_async_copy(outs[t].at[sl], dst, sem_out.at[sl])

    @pl.when(step == 0)
    def _():
        for t in range(3):
            in_copy(t, cur, slot).start()

    for t in range(3):
        in_copy(t, nxt, other).start()
    for t in range(3):
        in_copy(t, cur, slot).wait()

    @pl.when(step > 0)
    def _():
        for t in range(3):
            out_copy(t, cur, other).wait()

    for t in range(3):
        outs[t][slot] = ins[t][slot].astype(BF16)
    for t in range(3):
        out_copy(t, cur, slot).start()

    @pl.when(step == n_steps - 1)
    def _():
        for t in range(3):
            out_copy(t, cur, slot).wait()
            in_copy(t, nxt, other).wait()


def _da_kernel(qt_ref, k_ref, vt_ref, lam_ref, g_ref, wg_ref, wu_ref, wd_ref, o_ref, wgb_ref, wub_ref, wdb_ref,
               *scratch, s_len, c_len, tq, tk, lam_init, heads, layer):
    score_refs, cast_refs = scratch[:2 * heads], scratch[2 * heads:]
    sa_refs, sb_refs = score_refs[0::2], score_refs[1::2]
    qi = pl.program_id(2)
    step = (pl.program_id(0) * pl.num_programs(1) + pl.program_id(1)) * pl.num_programs(2) + qi
    n_steps = pl.num_programs(0) * pl.num_programs(1) * pl.num_programs(2)
    _cast_expert_weights(step, n_steps, layer, (wg_ref, wu_ref, wd_ref), (wgb_ref, wub_ref, wdb_ref), *cast_refs)
    n = s_len + c_len
    feat = lax.broadcasted_iota(jnp.int32, (2 * DA_DIM, tq), 0)

    def q_tiles(idx):
        start = pl.multiple_of(idx * tq, tq)
        out = []
        for hh in range(heads):
            qt = qt_ref[hh * HEAD_W:(hh + 1) * HEAD_W, pl.ds(start, tq)]
            zero = jnp.zeros_like(qt)
            out.append(jnp.concatenate([jnp.where(feat < DA_DIM, qt, zero), jnp.where(feat >= DA_DIM, qt, zero)],
                                       axis=1))
        return out

    q2 = q_tiles(qi)

    def k_rows(hh, rows):
        return k_ref[rows, hh * HEAD_W:(hh + 1) * HEAD_W]

    def vt_cols(hh, cols):
        return vt_ref[hh * DA_VT_ROWS:(hh + 1) * DA_VT_ROWS, cols]

    def scores(k_blk, q):
        return jnp.dot(k_blk, q, preferred_element_type=F32)

    def update(s, vt_blk, carry):
        m, acc = carry
        m_new = jnp.maximum(m, jnp.max(s, axis=0, keepdims=True))
        alpha = jnp.exp2(m - m_new)
        p = jnp.exp2(s - m_new)
        acc_new = alpha * acc + jnp.dot(vt_blk, p.astype(BF16), preferred_element_type=F32)
        return m_new, acc_new

    def init():
        one = (jnp.full((1, 2 * tq), NEG_INF, F32), jnp.zeros((DA_VT_ROWS, 2 * tq), F32))
        return tuple(one for _ in range(heads))

    def finalize(stats):
        lv = lam_ref[...]
        t1 = jnp.sum(lv[0:1, :] * lv[1:2, :], axis=-1, keepdims=True)
        t2 = jnp.sum(lv[2:3, :] * lv[3:4, :], axis=-1, keepdims=True)
        lam = jnp.exp(t1) - jnp.exp(t2) + lam_init
        for hh in range(heads):
            _, acc = stats[hh]
            o = acc[:DA_VDIM, :] / acc[DA_VDIM:DA_VDIM + 1, :]
            o = o[:, :tq] - lam * o[:, tq:]
            y = o * lax.rsqrt(jnp.mean(o * o, axis=0, keepdims=True) + EPS) * g_ref[...]
            o_ref[:, hh * HEAD_W:(hh + 1) * HEAD_W] = (y * (1.0 - lam_init)).T.astype(o_ref.dtype)

    k_steps = n // tk
    full_pairs = (k_steps - 1) // 2
    carry_first = k_steps % 2 == 0

    @pl.when(qi < s_len // tq)
    def _():
        def scores_into(dst_refs, step, qs):
            rows = pl.ds(pl.multiple_of(step * tk, tk), tk)
            for hh in range(heads):
                dst_refs[hh][...] = scores(k_rows(hh, rows), qs[hh])

        def update_from(src_refs, step, stats):
            cols = pl.ds(pl.multiple_of(step * tk, tk), tk)
            return tuple(update(src_refs[hh][...], vt_cols(hh, cols), stats[hh]) for hh in range(heads))

        def pair(j, stats):
            c0 = 2 * j
            scores_into(sb_refs, c0 + 1, q2)
            stats = update_from(sa_refs, c0, stats)
            scores_into(sa_refs, c0 + 2, q2)
            return update_from(sb_refs, c0 + 1, stats)

        if carry_first:
            @pl.when(qi == 0)
            def _():
                scores_into(sa_refs, 0, q2)
        else:
            scores_into(sa_refs, 0, q2)
        stats = lax.fori_loop(0, full_pairs, pair, init())
        c0 = 2 * full_pairs
        if carry_first:
            scores_into(sb_refs, c0 + 1, q2)
            stats = update_from(sa_refs, c0, stats)
            scores_into(sa_refs, 0, q_tiles(qi + 1))
            stats = update_from(sb_refs, c0 + 1, stats)
        else:
            stats = update_from(sa_refs, c0, stats)
        finalize(stats)

    @pl.when(qi >= s_len // tq)
    def _():
        rows, stats = slice(s_len, n), init()
        finalize(tuple(update(scores(k_rows(hh, rows), q2[hh]), vt_cols(hh, rows), stats[hh])
                       for hh in range(heads)))


def _da_call(qt, kr, vt, lam_vecs, subln_g, exp_w, s_len, c_len, layer):
    b, n, _ = kr.shape
    tq = _pick(math.gcd(s_len, c_len), (256, 128))
    tk = _pick(n, DA_KEY_CHUNKS)
    hps = DA_HEADS_PER_STEP
    assert DA_HEADS % hps == 0
    lam_init = 0.8 - 0.6 * math.exp(-0.3 * layer)
    grid = (b, DA_HEADS // hps, n // tq)
    n_chunks = 1 << (math.prod(grid).bit_length() - 1)
    flat = [w.reshape(w.shape[0], w.shape[1] * w.shape[2], w.shape[3]) for w in exp_w]
    chunk = [(w.shape[1] // n_chunks, w.shape[2]) for w in flat]
    assert all(w.shape[1] % n_chunks == 0 and r % 16 == 0 for w, (r, _) in zip(flat, chunk))
    once = pl.Buffered(1)
    any_spec = pl.BlockSpec(memory_space=pl.ANY)
    outs = pl.pallas_call(
        functools.partial(_da_kernel, s_len=s_len, c_len=c_len, tq=tq, tk=tk, lam_init=lam_init, heads=hps,
                          layer=layer),
        grid=grid,
        in_specs=[pl.BlockSpec((None, hps * HEAD_W, n), lambda bi, h, i: (bi, h, 0), pipeline_mode=once),
                  pl.BlockSpec((None, n, hps * HEAD_W), lambda bi, h, i: (bi, 0, h), pipeline_mode=once),
                  pl.BlockSpec((None, hps * DA_VT_ROWS, n), lambda bi, h, i: (bi, h, 0), pipeline_mode=once),
                  pl.BlockSpec((4, DA_DIM), lambda bi, h, i: (0, 0)),
                  pl.BlockSpec((DA_VDIM, 1), lambda bi, h, i: (0, 0)),
                  any_spec, any_spec, any_spec],
        out_specs=[pl.BlockSpec((None, tq, hps * HEAD_W), lambda bi, h, i: (bi, i, h)),
                   any_spec, any_spec, any_spec],
        out_shape=[jax.ShapeDtypeStruct((b, n, MIX_W), BF16)]
                  + [jax.ShapeDtypeStruct(w.shape[1:], BF16) for w in flat],
        scratch_shapes=[pltpu.VMEM((tk, 2 * tq), F32)] * (2 * hps)
                       + [pltpu.VMEM((2,) + c, F32) for c in chunk]
                       + [pltpu.VMEM((2,) + c, BF16) for c in chunk]
                       + [pltpu.SemaphoreType.DMA((2,)), pltpu.SemaphoreType.DMA((2,))],
        compiler_params=_cparams(3, VMEM_LIMIT),
        name="diff_attn",
    )(qt, kr, vt, lam_vecs, subln_g.reshape(DA_VDIM, 1), *flat)
    return outs[0], tuple(o.reshape(w.shape[1:]) for o, w in zip(outs[1:], exp_w))


POOL_HALO = 8


def _pool_kernel(prev_ref, cur_ref, next_ref, w_ref, sc_ref, o_ref, buf_ref, *, s_len, c_len, tm):
    i = pl.program_id(1)
    n_lat = s_len // tm
    n_all = (s_len + c_len) // tm
    seq_start = (i == 0) | (i == n_lat)
    seq_end = (i == n_lat - 1) | (i == n_all - 1)
    buf_ref[0:POOL_HALO, :] = jnp.where(seq_start, 0.0, prev_ref[...])
    buf_ref[POOL_HALO:POOL_HALO + tm, :] = cur_ref[...]
    buf_ref[POOL_HALO + tm:, :] = jnp.where(seq_end, 0.0, next_ref[...])
    in_ctx = i >= n_lat
    seq_len = jnp.where(in_ctx, c_len, s_len)
    pos = i * tm - jnp.where(in_ctx, s_len, 0) + lax.broadcasted_iota(jnp.int32, (tm, 1), 0)
    for g, win in enumerate(POOL_WINDOWS):
        half = win // 2
        cols = slice(g * POOL_WIDTH, (g + 1) * POOL_WIDTH)
        tot = buf_ref[POOL_HALO - half:POOL_HALO - half + tm, cols]
        for d in range(-half + 1, half):
            tot = tot + buf_ref[POOL_HALO + d:POOL_HALO + d + tm, cols]
        cnt = jnp.clip(pos + half, 0, seq_len) - jnp.clip(pos - half, 0, seq_len)
        resid = tot / cnt.astype(F32) - cur_ref[:, cols]
        y = jnp.dot(resid.astype(BF16), w_ref[g].astype(BF16), preferred_element_type=F32)
        o_ref[:, cols] = (y * sc_ref[:, cols]).astype(o_ref.dtype)


def _pool_call(pxb, pool_w, pool_scale, s_len, c_len):
    b, n, _ = pxb.shape
    tm = _pick(math.gcd(s_len, c_len), (ROW_TILE, 128))
    hb = tm // POOL_HALO
    last = n // POOL_HALO - 1
    return pl.pallas_call(
        functools.partial(_pool_kernel, s_len=s_len, c_len=c_len, tm=tm),
        grid=(b, n // tm),
        in_specs=[pl.BlockSpec((None, POOL_HALO, MIX_W), lambda bi, i: (bi, jnp.maximum(i * hb - 1, 0), 0)),
                  pl.BlockSpec((None, tm, MIX_W), lambda bi, i: (bi, i, 0)),
                  pl.BlockSpec((None, POOL_HALO, MIX_W), lambda bi, i: (bi, jnp.minimum((i + 1) * hb, last), 0)),
                  pl.BlockSpec((POOL_GROUPS, POOL_WIDTH, POOL_WIDTH), lambda bi, i: (0, 0, 0)),
                  pl.BlockSpec((1, MIX_W), lambda bi, i: (0, 0))],
        out_specs=pl.BlockSpec((None, tm, MIX_W), lambda bi, i: (bi, i, 0)),
        out_shape=jax.ShapeDtypeStruct((b, n, MIX_W), BF16),
        scratch_shapes=[pltpu.VMEM((tm + 2 * POOL_HALO, MIX_W), F32)],
        compiler_params=_cparams(2, VMEM_LIMIT),
        name="pool",
    )(pxb, pxb, pxb, pool_w, pool_scale.reshape(1, MIX_W))


def _sg_kernel(u_ref, v_ref, g_ref, w_ref, b_ref, o_ref, *, tm):
    v = jax.nn.gelu(v_ref[...])
    vn = (v * lax.rsqrt(jnp.mean(v * v, axis=-1, keepdims=True) + EPS) * g_ref[...]).astype(BF16)
    bias = b_ref[...]
    for c in range(tm // SG_CHUNK):
        rows = slice(c * SG_CHUNK, (c + 1) * SG_CHUNK)
        for g in range(SG_GROUPS):
            cols = slice(g * SG_WIDTH, (g + 1) * SG_WIDTH)
            mixed = jnp.dot(w_ref[g].astype(BF16), vn[rows, cols], preferred_element_type=F32)
            mixed = mixed + bias[:, g:g + 1]
            o_ref[rows, cols] = (jax.nn.gelu(u_ref[rows, cols]) * mixed).astype(o_ref.dtype)


def _sg_call(pxb, sg_norm_g, sg_w, sg_b):
    b, n, _ = pxb.shape
    tm = _pick(n, (ROW_TILE, 128))
    return pl.pallas_call(
        functools.partial(_sg_kernel, tm=tm),
        grid=(b, n // tm),
        in_specs=[pl.BlockSpec((None, tm, MIX_W), lambda bi, i: (bi, i, 1)),
                  pl.BlockSpec((None, tm, MIX_W), lambda bi, i: (bi, i, 2)),
                  pl.BlockSpec((1, MIX_W), lambda bi, i: (0, 0)),
                  pl.BlockSpec((SG_GROUPS, SG_CHUNK, SG_CHUNK), lambda bi, i: (0, 0, 0)),
                  pl.BlockSpec((SG_CHUNK, SG_GROUPS), lambda bi, i: (0, 0))],
        out_specs=pl.BlockSpec((None, tm, MIX_W), lambda bi, i: (bi, i, 0)),
        out_shape=jax.ShapeDtypeStruct((b, n, MIX_W), BF16),
        compiler_params=_cparams(2, VMEM_LIMIT),
        name="spatial_gate",
    )(pxb, pxb, sg_norm_g.reshape(1, MIX_W), sg_w, sg_b.T)


def _pack_bf16_pairs(x):
    w = x.shape[1] // 2
    hi = lax.bitcast_convert_type(x[:, :w].astype(BF16).astype(F32), jnp.uint32)
    lo = lax.bitcast_convert_type(x[:, w:].astype(BF16).astype(F32), jnp.uint32)
    return hi | (lo >> 16)


def _unpack_bf16_pairs(u):
    hi = lax.bitcast_convert_type(u & jnp.uint32(0xFFFF0000), F32).astype(BF16)
    lo = lax.bitcast_convert_type(u << 16, F32).astype(BF16)
    return jnp.concatenate([hi, lo], axis=1)


def _first_argmax(vals):
    best, idx = vals[0], jnp.zeros(vals[0].shape, jnp.int32)
    for k in range(1, len(vals)):
        take = vals[k] > best
        best = jnp.where(take, vals[k], best)
        idx = jnp.where(take, k, idx)
    return idx, best


def _out_router_kernel(ma_ref, mb_ref, mc_ref, md_ref, w_ref, x_ref, gx_ref, gc_ref,
                       g_ref, shx_ref, scx_ref, shc_ref, scc_ref, rw_ref, rb_ref,
                       x1_ref, h_ref, ri_ref, rf_ref, cnt_ref, xs0_ref, run_ref, zero_ref, zsem, *, s_len, tm, n_exp):
    bi, i = pl.program_id(0), pl.program_id(1)
    step = bi * pl.num_programs(1) + i
    rows = zero_ref.shape[0]
    chunk = jnp.minimum(step, xs0_ref.shape[0] // rows - 1)
    fill = pltpu.make_async_copy(zero_ref, xs0_ref.at[pl.ds(pl.multiple_of(chunk * rows, rows), rows)], zsem)

    @pl.when(step == 0)
    def _():
        run_ref[...] = jnp.zeros_like(run_ref)
        zero_ref[...] = jnp.zeros_like(zero_ref)

    @pl.when(step > 0)
    def _():
        fill.wait()

    fill.start()

    mix = jnp.concatenate([ma_ref[...], mb_ref[...], mc_ref[...], md_ref[...]], axis=1)
    acc = jnp.dot(mix, w_ref[...], preferred_element_type=F32)
    is_ctx = _row_is_ctx(i, tm, s_len)
    x1 = x_ref[...] + jnp.where(is_ctx, gc_ref[...], gx_ref[...]) * acc
    x1_ref[...] = x1
    h = _norm_mod(x1, g_ref[...], shx_ref[...], scx_ref[...], shc_ref[...], scc_ref[...], is_ctx)
    h_ref[...] = _pack_bf16_pairs(h)

    h_hi = h.astype(BF16)
    h_lo = (h - h_hi.astype(F32)).astype(BF16)
    part = jnp.dot(h_hi, rw_ref[0], preferred_element_type=F32)
    part += jnp.dot(h_lo, rw_ref[1], preferred_element_type=F32)
    part_t = part.T
    logits = part_t[0:n_exp, :] + part_t[n_exp:2 * n_exp, :]
    scores = jax.nn.sigmoid(logits)
    biased = scores + rb_ref[...]
    b_rows = [biased[e:e + 1, :] for e in range(n_exp)]
    s_rows = [scores[e:e + 1, :] for e in range(n_exp)]
    epg = EXPERTS_PER_GROUP
    group_scores = []
    for g in range(N_EXPERT_GROUPS):
        r = b_rows[g * epg:(g + 1) * epg]
        pair = None
        for a in range(epg):
            for c in range(a + 1, epg):
                t = r[a] + r[c]
                pair = t if pair is None else jnp.maximum(pair, t)
        group_scores.append(pair)
    group, _ = _first_argmax(group_scores)
    in_group = []
    for k in range(epg):
        sel = b_rows[k]
        for g in range(1, N_EXPERT_GROUPS):
            sel = jnp.where(group == g, b_rows[g * epg + k], sel)
        in_group.append(sel)
    loc0, _ = _first_argmax(in_group)
    rest = [jnp.where(loc0 == k, -jnp.inf, in_group[k]) for k in range(epg)]
    loc1, _ = _first_argmax(rest)
    e0 = group * epg + loc0
    e1 = group * epg + loc1
    g0 = jnp.zeros_like(s_rows[0])
    g1 = jnp.zeros_like(s_rows[0])
    for e in range(n_exp):
        g0 = jnp.where(e0 == e, s_rows[e], g0)
        g1 = jnp.where(e1 == e, s_rows[e], g1)
    tot = g0 + g1

    eidx = lax.broadcasted_iota(jnp.int32, (n_exp, tm), 0)
    hit0 = eidx == e0
    hit1 = eidx == e1
    onehot = jnp.where(hit0 | hit1, 1.0, 0.0)
    before = lax.broadcasted_iota(jnp.int32, (tm, tm), 0) < lax.broadcasted_iota(jnp.int32, (tm, tm), 1)
    tri = jnp.where(before, 1.0, 0.0).astype(BF16)
    run = run_ref[:, 0:1]
    rank = jnp.dot(onehot.astype(BF16), tri, preferred_element_type=F32) + run
    rank0 = jnp.sum(jnp.where(hit0, rank, 0.0), axis=0, keepdims=True)
    rank1 = jnp.sum(jnp.where(hit1, rank, 0.0), axis=0, keepdims=True)
    new_run = run + jnp.sum(onehot, axis=1, keepdims=True)
    run_ref[...] = jnp.broadcast_to(new_run, run_ref.shape)
    cnt_ref[...] = jnp.broadcast_to(new_run, cnt_ref.shape).astype(jnp.int32)

    zi = jnp.zeros((4, tm), jnp.int32)
    ri_ref[...] = jnp.concatenate([e0, e1, rank0.astype(jnp.int32), rank1.astype(jnp.int32), zi], axis=0)
    zf = jnp.zeros((6, tm), F32)
    rf_ref[...] = jnp.concatenate([g0 / tot, g1 / tot, zf], axis=0)

    @pl.when(step == pl.num_programs(0) * pl.num_programs(1) - 1)
    def _():
        fill.wait()


def _router_slabs(router_w):
    d, n_exp = router_w.shape
    w_hi = router_w.astype(BF16)
    w_lo = (router_w - w_hi.astype(F32)).astype(BF16)
    zeros = lambda cols: jnp.zeros((d, cols), BF16)
    return jnp.stack([jnp.concatenate([w_hi, w_lo, zeros(HEAD_W - 2 * n_exp)], axis=1),
                      jnp.concatenate([w_hi, zeros(HEAD_W - n_exp)], axis=1)])


def _out_router_call(mixes, w_out_bf, layer, xc, g2, mod3, router_w, router_b, s_len):
    b, n, d = xc.shape
    n_exp = router_w.shape[1]
    tm = _pick(n, (ROW_TILE, 128))
    per_b = n // tm
    mx = lambda k: pl.BlockSpec((None, 1, d), lambda bi, i: (bi, 0, k))
    mc = lambda k: pl.BlockSpec((None, 1, d), lambda bi, i: (b, 0, k))
    mix_spec = pl.BlockSpec((None, tm, MIX_W), lambda bi, i: (bi, i, 0))
    row_spec = pl.BlockSpec((None, tm, d), lambda bi, i: (bi, i, 0))
    tok_spec = pl.BlockSpec((8, tm), lambda bi, i: (0, bi * per_b + i))
    n_pad = b * n * TOP_K + n_exp * MOE_BLOCK
    fill_rows = n_pad // (1 << ((b * per_b).bit_length() - 1))
    assert n_pad % fill_rows == 0 and fill_rows % 8 == 0
    return pl.pallas_call(
        functools.partial(_out_router_kernel, s_len=s_len, tm=tm, n_exp=n_exp),
        grid=(b, per_b),
        in_specs=[mix_spec, mix_spec, mix_spec, mix_spec,
                  pl.BlockSpec((None, 4 * MIX_W, d), lambda bi, i: (layer, 0, 0)),
                  row_spec, mx(2), mc(2),
                  pl.BlockSpec((1, d), lambda bi, i: (0, 0)),
                  mx(3), mx(4), mc(3), mc(4),
                  pl.BlockSpec((2, d, HEAD_W), lambda bi, i: (0, 0, 0)),
                  pl.BlockSpec((n_exp, 1), lambda bi, i: (0, 0))],
        out_specs=[row_spec, pl.BlockSpec((None, tm, d // 2), lambda bi, i: (bi, i, 0)), tok_spec, tok_spec,
                   pl.BlockSpec((n_exp, HEAD_W), lambda bi, i: (0, 0)),
                   pl.BlockSpec(memory_space=pl.ANY)],
        out_shape=[jax.ShapeDtypeStruct((b, n, d), F32),
                   jax.ShapeDtypeStruct((b, n, d // 2), jnp.uint32),
                   jax.ShapeDtypeStruct((8, b * n), jnp.int32),
                   jax.ShapeDtypeStruct((8, b * n), F32),
                   jax.ShapeDtypeStruct((n_exp, HEAD_W), jnp.int32),
                   jax.ShapeDtypeStruct((n_pad, d // 2), jnp.uint32)],
        scratch_shapes=[pltpu.VMEM((n_exp, HEAD_W), F32), pltpu.VMEM((fill_rows, d // 2), jnp.uint32),
                        pltpu.SemaphoreType.DMA],
        compiler_params=_cparams(2, VMEM_LIMIT),
        name="out_router",
    )(*mixes, w_out_bf, xc, mod3, mod3, g2.reshape(1, d), mod3, mod3, mod3, mod3,
      _router_slabs(router_w), router_b.reshape(n_exp, 1))


def _row_copy(src_ref, src_row, dst_ref, dst_row, sem):
    return pltpu.make_async_copy(src_ref.at[pl.ds(src_row, 1)], dst_ref.at[pl.ds(dst_row, 1)], sem)


def _dispatch_kernel(p0_ref, p1_ref, h_ref, xs_in_ref, xs_ref, sem, *, tm):
    del xs_in_ref
    base = pl.program_id(0) * tm

    for r in range(tm):
        _row_copy(h_ref, r, xs_ref, p0_ref[base + r], sem).start()
        _row_copy(h_ref, r, xs_ref, p1_ref[base + r], sem).start()

    tile_copy = pltpu.make_async_copy(h_ref, xs_ref.at[pl.ds(0, tm)], sem)
    tile_copy.wait()
    tile_copy.wait()


def _dispatch_call(h2, pos0, pos1, xs0):
    t, d = h2.shape
    n_pad = xs0.shape[0]
    tm = _pick(t, (ROW_TILE, 128))
    grid_spec = pltpu.PrefetchScalarGridSpec(
        num_scalar_prefetch=2,
        grid=(t // tm,),
        in_specs=[pl.BlockSpec((tm, d), lambda i, p0, p1: (i, 0)),
                  pl.BlockSpec(memory_space=pl.ANY)],
        out_specs=pl.BlockSpec(memory_space=pl.ANY),
        scratch_shapes=[pltpu.SemaphoreType.DMA],
    )
    return pl.pallas_call(
        functools.partial(_dispatch_kernel, tm=tm),
        grid_spec=grid_spec,
        out_shape=jax.ShapeDtypeStruct((n_pad, d), h2.dtype),
        input_output_aliases={3: 0},
        compiler_params=_cparams(1, VMEM_LIMIT),
        name="dispatch",
    )(pos0, pos1, h2, xs0)


def _ffn_kernel(be_ref, nb_ref, x_ref, wg_ref, wu_ref, wd_ref, y_ref):
    blk = pl.program_id(0)

    @pl.when(blk < nb_ref[0])
    def _():
        x = _unpack_bf16_pairs(x_ref[...])
        gate = jnp.dot(x, wg_ref[...], preferred_element_type=F32)
        up = jnp.dot(x, wu_ref[...], preferred_element_type=F32)
        hid = (jax.nn.silu(gate) * up).astype(BF16)
        y_ref[...] = jnp.dot(hid, wd_ref[...], preferred_element_type=F32)

    @pl.when(blk >= nb_ref[0])
    def _():
        y_ref[...] = jnp.zeros_like(y_ref)


def _ffn_call(xs, blk_expert, n_used, wg, wu, wd):
    n_pad, half = xs.shape
    d = 2 * half
    d_exp = wg.shape[2]
    grid_spec = pltpu.PrefetchScalarGridSpec(
        num_scalar_prefetch=2,
        grid=(n_pad // MOE_BLOCK,),
        in_specs=[pl.BlockSpec((MOE_BLOCK, half), lambda i, be, nb: (i, 0)),
                  pl.BlockSpec((None, d, d_exp), lambda i, be, nb: (be[i], 0, 0)),
                  pl.BlockSpec((None, d, d_exp), lambda i, be, nb: (be[i], 0, 0)),
                  pl.BlockSpec((None, d_exp, d), lambda i, be, nb: (be[i], 0, 0))],
        out_specs=pl.BlockSpec((MOE_BLOCK, d), lambda i, be, nb: (i, 0)),
    )
    return pl.pallas_call(
        _ffn_kernel,
        grid_spec=grid_spec,
        out_shape=jax.ShapeDtypeStruct((n_pad, d), F32),
        compiler_params=_cparams(1, VMEM_LIMIT),
        name="expert_ffn",
    )(blk_expert, n_used, xs, wg, wu, wd)


def _combine_kernel(p0_ref, p1_ref, ys_ref, x_ref, gt_ref, gx_ref, gc_ref, *rest, s_len, n_len, tm, with_next):
    if with_next:
        g_ref, shx_ref, scx_ref, shc_ref, scc_ref, o_ref, h_ref, y0_ref, y1_ref, sem = rest
    else:
        g_ref, o_ref, y0_ref, y1_ref, sem = rest
    i = pl.program_id(0)
    last = pl.num_programs(0) - 1

    def start_tile(tile, slot):
        base = tile * tm
        for r in range(tm):
            _row_copy(ys_ref, p0_ref[base + r], y0_ref.at[slot], r, sem.at[slot]).start()
            _row_copy(ys_ref, p1_ref[base + r], y1_ref.at[slot], r, sem.at[slot]).start()

    def wait_tile(slot):
        pltpu.make_async_copy(ys_ref.at[pl.ds(0, tm)], y0_ref.at[slot], sem.at[slot]).wait()
        pltpu.make_async_copy(ys_ref.at[pl.ds(0, tm)], y1_ref.at[slot], sem.at[slot]).wait()

    def combine(slot):
        row = ((i * tm) % n_len) + lax.broadcasted_iota(jnp.int32, (tm, 1), 0)
        is_ctx = row >= s_len
        gate = jnp.where(is_ctx, gc_ref[...], gx_ref[...])
        gt = gt_ref[...]
        y = gt[:, 0:1] * y0_ref[slot] + gt[:, 1:2] * y1_ref[slot]
        x_new = x_ref[...] + gate * y
        if with_next:
            o_ref[...] = x_new
            h = _norm_mod(x_new, g_ref[...], shx_ref[...], scx_ref[...], shc_ref[...], scc_ref[...], is_ctx)
            h_ref[...] = h.astype(h_ref.dtype)
        else:
            @pl.when((i * tm) % n_len < s_len)
            def _():
                o_ref[...] = x_new * lax.rsqrt(jnp.mean(x_new * x_new, axis=-1, keepdims=True) + EPS) * g_ref[...]

    @pl.when(i == 0)
    def _():
        start_tile(0, 0)

    for slot in (0, 1):
        @pl.when(i % 2 == slot)
        def _(slot=slot):
            wait_tile(slot)
            start_tile(jnp.minimum(i + 1, last), 1 - slot)
            combine(slot)

            @pl.when(i == last)
            def _():
                wait_tile(1 - slot)


def _combine_call(ys, pos0, pos1, x1, gates, mod3, s_len, k_gate, next_norm, final_g):
    b, n, d = x1.shape
    t = b * n
    tm = _pick(n, (ROW_TILE, 128))
    row_spec = pl.BlockSpec((tm, d), lambda i, p0, p1: (i, 0))
    mx = lambda k: pl.BlockSpec((None, 1, d), lambda i, p0, p1: ((i * tm) // n, 0, k))
    mc = lambda k: pl.BlockSpec((None, 1, d), lambda i, p0, p1: (b, 0, k))
    in_specs = [pl.BlockSpec(memory_space=pl.ANY), row_spec,
                pl.BlockSpec((tm, TOP_K), lambda i, p0, p1: (i, 0)), mx(k_gate), mc(k_gate)]
    args = [pos0, pos1, ys, x1.reshape(t, d), gates, mod3, mod3]
    if next_norm is None:
        per_b, lat = n // tm, s_len // tm
        in_specs.append(pl.BlockSpec((1, d), lambda i, p0, p1: (0, 0)))
        args.append(final_g.reshape(1, d))
        out_specs = [pl.BlockSpec((tm, d), lambda i, p0, p1: ((i // per_b) * lat + jnp.minimum(i % per_b, lat - 1), 0))]
        out_shape = [jax.ShapeDtypeStruct((b * s_len, d), F32)]
    else:
        out_specs, out_shape = [row_spec], [jax.ShapeDtypeStruct((t, d), F32)]
    if next_norm is not None:
        g_next, mod_next = next_norm
        in_specs += [pl.BlockSpec((1, d), lambda i, p0, p1: (0, 0)), mx(0), mx(1), mc(0), mc(1)]
        args += [g_next.reshape(1, d), mod_next, mod_next, mod_next, mod_next]
        out_specs.append(row_spec)
        out_shape.append(jax.ShapeDtypeStruct((t, d), BF16))
    grid_spec = pltpu.PrefetchScalarGridSpec(
        num_scalar_prefetch=2,
        grid=(t // tm,),
        in_specs=in_specs,
        out_specs=out_specs,
        scratch_shapes=[pltpu.VMEM((2, tm, d), F32), pltpu.VMEM((2, tm, d), F32), pltpu.SemaphoreType.DMA((2,))],
    )
    outs = pl.pallas_call(
        functools.partial(_combine_kernel, s_len=s_len, n_len=n, tm=tm, with_next=next_norm is not None),
        grid_spec=grid_spec,
        out_shape=out_shape,
        compiler_params=_cparams(1, VMEM_LIMIT),
        name="combine",
    )(*args)
    if next_norm is None:
        return outs[0].reshape(b, s_len, d), None
    return outs[0].reshape(b, n, d), outs[1]


def _moe(x1, h2, ri, rf, cnt, xs0, mod3, wg, wu, wd, s_len, next_norm, final_g):
    b, n, d = x1.shape
    t = b * n
    counts = cnt[:, 0]
    n_blk = (counts + MOE_BLOCK - 1) // MOE_BLOCK
    blk_end = jnp.cumsum(n_blk)
    starts = (blk_end - n_blk) * MOE_BLOCK
    pos0 = (starts[ri[0]] + ri[2]).astype(jnp.int32)
    pos1 = (starts[ri[1]] + ri[3]).astype(jnp.int32)
    n_pad = xs0.shape[0]
    n_used = blk_end[-1:].astype(jnp.int32)
    blk_ids = jnp.minimum(jnp.arange(n_pad // MOE_BLOCK, dtype=jnp.int32), n_used[0] - 1)
    blk_expert = jnp.sum(blk_ids[:, None] >= blk_end[None, :], axis=1).astype(jnp.int32)
    xs = _dispatch_call(h2.reshape(t, d // 2), pos0, pos1, xs0)
    ys = _ffn_call(xs, blk_expert, n_used, wg, wu, wd)
    return _combine_call(ys, pos0, pos1, x1, rf[:TOP_K].T, mod3, s_len, 5, next_norm, final_g)


def kernel(x, c, ctx, c_ctx, w_ada, b_ada, g_norm1, g_norm2, w_in, w_out, na_rpb, da_lambda, da_subln_g,
           pool_w, pool_scale, sg_norm_g, sg_w, sg_b, router_w, router_b, exp_w_gate, exp_w_up, exp_w_down,
           g_final):
    b, s_len, d = x.shape
    c_len = ctx.shape[1]
    n = s_len + c_len
    depth = w_ada.shape[0]
    assert b + 1 <= 8 and w_in.shape[2] == ATT_COLS + REST_COLS and w_out.shape[1] == 4 * MIX_W

    cond = jnp.zeros((8, d), F32).at[:b].set(c).at[b].set(c_ctx)
    mod_all = _ada_all(cond, w_ada, b_ada)
    cos, sin = _rope_tables(s_len, c_len)
    w_out_bf = w_out.astype(BF16)
    mods = [mod_all[l].reshape(8, 1, N_MOD * d) for l in range(depth)]

    xc, h1 = _join_norm_call(x, ctx, g_norm1[0], mods[0])
    h1 = h1.reshape(b * n, d)
    for l in range(depth):
        mod3 = mods[l]
        pxa = _matmul_cols(h1, w_in, l, 0, ATT_COLS, BF16, 1024).reshape(b, n, ATT_COLS)
        pxb = _matmul_cols(h1, w_in, l, ATT_COLS, REST_COLS, F32, 768).reshape(b, n, REST_COLS)
        qt, kr, vt = _rope_call(pxa, cos, sin)
        mix_a = _na_call(pxa, na_rpb[l], s_len, c_len)
        mix_b, (wg, wu, wd) = _da_call(qt, kr, vt, da_lambda[l], da_subln_g[l],
                                       (exp_w_gate, exp_w_up, exp_w_down), s_len, c_len, l)
        mix_c = _pool_call(pxb, pool_w[l], pool_scale[l], s_len, c_len)
        mix_d = _sg_call(pxb, sg_norm_g[l], sg_w[l], sg_b[l])
        x1, h2, ri, rf, cnt, xs0 = _out_router_call((mix_a, mix_b, mix_c, mix_d), w_out_bf, l, xc, g_norm2[l], mod3,
                                               router_w, router_b, s_len)
        next_norm = (g_norm1[l + 1], mods[l + 1]) if l + 1 < depth else None
        xc, h1 = _moe(x1, h2, ri, rf, cnt, xs0, mod3, wg, wu, wd, s_len, next_norm, g_final)
    return xc
```

```python
import functools
import math

import numpy as np
import jax
import jax.numpy as jnp
from jax import lax
from jax.experimental import pallas as pl
from jax.experimental.pallas import tpu as pltpu

GRID_W = 64
NA_HEADS = 4
NA_DIM = 128
NA_WIN_H = 8
NA_WIN_W = 16
DA_HEADS = 4
DA_DIM = 64
DA_VDIM = 2 * DA_DIM
ROPE_BASE = 10000.0
POOL_GROUPS = 4
POOL_WIDTH = 128
POOL_WINDOWS = (2, 4, 8, 16)
SG_GROUPS = 4
SG_WIDTH = 128
SG_CHUNK = 128
N_EXPERT_GROUPS = 4
EXPERTS_PER_GROUP = 4
TOP_K = 2
N_MOD = 6
EPS = 1e-6
NEG_INF = -1e30
LOG2_E = math.log2(math.e)

HEAD_W = 128
MIX_W = 512
ATT_COLS = 6 * MIX_W
REST_COLS = 3 * MIX_W
MOE_BLOCK = 256
NA_Q_ROWS = 4
NA_HEADS_PER_STEP = 4
DA_HEADS_PER_STEP = 2
ROW_TILE = 256
DA_KEY_CHUNKS = (1408, 768, 512, 256, 128)
DA_VT_ROWS = DA_VDIM + 16
VMEM_LIMIT = 56 * 1024 * 1024

F32 = jnp.float32
BF16 = jnp.bfloat16


def _cparams(n_axes, vmem=None):
    return pltpu.CompilerParams(dimension_semantics=("arbitrary",) * n_axes,
                                vmem_limit_bytes=vmem)


def _pick(n, candidates):
    for c in candidates:
        if n % c == 0:
            return c
    raise ValueError(f"no tile in {candidates} divides {n}")


def _ada_kernel(c_ref, w_ref, b_ref, o_ref):
    cs = jax.nn.silu(c_ref[...]).astype(BF16)
    o_ref[...] = jnp.dot(cs, w_ref[...].astype(BF16), preferred_element_type=F32) + b_ref[...]


def _ada_all(cond, w_ada, b_ada):
    depth, d, n6 = w_ada.shape
    tn = _pick(n6, (1024, 512, 256, 128))
    return pl.pallas_call(
        _ada_kernel,
        grid=(depth, n6 // tn),
        in_specs=[pl.BlockSpec((8, d), lambda l, j: (0, 0)),
                  pl.BlockSpec((None, d, tn), lambda l, j: (l, 0, j)),
                  pl.BlockSpec((None, 1, tn), lambda l, j: (l, 0, j))],
        out_specs=pl.BlockSpec((None, 8, tn), lambda l, j: (l, 0, j)),
        out_shape=jax.ShapeDtypeStruct((depth, 8, n6), F32),
        compiler_params=_cparams(2, VMEM_LIMIT),
        name="adaln",
    )(cond, w_ada, b_ada.reshape(depth, 1, n6))


def _row_is_ctx(i, tm, s_len):
    row = i * tm + lax.broadcasted_iota(jnp.int32, (tm, 1), 0)
    return row >= s_len


def _norm_mod(x, g, shx, scx, shc, scc, is_ctx):
    y = x * lax.rsqrt(jnp.mean(x * x, axis=-1, keepdims=True) + EPS) * g
    shift = jnp.where(is_ctx, shc, shx)
    scale = jnp.where(is_ctx, scc, scx)
    return y * (1.0 + scale) + shift


def _join_norm_kernel(x_ref, c_ref, g_ref, shx_ref, scx_ref, shc_ref, scc_ref, xc_ref, h_ref, *, s_len, tm):
    i = pl.program_id(1)
    is_ctx = _row_is_ctx(i, tm, s_len)
    x = jnp.where(i < s_len // tm, x_ref[...], c_ref[...])
    xc_ref[...] = x
    h = _norm_mod(x, g_ref[...], shx_ref[...], scx_ref[...], shc_ref[...], scc_ref[...], is_ctx)
    h_ref[...] = h.astype(h_ref.dtype)


def _join_norm_call(x, ctx, g, mod3):
    b, s_len, d = x.shape
    c_len = ctx.shape[1]
    n = s_len + c_len
    tm = _pick(math.gcd(s_len, c_len), (ROW_TILE, 128))
    n_lat = s_len // tm
    mx = lambda k: pl.BlockSpec((None, 1, d), lambda bi, i: (bi, 0, k))
    mc = lambda k: pl.BlockSpec((None, 1, d), lambda bi, i: (b, 0, k))
    row_spec = pl.BlockSpec((None, tm, d), lambda bi, i: (bi, i, 0))
    return pl.pallas_call(
        functools.partial(_join_norm_kernel, s_len=s_len, tm=tm),
        grid=(b, n // tm),
        in_specs=[pl.BlockSpec((None, tm, d), lambda bi, i: (bi, jnp.minimum(i, n_lat - 1), 0)),
                  pl.BlockSpec((None, tm, d), lambda bi, i: (bi, jnp.maximum(i - n_lat, 0), 0)),
                  pl.BlockSpec((1, d), lambda bi, i: (0, 0)),
                  mx(0), mx(1), mc(0), mc(1)],
        out_specs=[row_spec, row_spec],
        out_shape=[jax.ShapeDtypeStruct((b, n, d), F32), jax.ShapeDtypeStruct((b, n, d), BF16)],
        compiler_params=_cparams(2, VMEM_LIMIT),
        name="join_norm",
    )(x, ctx, g.reshape(1, d), mod3, mod3, mod3, mod3)


def _mm_kernel(a_ref, w_ref, o_ref, wbf_ref):
    @pl.when(pl.program_id(1) == 0)
    def _():
        wbf_ref[...] = w_ref[...].astype(BF16)

    o_ref[...] = jnp.dot(a_ref[...], wbf_ref[...], preferred_element_type=F32).astype(o_ref.dtype)


def _matmul_cols(a, w, layer, col0, ncols, out_dtype, tn):
    m, k = a.shape
    tm = _pick(m, (1536, 768, 512, 384, 256, 128))
    off = col0 // tn
    return pl.pallas_call(
        _mm_kernel,
        grid=(ncols // tn, m // tm),
        in_specs=[pl.BlockSpec((tm, k), lambda j, i: (i, 0)),
                  pl.BlockSpec((None, k, tn), lambda j, i: (layer, 0, j + off))],
        out_specs=pl.BlockSpec((tm, tn), lambda j, i: (i, j)),
        out_shape=jax.ShapeDtypeStruct((m, ncols), out_dtype),
        scratch_shapes=[pltpu.VMEM((k, tn), BF16)],
        compiler_params=_cparams(2, VMEM_LIMIT),
        name="in_proj",
    )(a, w)


def _rope_tables(s_len, c_len):
    t = jnp.arange(s_len)
    row = (t // GRID_W).astype(F32)
    col = (t % GRID_W).astype(F32)
    n_freq = DA_DIM // 4
    inv_freq = 1.0 / (ROPE_BASE ** (jnp.arange(n_freq, dtype=F32) / n_freq))
    ang = jnp.concatenate([row[:, None] * inv_freq, col[:, None] * inv_freq], axis=-1)
    cos = jnp.concatenate([jnp.cos(ang), jnp.ones((c_len, DA_DIM // 2), F32)], axis=0)
    sin = jnp.concatenate([jnp.sin(ang), jnp.zeros((c_len, DA_DIM // 2), F32)], axis=0)
    return jnp.tile(cos, (1, 4)), jnp.tile(jnp.concatenate([-sin, sin], axis=-1), (1, 2))


def _rope_kernel(q_ref, k_ref, v_ref, c_ref, s_ref, qt_ref, kr_ref, vt_ref, *, tm, q_scale):
    cos = c_ref[...]
    sin = s_ref[...]
    lane = lax.broadcasted_iota(jnp.int32, (tm, HEAD_W), 1)
    first_half = (lane % DA_DIM) < (DA_DIM // 2)

    def rotate(x):
        swapped = jnp.where(first_half, pltpu.roll(x, HEAD_W - DA_DIM // 2, 1), pltpu.roll(x, DA_DIM // 2, 1))
        return x * cos + swapped * sin

    for h in range(DA_HEADS):
        cols = slice(h * HEAD_W, (h + 1) * HEAD_W)
        q = rotate(q_ref[:, cols].astype(F32)) * q_scale
        qt_ref[cols, :] = q.T.astype(qt_ref.dtype)
        kr_ref[:, cols] = rotate(k_ref[:, cols].astype(F32)).astype(kr_ref.dtype)
        vt_ref[h * DA_VT_ROWS:h * DA_VT_ROWS + DA_VDIM, :] = v_ref[:, cols].astype(F32).T.astype(vt_ref.dtype)
        pad = lax.broadcasted_iota(jnp.int32, (DA_VT_ROWS - DA_VDIM, tm), 0)
        vt_ref[h * DA_VT_ROWS + DA_VDIM:(h + 1) * DA_VT_ROWS, :] = jnp.where(pad == 0, 1.0, 0.0).astype(vt_ref.dtype)


def _rope_call(pxa, cos, sin):
    b, n, _ = pxa.shape
    tm = _pick(n, (ROW_TILE, 128))
    col_spec = lambda k: pl.BlockSpec((None, tm, MIX_W), lambda bi, i: (bi, i, k))
    t_spec = lambda rows: pl.BlockSpec((None, rows, tm), lambda bi, i: (bi, 0, i))
    return pl.pallas_call(
        functools.partial(_rope_kernel, tm=tm, q_scale=DA_DIM ** -0.5 * LOG2_E),
        grid=(b, n // tm),
        in_specs=[col_spec(3), col_spec(4), col_spec(5),
                  pl.BlockSpec((tm, HEAD_W), lambda bi, i: (i, 0)),
                  pl.BlockSpec((tm, HEAD_W), lambda bi, i: (i, 0))],
        out_specs=[t_spec(MIX_W), col_spec(0), t_spec(DA_HEADS * DA_VT_ROWS)],
        out_shape=[jax.ShapeDtypeStruct((b, MIX_W, n), BF16),
                   jax.ShapeDtypeStruct((b, n, MIX_W), BF16),
                   jax.ShapeDtypeStruct((b, DA_HEADS * DA_VT_ROWS, n), BF16)],
        compiler_params=_cparams(2, VMEM_LIMIT),
        name="rope",
    )(pxa, pxa, pxa, cos, sin)


def _na_plan(rows, r_q):
    kh = min(NA_WIN_H, rows)
    key_rows = r_q + kh - 1
    assert rows % r_q == 0 and rows >= key_rows
    patterns, var, ustart = [], [], []
    for j in range(rows // r_q):
        r = r_q * j + np.arange(r_q)
        rs = np.clip(r - kh // 2, 0, rows - kh)
        u = int(np.clip(rs.min(), 0, rows - key_rows))
        kr = u + np.arange(key_rows)
        valid = (kr[None, :] >= rs[:, None]) & (kr[None, :] < rs[:, None] + kh)
        assert (valid.sum(1) == kh).all()
        dr0 = u - r + (NA_WIN_H - 1)
        key = (valid.tobytes(), dr0.tobytes())
        for v, (k2, _, _) in enumerate(patterns):
            if k2 == key:
                break
        else:
            v = len(patterns)
            patterns.append((key, valid, dr0))
        var.append(v)
        ustart.append(u * GRID_W)
    valid = np.stack([p[1] for p in patterns])
    dr0 = np.stack([p[2] for p in patterns])
    return np.asarray(var, np.int32), np.asarray(ustart, np.int32), valid, dr0


def _na_bias_tables(rpb, valid, dr0):
    col = np.arange(GRID_W)
    cs = np.clip(col - NA_WIN_W // 2, 0, GRID_W - NA_WIN_W)
    cmask = (col[None, :] >= cs[:, None]) & (col[None, :] < cs[:, None] + NA_WIN_W)
    dc = np.clip(col[None, :] - col[:, None], -(NA_WIN_W - 1), NA_WIN_W - 1) + (NA_WIN_W - 1)
    heads, n_dr, n_dc = rpb.shape
    v, r_q, key_rows = valid.shape
    pick = (dc.reshape(1, -1) == np.arange(n_dc)[:, None]).astype(np.float32)
    by_col = jnp.dot(rpb.astype(F32).reshape(heads * n_dr, n_dc), pick,
                     precision=lax.Precision.HIGHEST).reshape(heads, n_dr, GRID_W, GRID_W)
    padded = jnp.pad(by_col, ((0, 0), (key_rows, key_rows), (0, 0), (0, 0)))
    tab = jnp.stack([lax.slice_in_dim(padded, int(d) + key_rows, int(d) + 2 * key_rows, axis=1)
                     for d in dr0.reshape(-1)], axis=1)
    tab = tab.reshape(heads, v, r_q, key_rows, GRID_W, GRID_W).transpose(0, 1, 2, 4, 3, 5)
    mask = valid[:, :, None, :, None] & cmask[None, None, :, None, :]
    tab = jnp.where(mask[None], tab, NEG_INF)
    return tab.reshape(heads, v, r_q * GRID_W, key_rows * GRID_W)


def _na_kernel(var_ref, u_ref, q_ref, k_ref, v_ref, kc_ref, vc_ref, bias_ref, o_ref, *, n_lat, kw, scale):
    j = pl.program_id(2)
    nt = (((1,), (1,)), ((), ()))

    def one_head(hh, local):
        cols = slice(hh * HEAD_W, (hh + 1) * HEAD_W)
        q = q_ref[:, cols]
        vc = vc_ref[:, cols]
        s_ctx = lax.dot_general(q, kc_ref[:, cols], nt, preferred_element_type=F32) * scale
        m = jnp.max(s_ctx, axis=-1, keepdims=True)
        if local:
            u = pl.multiple_of(u_ref[j], GRID_W)
            s = lax.dot_general(q, k_ref[pl.ds(u, kw), cols], nt, preferred_element_type=F32) * scale + bias_ref[hh]
            m = jnp.maximum(m, jnp.max(s, axis=-1, keepdims=True))
            p = jnp.exp(s - m)
        pc = jnp.exp(s_ctx - m)
        l = jnp.sum(pc, axis=-1, keepdims=True)
        o = jnp.dot(pc.astype(BF16), vc, preferred_element_type=F32)
        if local:
            l += jnp.sum(p, axis=-1, keepdims=True)
            o += jnp.dot(p.astype(BF16), v_ref[pl.ds(u, kw), cols], preferred_element_type=F32)
        o_ref[:, cols] = (o / l).astype(o_ref.dtype)

    @pl.when(j < n_lat)
    def _():
        for hh in range(NA_HEADS_PER_STEP):
            one_head(hh, True)

    @pl.when(j >= n_lat)
    def _():
        for hh in range(NA_HEADS_PER_STEP):
            one_head(hh, False)


def _na_call(pxa, rpb, s_len, c_len):
    b, n, _ = pxa.shape
    rows = s_len // GRID_W
    tq = NA_Q_ROWS * GRID_W
    hps = NA_HEADS_PER_STEP
    assert s_len % tq == 0 and c_len % tq == 0 and s_len % c_len == 0 and NA_HEADS % hps == 0
    var, ustart, valid, dr0 = _na_plan(rows, NA_Q_ROWS)
    n_lat, n_q = s_len // tq, n // tq
    kw = valid.shape[2] * GRID_W
    tabs = _na_bias_tables(rpb, valid, dr0)
    pad = np.zeros(n_q - n_lat, np.int32)
    var = jnp.asarray(np.concatenate([var, pad]))
    ustart = jnp.asarray(np.concatenate([ustart, pad]))
    ctx_blk = s_len // c_len
    wide = hps * HEAD_W
    k_blk, v_blk = NA_HEADS // hps, 2 * NA_HEADS // hps
    grid_spec = pltpu.PrefetchScalarGridSpec(
        num_scalar_prefetch=2,
        grid=(b, NA_HEADS // hps, n_q),
        in_specs=[pl.BlockSpec((None, tq, wide), lambda bi, h, j, vr, ur: (bi, j, h)),
                  pl.BlockSpec((None, s_len, wide), lambda bi, h, j, vr, ur: (bi, 0, k_blk + h)),
                  pl.BlockSpec((None, s_len, wide), lambda bi, h, j, vr, ur: (bi, 0, v_blk + h)),
                  pl.BlockSpec((None, c_len, wide), lambda bi, h, j, vr, ur: (bi, ctx_blk, k_blk + h)),
                  pl.BlockSpec((None, c_len, wide), lambda bi, h, j, vr, ur: (bi, ctx_blk, v_blk + h)),
                  pl.BlockSpec((hps, None, tq, kw), lambda bi, h, j, vr, ur: (h, vr[j], 0, 0))],
        out_specs=pl.BlockSpec((None, tq, wide), lambda bi, h, j, vr, ur: (bi, j, h)),
    )
    return pl.pallas_call(
        functools.partial(_na_kernel, n_lat=n_lat, kw=kw, scale=NA_DIM ** -0.5),
        grid_spec=grid_spec,
        out_shape=jax.ShapeDtypeStruct((b, n, MIX_W), BF16),
        compiler_params=_cparams(3, VMEM_LIMIT),
        name="nbr_attn",
    )(var, ustart, pxa, pxa, pxa, pxa, pxa, tabs)


def _cast_expert_weights(step, n_steps, layer, srcs, dsts, in_g, in_u, in_d, out_g, out_u, out_d, sem_in, sem_out):
    ins, outs = (in_g, in_u, in_d), (out_g, out_u, out_d)
    n_chunks = srcs[0].shape[1] // in_g.shape[1]
    cur = jnp.minimum(step, n_chunks - 1)
    nxt = jnp.minimum(step + 1, n_chunks - 1)
    slot = step % 2
    other = 1 - slot

    def in_copy(t, chunk, sl):
        rows = ins[t].shape[1]
        src = srcs[t].at[layer, pl.ds(pl.multiple_of(chunk * rows, rows), rows)]
        return pltpu.make_async_copy(src, ins[t].at[sl], sem_in.at[sl])

    def out_copy(t, chunk, sl):
        rows = outs[t].shape[1]
        dst = dsts[t].at[pl.ds(pl.multiple_of(chunk * rows, rows), rows)]
        return pltpu.make_async_copy(outs[t].at[sl], dst, sem_out.at[sl])

    @pl.when(step == 0)
    def _():
        for t in range(3):
            in_copy(t, cur, slot).start()

    for t in range(3):
        in_copy(t, nxt, other).start()
    for t in range(3):
        in_copy(t, cur, slot).wait()

    @pl.when(step > 0)
    def _():
        for t in range(3):
            out_copy(t, cur, other).wait()

    for t in range(3):
        outs[t][slot] = ins[t][slot].astype(BF16)
    for t in range(3):
        out_copy(t, cur, slot).start()

    @pl.when(step == n_steps - 1)
    def _():
        for t in range(3):
            out_copy(t, cur, slot).wait()
            in_copy(t, nxt, other).wait()


def _zero_fill(step, n_steps, dst_ref, zero_ref, sem):
    rows = zero_ref.shape[0]
    chunk = jnp.minimum(step, dst_ref.shape[0] // rows - 1)
    fill = pltpu.make_async_copy(zero_ref, dst_ref.at[pl.ds(pl.multiple_of(chunk * rows, rows), rows)], sem)

    @pl.when(step == 0)
    def _():
        zero_ref[...] = jnp.zeros_like(zero_ref)

    @pl.when(step > 0)
    def _():
        fill.wait()

    fill.start()

    @pl.when(step == n_steps - 1)
    def _():
        fill.wait()


def _da_kernel(qt_ref, k_ref, vt_ref, lam_ref, g_ref, wg_ref, wu_ref, wd_ref, o_ref, wgb_ref, wub_ref, wdb_ref,
               xs0_ref, *scratch, s_len, c_len, tq, tk, lam_init, heads, layer):
    score_refs, cast_refs, fill_refs = scratch[:2 * heads], scratch[2 * heads:-2], scratch[-2:]
    sa_refs, sb_refs = score_refs[0::2], score_refs[1::2]
    qi = pl.program_id(2)
    step = (pl.program_id(0) * pl.num_programs(1) + pl.program_id(1)) * pl.num_programs(2) + qi
    n_steps = pl.num_programs(0) * pl.num_programs(1) * pl.num_programs(2)
    _cast_expert_weights(step, n_steps, layer, (wg_ref, wu_ref, wd_ref), (wgb_ref, wub_ref, wdb_ref), *cast_refs)
    _zero_fill(step, n_steps, xs0_ref, *fill_refs)
    n = s_len + c_len
    feat = lax.broadcasted_iota(jnp.int32, (2 * DA_DIM, tq), 0)

    def q_tiles(idx):
        start = pl.multiple_of(idx * tq, tq)
        out = []
        for hh in range(heads):
            qt = qt_ref[hh * HEAD_W:(hh + 1) * HEAD_W, pl.ds(start, tq)]
            zero = jnp.zeros_like(qt)
            out.append(jnp.concatenate([jnp.where(feat < DA_DIM, qt, zero), jnp.where(feat >= DA_DIM, qt, zero)],
                                       axis=1))
        return out

    q2 = q_tiles(qi)

    def k_rows(hh, rows):
        return k_ref[rows, hh * HEAD_W:(hh + 1) * HEAD_W]

    def vt_cols(hh, cols):
        return vt_ref[hh * DA_VT_ROWS:(hh + 1) * DA_VT_ROWS, cols]

    def scores(k_blk, q):
        return jnp.dot(k_blk, q, preferred_element_type=F32)

    def update(s, vt_blk, carry):
        m, acc = carry
        m_new = jnp.maximum(m, jnp.max(s, axis=0, keepdims=True))
        alpha = jnp.exp2(m - m_new)
        p = jnp.exp2(s - m_new)
        acc_new = alpha * acc + jnp.dot(vt_blk, p.astype(BF16), preferred_element_type=F32)
        return m_new, acc_new

    def init():
        one = (jnp.full((1, 2 * tq), NEG_INF, F32), jnp.zeros((DA_VT_ROWS, 2 * tq), F32))
        return tuple(one for _ in range(heads))

    def finalize(stats):
        lv = lam_ref[...]
        t1 = jnp.sum(lv[0:1, :] * lv[1:2, :], axis=-1, keepdims=True)
        t2 = jnp.sum(lv[2:3, :] * lv[3:4, :], axis=-1, keepdims=True)
        lam = jnp.exp(t1) - jnp.exp(t2) + lam_init
        for hh in range(heads):
            _, acc = stats[hh]
            o = acc[:DA_VDIM, :] / acc[DA_VDIM:DA_VDIM + 1, :]
            o = o[:, :tq] - lam * o[:, tq:]
            y = o * lax.rsqrt(jnp.mean(o * o, axis=0, keepdims=True) + EPS) * g_ref[...]
            o_ref[:, hh * HEAD_W:(hh + 1) * HEAD_W] = (y * (1.0 - lam_init)).T.astype(o_ref.dtype)

    k_steps = n // tk
    full_pairs = (k_steps - 1) // 2
    carry_first = k_steps % 2 == 0

    @pl.when(qi < s_len // tq)
    def _():
        def scores_into(dst_refs, step, qs):
            rows = pl.ds(pl.multiple_of(step * tk, tk), tk)
            for hh in range(heads):
                dst_refs[hh][...] = scores(k_rows(hh, rows), qs[hh])

        def update_from(src_refs, step, stats):
            cols = pl.ds(pl.multiple_of(step * tk, tk), tk)
            return tuple(update(src_refs[hh][...], vt_cols(hh, cols), stats[hh]) for hh in range(heads))

        def pair(j, stats):
            c0 = 2 * j
            scores_into(sb_refs, c0 + 1, q2)
            stats = update_from(sa_refs, c0, stats)
            scores_into(sa_refs, c0 + 2, q2)
            return update_from(sb_refs, c0 + 1, stats)

        if carry_first:
            @pl.when(qi == 0)
            def _():
                scores_into(sa_refs, 0, q2)
        else:
            scores_into(sa_refs, 0, q2)
        stats = lax.fori_loop(0, full_pairs, pair, init())
        c0 = 2 * full_pairs
        if carry_first:
            scores_into(sb_refs, c0 + 1, q2)
            stats = update_from(sa_refs, c0, stats)
            scores_into(sa_refs, 0, q_tiles(qi + 1))
            stats = update_from(sb_refs, c0 + 1, stats)
        else:
            stats = update_from(sa_refs, c0, stats)
        finalize(stats)

    @pl.when(qi >= s_len // tq)
    def _():
        rows, stats = slice(s_len, n), init()
        finalize(tuple(update(scores(k_rows(hh, rows), q2[hh]), vt_cols(hh, rows), stats[hh])
                       for hh in range(heads)))


def _da_call(qt, kr, vt, lam_vecs, subln_g, exp_w, n_dispatch, s_len, c_len, layer):
    b, n, _ = kr.shape
    tq = _pick(math.gcd(s_len, c_len), (256, 128))
    tk = _pick(n, DA_KEY_CHUNKS)
    hps = DA_HEADS_PER_STEP
    assert DA_HEADS % hps == 0
    lam_init = 0.8 - 0.6 * math.exp(-0.3 * layer)
    grid = (b, DA_HEADS // hps, n // tq)
    n_chunks = 1 << (math.prod(grid).bit_length() - 1)
    flat = [w.reshape(w.shape[0], w.shape[1] * w.shape[2], w.shape[3]) for w in exp_w]
    chunk = [(w.shape[1] // n_chunks, w.shape[2]) for w in flat]
    assert all(w.shape[1] % n_chunks == 0 and r % 16 == 0 for w, (r, _) in zip(flat, chunk))
    half = exp_w[0].shape[2] // 2
    fill_rows = n_dispatch // n_chunks
    assert n_dispatch % n_chunks == 0 and fill_rows % 8 == 0
    once = pl.Buffered(1)
    any_spec = pl.BlockSpec(memory_space=pl.ANY)
    outs = pl.pallas_call(
        functools.partial(_da_kernel, s_len=s_len, c_len=c_len, tq=tq, tk=tk, lam_init=lam_init, heads=hps,
                          layer=layer),
        grid=grid,
        in_specs=[pl.BlockSpec((None, hps * HEAD_W, n), lambda bi, h, i: (bi, h, 0), pipeline_mode=once),
                  pl.BlockSpec((None, n, hps * HEAD_W), lambda bi, h, i: (bi, 0, h), pipeline_mode=once),
                  pl.BlockSpec((None, hps * DA_VT_ROWS, n), lambda bi, h, i: (bi, h, 0), pipeline_mode=once),
                  pl.BlockSpec((4, DA_DIM), lambda bi, h, i: (0, 0)),
                  pl.BlockSpec((DA_VDIM, 1), lambda bi, h, i: (0, 0)),
                  any_spec, any_spec, any_spec],
        out_specs=[pl.BlockSpec((None, tq, hps * HEAD_W), lambda bi, h, i: (bi, i, h)),
                   any_spec, any_spec, any_spec, any_spec],
        out_shape=[jax.ShapeDtypeStruct((b, n, MIX_W), BF16)]
                  + [jax.ShapeDtypeStruct(w.shape[1:], BF16) for w in flat]
                  + [jax.ShapeDtypeStruct((n_dispatch, half), jnp.uint32)],
        scratch_shapes=[pltpu.VMEM((tk, 2 * tq), F32)] * (2 * hps)
                       + [pltpu.VMEM((2,) + c, F32) for c in chunk]
                       + [pltpu.VMEM((2,) + c, BF16) for c in chunk]
                       + [pltpu.SemaphoreType.DMA((2,)), pltpu.SemaphoreType.DMA((2,))]
                       + [pltpu.VMEM((fill_rows, half), jnp.uint32), pltpu.SemaphoreType.DMA],
        compiler_params=_cparams(3, VMEM_LIMIT),
        name="diff_attn",
    )(qt, kr, vt, lam_vecs, subln_g.reshape(DA_VDIM, 1), *flat)
    return outs[0], tuple(o.reshape(w.shape[1:]) for o, w in zip(outs[1:4], exp_w)), outs[4]


POOL_HALO = 8


def _pool_kernel(prev_ref, cur_ref, next_ref, w_ref, sc_ref, o_ref, buf_ref, *, s_len, c_len, tm):
    i = pl.program_id(1)
    n_lat = s_len // tm
    n_all = (s_len + c_len) // tm
    seq_start = (i == 0) | (i == n_lat)
    seq_end = (i == n_lat - 1) | (i == n_all - 1)
    buf_ref[0:POOL_HALO, :] = jnp.where(seq_start, 0.0, prev_ref[...])
    buf_ref[POOL_HALO:POOL_HALO + tm, :] = cur_ref[...]
    buf_ref[POOL_HALO + tm:, :] = jnp.where(seq_end, 0.0, next_ref[...])
    in_ctx = i >= n_lat
    seq_len = jnp.where(in_ctx, c_len, s_len)
    pos = i * tm - jnp.where(in_ctx, s_len, 0) + lax.broadcasted_iota(jnp.int32, (tm, 1), 0)
    for g, win in enumerate(POOL_WINDOWS):
        half = win // 2
        cols = slice(g * POOL_WIDTH, (g + 1) * POOL_WIDTH)
        tot = buf_ref[POOL_HALO - half:POOL_HALO - half + tm, cols]
        for d in range(-half + 1, half):
            tot = tot + buf_ref[POOL_HALO + d:POOL_HALO + d + tm, cols]
        cnt = jnp.clip(pos + half, 0, seq_len) - jnp.clip(pos - half, 0, seq_len)
        resid = tot / cnt.astype(F32) - cur_ref[:, cols]
        y = jnp.dot(resid.astype(BF16), w_ref[g].astype(BF16), preferred_element_type=F32)
        o_ref[:, cols] = (y * sc_ref[:, cols]).astype(o_ref.dtype)


def _pool_call(pxb, pool_w, pool_scale, s_len, c_len):
    b, n, _ = pxb.shape
    tm = _pick(math.gcd(s_len, c_len), (ROW_TILE, 128))
    hb = tm // POOL_HALO
    last = n // POOL_HALO - 1
    return pl.pallas_call(
        functools.partial(_pool_kernel, s_len=s_len, c_len=c_len, tm=tm),
        grid=(b, n // tm),
        in_specs=[pl.BlockSpec((None, POOL_HALO, MIX_W), lambda bi, i: (bi, jnp.maximum(i * hb - 1, 0), 0)),
                  pl.BlockSpec((None, tm, MIX_W), lambda bi, i: (bi, i, 0)),
                  pl.BlockSpec((None, POOL_HALO, MIX_W), lambda bi, i: (bi, jnp.minimum((i + 1) * hb, last), 0)),
                  pl.BlockSpec((POOL_GROUPS, POOL_WIDTH, POOL_WIDTH), lambda bi, i: (0, 0, 0)),
                  pl.BlockSpec((1, MIX_W), lambda bi, i: (0, 0))],
        out_specs=pl.BlockSpec((None, tm, MIX_W), lambda bi, i: (bi, i, 0)),
        out_shape=jax.ShapeDtypeStruct((b, n, MIX_W), BF16),
        scratch_shapes=[pltpu.VMEM((tm + 2 * POOL_HALO, MIX_W), F32)],
        compiler_params=_cparams(2, VMEM_LIMIT),
        name="pool",
    )(pxb, pxb, pxb, pool_w, pool_scale.reshape(1, MIX_W))


def _sg_kernel(u_ref, v_ref, g_ref, w_ref, b_ref, o_ref, *, tm):
    v = jax.nn.gelu(v_ref[...])
    vn = (v * lax.rsqrt(jnp.mean(v * v, axis=-1, keepdims=True) + EPS) * g_ref[...]).astype(BF16)
    bias = b_ref[...]
    for c in range(tm // SG_CHUNK):
        rows = slice(c * SG_CHUNK, (c + 1) * SG_CHUNK)
        for g in range(SG_GROUPS):
            cols = slice(g * SG_WIDTH, (g + 1) * SG_WIDTH)
            mixed = jnp.dot(w_ref[g].astype(BF16), vn[rows, cols], preferred_element_type=F32)
            mixed = mixed + bias[:, g:g + 1]
            o_ref[rows, cols] = (jax.nn.gelu(u_ref[rows, cols]) * mixed).astype(o_ref.dtype)


def _sg_call(pxb, sg_norm_g, sg_w, sg_b):
    b, n, _ = pxb.shape
    tm = _pick(n, (ROW_TILE, 128))
    return pl.pallas_call(
        functools.partial(_sg_kernel, tm=tm),
        grid=(b, n // tm),
        in_specs=[pl.BlockSpec((None, tm, MIX_W), lambda bi, i: (bi, i, 1)),
                  pl.BlockSpec((None, tm, MIX_W), lambda bi, i: (bi, i, 2)),
                  pl.BlockSpec((1, MIX_W), lambda bi, i: (0, 0)),
                  pl.BlockSpec((SG_GROUPS, SG_CHUNK, SG_CHUNK), lambda bi, i: (0, 0, 0)),
                  pl.BlockSpec((SG_CHUNK, SG_GROUPS), lambda bi, i: (0, 0))],
        out_specs=pl.BlockSpec((None, tm, MIX_W), lambda bi, i: (bi, i, 0)),
        out_shape=jax.ShapeDtypeStruct((b, n, MIX_W), BF16),
        compiler_params=_cparams(2, VMEM_LIMIT),
        name="spatial_gate",
    )(pxb, pxb, sg_norm_g.reshape(1, MIX_W), sg_w, sg_b.T)


def _pack_bf16_pairs(x):
    w = x.shape[1] // 2
    hi = lax.bitcast_convert_type(x[:, :w].astype(BF16).astype(F32), jnp.uint32)
    lo = lax.bitcast_convert_type(x[:, w:].astype(BF16).astype(F32), jnp.uint32)
    return hi | (lo >> 16)


def _unpack_bf16_pairs(u):
    hi = lax.bitcast_convert_type(u & jnp.uint32(0xFFFF0000), F32).astype(BF16)
    lo = lax.bitcast_convert_type(u << 16, F32).astype(BF16)
    return jnp.concatenate([hi, lo], axis=1)


def _first_argmax(vals):
    best, idx = vals[0], jnp.zeros(vals[0].shape, jnp.int32)
    for k in range(1, len(vals)):
        take = vals[k] > best
        best = jnp.where(take, vals[k], best)
        idx = jnp.where(take, k, idx)
    return idx, best


def _out_router_kernel(ma_ref, mb_ref, mc_ref, md_ref, w_ref, x_ref, gx_ref, gc_ref,
                       g_ref, shx_ref, scx_ref, shc_ref, scc_ref, rw_ref, rb_ref,
                       x1_ref, h_ref, ri_ref, rf_ref, cnt_ref, run_ref, *, s_len, tm, n_exp):
    bi, i = pl.program_id(0), pl.program_id(1)

    @pl.when((bi == 0) & (i == 0))
    def _():
        run_ref[...] = jnp.zeros_like(run_ref)

    mix = jnp.concatenate([ma_ref[...], mb_ref[...], mc_ref[...], md_ref[...]], axis=1)
    acc = jnp.dot(mix, w_ref[...], preferred_element_type=F32)
    is_ctx = _row_is_ctx(i, tm, s_len)
    x1 = x_ref[...] + jnp.where(is_ctx, gc_ref[...], gx_ref[...]) * acc
    x1_ref[...] = x1
    h = _norm_mod(x1, g_ref[...], shx_ref[...], scx_ref[...], shc_ref[...], scc_ref[...], is_ctx)
    h_ref[...] = _pack_bf16_pairs(h)

    h_hi = h.astype(BF16)
    h_lo = (h - h_hi.astype(F32)).astype(BF16)
    part = jnp.dot(h_hi, rw_ref[0], preferred_element_type=F32)
    part += jnp.dot(h_lo, rw_ref[1], preferred_element_type=F32)
    part_t = part.T
    logits = part_t[0:n_exp, :] + part_t[n_exp:2 * n_exp, :]
    scores = jax.nn.sigmoid(logits)
    biased = scores + rb_ref[...]
    b_rows = [biased[e:e + 1, :] for e in range(n_exp)]
    s_rows = [scores[e:e + 1, :] for e in range(n_exp)]
    epg = EXPERTS_PER_GROUP
    group_scores = []
    for g in range(N_EXPERT_GROUPS):
        r = b_rows[g * epg:(g + 1) * epg]
        pair = None
        for a in range(epg):
            for c in range(a + 1, epg):
                t = r[a] + r[c]
                pair = t if pair is None else jnp.maximum(pair, t)
        group_scores.append(pair)
    group, _ = _first_argmax(group_scores)
    in_group = []
    for k in range(epg):
        sel = b_rows[k]
        for g in range(1, N_EXPERT_GROUPS):
            sel = jnp.where(group == g, b_rows[g * epg + k], sel)
        in_group.append(sel)
    loc0, _ = _first_argmax(in_group)
    rest = [jnp.where(loc0 == k, -jnp.inf, in_group[k]) for k in range(epg)]
    loc1, _ = _first_argmax(rest)
    e0 = group * epg + loc0
    e1 = group * epg + loc1
    g0 = jnp.zeros_like(s_rows[0])
    g1 = jnp.zeros_like(s_rows[0])
    for e in range(n_exp):
        g0 = jnp.where(e0 == e, s_rows[e], g0)
        g1 = jnp.where(e1 == e, s_rows[e], g1)
    tot = g0 + g1

    eidx = lax.broadcasted_iota(jnp.int32, (n_exp, tm), 0)
    hit0 = eidx == e0
    hit1 = eidx == e1
    onehot = jnp.where(hit0 | hit1, 1.0, 0.0)
    before = lax.broadcasted_iota(jnp.int32, (tm, tm), 0) < lax.broadcasted_iota(jnp.int32, (tm, tm), 1)
    tri = jnp.where(before, 1.0, 0.0).astype(BF16)
    run = run_ref[:, 0:1]
    rank = jnp.dot(onehot.astype(BF16), tri, preferred_element_type=F32) + run
    rank0 = jnp.sum(jnp.where(hit0, rank, 0.0), axis=0, keepdims=True)
    rank1 = jnp.sum(jnp.where(hit1, rank, 0.0), axis=0, keepdims=True)
    new_run = run + jnp.sum(onehot, axis=1, keepdims=True)
    run_ref[...] = jnp.broadcast_to(new_run, run_ref.shape)
    cnt_ref[...] = jnp.broadcast_to(new_run, cnt_ref.shape).astype(jnp.int32)

    zi = jnp.zeros((4, tm), jnp.int32)
    ri_ref[...] = jnp.concatenate([e0, e1, rank0.astype(jnp.int32), rank1.astype(jnp.int32), zi], axis=0)
    zf = jnp.zeros((6, tm), F32)
    rf_ref[...] = jnp.concatenate([g0 / tot, g1 / tot, zf], axis=0)


def _router_slabs(router_w):
    d, n_exp = router_w.shape
    w_hi = router_w.astype(BF16)
    w_lo = (router_w - w_hi.astype(F32)).astype(BF16)
    zeros = lambda cols: jnp.zeros((d, cols), BF16)
    return jnp.stack([jnp.concatenate([w_hi, w_lo, zeros(HEAD_W - 2 * n_exp)], axis=1),
                      jnp.concatenate([w_hi, zeros(HEAD_W - n_exp)], axis=1)])


def _out_router_call(mixes, w_out_bf, layer, xc, g2, mod3, router_w, router_b, s_len):
    b, n, d = xc.shape
    n_exp = router_w.shape[1]
    tm = _pick(n, (ROW_TILE, 128))
    per_b = n // tm
    mx = lambda k: pl.BlockSpec((None, 1, d), lambda bi, i: (bi, 0, k))
    mc = lambda k: pl.BlockSpec((None, 1, d), lambda bi, i: (b, 0, k))
    mix_spec = pl.BlockSpec((None, tm, MIX_W), lambda bi, i: (bi, i, 0))
    row_spec = pl.BlockSpec((None, tm, d), lambda bi, i: (bi, i, 0))
    tok_spec = pl.BlockSpec((8, tm), lambda bi, i: (0, bi * per_b + i))
    return pl.pallas_call(
        functools.partial(_out_router_kernel, s_len=s_len, tm=tm, n_exp=n_exp),
        grid=(b, per_b),
        in_specs=[mix_spec, mix_spec, mix_spec, mix_spec,
                  pl.BlockSpec((None, 4 * MIX_W, d), lambda bi, i: (layer, 0, 0), pipeline_mode=pl.Buffered(1)),
                  row_spec, mx(2), mc(2),
                  pl.BlockSpec((1, d), lambda bi, i: (0, 0)),
                  mx(3), mx(4), mc(3), mc(4),
                  pl.BlockSpec((2, d, HEAD_W), lambda bi, i: (0, 0, 0)),
                  pl.BlockSpec((n_exp, 1), lambda bi, i: (0, 0))],
        out_specs=[row_spec, pl.BlockSpec((None, tm, d // 2), lambda bi, i: (bi, i, 0)), tok_spec, tok_spec,
                   pl.BlockSpec((n_exp, HEAD_W), lambda bi, i: (0, 0))],
        out_shape=[jax.ShapeDtypeStruct((b, n, d), F32),
                   jax.ShapeDtypeStruct((b, n, d // 2), jnp.uint32),
                   jax.ShapeDtypeStruct((8, b * n), jnp.int32),
                   jax.ShapeDtypeStruct((8, b * n), F32),
                   jax.ShapeDtypeStruct((n_exp, HEAD_W), jnp.int32)],
        scratch_shapes=[pltpu.VMEM((n_exp, HEAD_W), F32)],
        compiler_params=_cparams(2, VMEM_LIMIT),
        name="out_router",
    )(*mixes, w_out_bf, xc, mod3, mod3, g2.reshape(1, d), mod3, mod3, mod3, mod3,
      _router_slabs(router_w), router_b.reshape(n_exp, 1))


def _row_copy(src_ref, src_row, dst_ref, dst_row, sem):
    return pltpu.make_async_copy(src_ref.at[pl.ds(src_row, 1)], dst_ref.at[pl.ds(dst_row, 1)], sem)


def _dispatch_kernel(p0_ref, p1_ref, h_ref, xs_in_ref, xs_ref, sem, *, tm):
    del xs_in_ref
    base = pl.program_id(0) * tm

    for r in range(tm):
        _row_copy(h_ref, r, xs_ref, p0_ref[base + r], sem).start()
        _row_copy(h_ref, r, xs_ref, p1_ref[base + r], sem).start()

    tile_copy = pltpu.make_async_copy(h_ref, xs_ref.at[pl.ds(0, tm)], sem)
    tile_copy.wait()
    tile_copy.wait()


def _dispatch_call(h2, pos0, pos1, xs0):
    t, d = h2.shape
    n_pad = xs0.shape[0]
    tm = _pick(t, (ROW_TILE, 128))
    grid_spec = pltpu.PrefetchScalarGridSpec(
        num_scalar_prefetch=2,
        grid=(t // tm,),
        in_specs=[pl.BlockSpec((tm, d), lambda i, p0, p1: (i, 0)),
                  pl.BlockSpec(memory_space=pl.ANY)],
        out_specs=pl.BlockSpec(memory_space=pl.ANY),
        scratch_shapes=[pltpu.SemaphoreType.DMA],
    )
    return pl.pallas_call(
        functools.partial(_dispatch_kernel, tm=tm),
        grid_spec=grid_spec,
        out_shape=jax.ShapeDtypeStruct((n_pad, d), h2.dtype),
        input_output_aliases={3: 0},
        compiler_params=_cparams(1, VMEM_LIMIT),
        name="dispatch",
    )(pos0, pos1, h2, xs0)


def _ffn_kernel(be_ref, nb_ref, x_ref, wg_ref, wu_ref, wd_ref, y_ref):
    blk = pl.program_id(0)

    @pl.when(blk < nb_ref[0])
    def _():
        x = _unpack_bf16_pairs(x_ref[...])
        gate = jnp.dot(x, wg_ref[...], preferred_element_type=F32)
        up = jnp.dot(x, wu_ref[...], preferred_element_type=F32)
        hid = (jax.nn.silu(gate) * up).astype(BF16)
        y_ref[...] = jnp.dot(hid, wd_ref[...], preferred_element_type=F32)

    @pl.when(blk >= nb_ref[0])
    def _():
        y_ref[...] = jnp.zeros_like(y_ref)


def _ffn_call(xs, blk_expert, n_used, wg, wu, wd):
    n_pad, half = xs.shape
    d = 2 * half
    d_exp = wg.shape[2]
    grid_spec = pltpu.PrefetchScalarGridSpec(
        num_scalar_prefetch=2,
        grid=(n_pad // MOE_BLOCK,),
        in_specs=[pl.BlockSpec((MOE_BLOCK, half), lambda i, be, nb: (i, 0)),
                  pl.BlockSpec((None, d, d_exp), lambda i, be, nb: (be[i], 0, 0)),
                  pl.BlockSpec((None, d, d_exp), lambda i, be, nb: (be[i], 0, 0)),
                  pl.BlockSpec((None, d_exp, d), lambda i, be, nb: (be[i], 0, 0))],
        out_specs=pl.BlockSpec((MOE_BLOCK, d), lambda i, be, nb: (i, 0)),
    )
    return pl.pallas_call(
        _ffn_kernel,
        grid_spec=grid_spec,
        out_shape=jax.ShapeDtypeStruct((n_pad, d), F32),
        compiler_params=_cparams(1, VMEM_LIMIT),
        name="expert_ffn",
    )(blk_expert, n_used, xs, wg, wu, wd)


def _combine_kernel(p0_ref, p1_ref, ys_ref, x_ref, gt_ref, gx_ref, gc_ref, *rest, s_len, n_len, tm, with_next):
    if with_next:
        g_ref, shx_ref, scx_ref, shc_ref, scc_ref, o_ref, h_ref, y0_ref, y1_ref, sem = rest
    else:
        g_ref, o_ref, y0_ref, y1_ref, sem = rest
    i = pl.program_id(0)
    last = pl.num_programs(0) - 1

    def start_tile(tile, slot):
        base = tile * tm
        for r in range(tm):
            _row_copy(ys_ref, p0_ref[base + r], y0_ref.at[slot], r, sem.at[slot]).start()
            _row_copy(ys_ref, p1_ref[base + r], y1_ref.at[slot], r, sem.at[slot]).start()

    def wait_tile(slot):
        pltpu.make_async_copy(ys_ref.at[pl.ds(0, tm)], y0_ref.at[slot], sem.at[slot]).wait()
        pltpu.make_async_copy(ys_ref.at[pl.ds(0, tm)], y1_ref.at[slot], sem.at[slot]).wait()

    def combine(slot):
        row = ((i * tm) % n_len) + lax.broadcasted_iota(jnp.int32, (tm, 1), 0)
        is_ctx = row >= s_len
        gate = jnp.where(is_ctx, gc_ref[...], gx_ref[...])
        gt = gt_ref[...]
        y = gt[:, 0:1] * y0_ref[slot] + gt[:, 1:2] * y1_ref[slot]
        x_new = x_ref[...] + gate * y
        if with_next:
            o_ref[...] = x_new
            h = _norm_mod(x_new, g_ref[...], shx_ref[...], scx_ref[...], shc_ref[...], scc_ref[...], is_ctx)
            h_ref[...] = h.astype(h_ref.dtype)
        else:
            @pl.when((i * tm) % n_len < s_len)
            def _():
                o_ref[...] = x_new * lax.rsqrt(jnp.mean(x_new * x_new, axis=-1, keepdims=True) + EPS) * g_ref[...]

    @pl.when(i == 0)
    def _():
        start_tile(0, 0)

    for slot in (0, 1):
        @pl.when(i % 2 == slot)
        def _(slot=slot):
            wait_tile(slot)
            start_tile(jnp.minimum(i + 1, last), 1 - slot)
            combine(slot)

            @pl.when(i == last)
            def _():
                wait_tile(1 - slot)


def _combine_call(ys, pos0, pos1, x1, gates, mod3, s_len, k_gate, next_norm, final_g):
    b, n, d = x1.shape
    t = b * n
    tm = _pick(n, (ROW_TILE, 128))
    row_spec = pl.BlockSpec((tm, d), lambda i, p0, p1: (i, 0))
    mx = lambda k: pl.BlockSpec((None, 1, d), lambda i, p0, p1: ((i * tm) // n, 0, k))
    mc = lambda k: pl.BlockSpec((None, 1, d), lambda i, p0, p1: (b, 0, k))
    in_specs = [pl.BlockSpec(memory_space=pl.ANY), row_spec,
                pl.BlockSpec((tm, TOP_K), lambda i, p0, p1: (i, 0)), mx(k_gate), mc(k_gate)]
    args = [pos0, pos1, ys, x1.reshape(t, d), gates, mod3, mod3]
    if next_norm is None:
        per_b, lat = n // tm, s_len // tm
        in_specs.append(pl.BlockSpec((1, d), lambda i, p0, p1: (0, 0)))
        args.append(final_g.reshape(1, d))
        out_specs = [pl.BlockSpec((tm, d), lambda i, p0, p1: ((i // per_b) * lat + jnp.minimum(i % per_b, lat - 1), 0))]
        out_shape = [jax.ShapeDtypeStruct((b * s_len, d), F32)]
    else:
        out_specs, out_shape = [row_spec], [jax.ShapeDtypeStruct((t, d), F32)]
    if next_norm is not None:
        g_next, mod_next = next_norm
        in_specs += [pl.BlockSpec((1, d), lambda i, p0, p1: (0, 0)), mx(0), mx(1), mc(0), mc(1)]
        args += [g_next.reshape(1, d), mod_next, mod_next, mod_next, mod_next]
        out_specs.append(row_spec)
        out_shape.append(jax.ShapeDtypeStruct((t, d), BF16))
    grid_spec = pltpu.PrefetchScalarGridSpec(
        num_scalar_prefetch=2,
        grid=(t // tm,),
        in_specs=in_specs,
        out_specs=out_specs,
        scratch_shapes=[pltpu.VMEM((2, tm, d), F32), pltpu.VMEM((2, tm, d), F32), pltpu.SemaphoreType.DMA((2,))],
    )
    outs = pl.pallas_call(
        functools.partial(_combine_kernel, s_len=s_len, n_len=n, tm=tm, with_next=next_norm is not None),
        grid_spec=grid_spec,
        out_shape=out_shape,
        compiler_params=_cparams(1, VMEM_LIMIT),
        name="combine",
    )(*args)
    if next_norm is None:
        return outs[0].reshape(b, s_len, d), None
    return outs[0].reshape(b, n, d), outs[1]


def _moe(x1, h2, ri, rf, cnt, xs0, mod3, wg, wu, wd, s_len, next_norm, final_g):
    b, n, d = x1.shape
    t = b * n
    counts = cnt[:, 0]
    n_blk = (counts + MOE_BLOCK - 1) // MOE_BLOCK
    blk_end = jnp.cumsum(n_blk)
    starts = (blk_end - n_blk) * MOE_BLOCK
    pos0 = (starts[ri[0]] + ri[2]).astype(jnp.int32)
    pos1 = (starts[ri[1]] + ri[3]).astype(jnp.int32)
    n_pad = xs0.shape[0]
    n_used = blk_end[-1:].astype(jnp.int32)
    blk_ids = jnp.minimum(jnp.arange(n_pad // MOE_BLOCK, dtype=jnp.int32), n_used[0] - 1)
    blk_expert = jnp.sum(blk_ids[:, None] >= blk_end[None, :], axis=1).astype(jnp.int32)
    xs = _dispatch_call(h2.reshape(t, d // 2), pos0, pos1, xs0)
    ys = _ffn_call(xs, blk_expert, n_used, wg, wu, wd)
    return _combine_call(ys, pos0, pos1, x1, rf[:TOP_K].T, mod3, s_len, 5, next_norm, final_g)


def kernel(x, c, ctx, c_ctx, w_ada, b_ada, g_norm1, g_norm2, w_in, w_out, na_rpb, da_lambda, da_subln_g,
           pool_w, pool_scale, sg_norm_g, sg_w, sg_b, router_w, router_b, exp_w_gate, exp_w_up, exp_w_down,
           g_final):
    b, s_len, d = x.shape
    c_len = ctx.shape[1]
    n = s_len + c_len
    depth = w_ada.shape[0]
    assert b + 1 <= 8 and w_in.shape[2] == ATT_COLS + REST_COLS and w_out.shape[1] == 4 * MIX_W

    cond = jnp.zeros((8, d), F32).at[:b].set(c).at[b].set(c_ctx)
    mod_all = _ada_all(cond, w_ada, b_ada)
    cos, sin = _rope_tables(s_len, c_len)
    w_out_bf = w_out.astype(BF16)
    mods = [mod_all[l].reshape(8, 1, N_MOD * d) for l in range(depth)]
    n_dispatch = b * n * TOP_K + exp_w_gate.shape[1] * MOE_BLOCK

    xc, h1 = _join_norm_call(x, ctx, g_norm1[0], mods[0])
    h1 = h1.reshape(b * n, d)
    for l in range(depth):
        mod3 = mods[l]
        pxa = _matmul_cols(h1, w_in, l, 0, ATT_COLS, BF16, 1024).reshape(b, n, ATT_COLS)
        pxb = _matmul_cols(h1, w_in, l, ATT_COLS, REST_COLS, F32, 768).reshape(b, n, REST_COLS)
        qt, kr, vt = _rope_call(pxa, cos, sin)
        mix_a = _na_call(pxa, na_rpb[l], s_len, c_len)
        mix_b, (wg, wu, wd), xs0 = _da_call(qt, kr, vt, da_lambda[l], da_subln_g[l],
                                            (exp_w_gate, exp_w_up, exp_w_down), n_dispatch, s_len, c_len, l)
        mix_c = _pool_call(pxb, pool_w[l], pool_scale[l], s_len, c_len)
        mix_d = _sg_call(pxb, sg_norm_g[l], sg_w[l], sg_b[l])
        x1, h2, ri, rf, cnt = _out_router_call((mix_a, mix_b, mix_c, mix_d), w_out_bf, l, xc, g_norm2[l], mod3,
                                               router_w, router_b, s_len)
        next_norm = (g_norm1[l + 1], mods[l + 1]) if l + 1 < depth else None
        xc, h1 = _moe(x1, h2, ri, rf, cnt, xs0, mod3, wg, wu, wd, s_len, next_norm, g_final)
    return xc
```

```python
import functools
import math

import numpy as np
import jax
import jax.numpy as jnp
from jax import lax
from jax.experimental import pallas as pl
from jax.experimental.pallas import tpu as pltpu

GRID_W = 64
NA_HEADS = 4
NA_DIM = 128
NA_WIN_H = 8
NA_WIN_W = 16
DA_HEADS = 4
DA_DIM = 64
DA_VDIM = 2 * DA_DIM
ROPE_BASE = 10000.0
POOL_GROUPS = 4
POOL_WIDTH = 128
POOL_WINDOWS = (2, 4, 8, 16)
SG_GROUPS = 4
SG_WIDTH = 128
SG_CHUNK = 128
N_EXPERT_GROUPS = 4
EXPERTS_PER_GROUP = 4
TOP_K = 2
N_MOD = 6
EPS = 1e-6
NEG_INF = -1e30
LOG2_E = math.log2(math.e)

HEAD_W = 128
MIX_W = 512
ATT_COLS = 6 * MIX_W
REST_COLS = 3 * MIX_W
MOE_BLOCK = 256
NA_Q_ROWS = 4
NA_HEADS_PER_STEP = 4
DA_HEADS_PER_STEP = 2
ROW_TILE = 256
DA_KEY_CHUNKS = (1408, 768, 512, 256, 128)
DA_VT_ROWS = DA_VDIM + 16
VMEM_LIMIT = 56 * 1024 * 1024

F32 = jnp.float32
BF16 = jnp.bfloat16


def _cparams(n_axes, vmem=None):
    return pltpu.CompilerParams(dimension_semantics=("arbitrary",) * n_axes,
                                vmem_limit_bytes=vmem)


def _pick(n, candidates):
    for c in candidates:
        if n % c == 0:
            return c
    raise ValueError(f"no tile in {candidates} divides {n}")


def _ada_kernel(c_ref, w_ref, b_ref, o_ref):
    cs = jax.nn.silu(c_ref[...]).astype(BF16)
    o_ref[...] = jnp.dot(cs, w_ref[...].astype(BF16), preferred_element_type=F32) + b_ref[...]


def _ada_all(cond, w_ada, b_ada):
    depth, d, n6 = w_ada.shape
    tn = _pick(n6, (1024, 512, 256, 128))
    return pl.pallas_call(
        _ada_kernel,
        grid=(depth, n6 // tn),
        in_specs=[pl.BlockSpec((8, d), lambda l, j: (0, 0)),
                  pl.BlockSpec((None, d, tn), lambda l, j: (l, 0, j)),
                  pl.BlockSpec((None, 1, tn), lambda l, j: (l, 0, j))],
        out_specs=pl.BlockSpec((None, 8, tn), lambda l, j: (l, 0, j)),
        out_shape=jax.ShapeDtypeStruct((depth, 8, n6), F32),
        compiler_params=_cparams(2, VMEM_LIMIT),
        name="adaln",
    )(cond, w_ada, b_ada.reshape(depth, 1, n6))


def _row_is_ctx(i, tm, s_len):
    row = i * tm + lax.broadcasted_iota(jnp.int32, (tm, 1), 0)
    return row >= s_len


def _norm_mod(x, g, shx, scx, shc, scc, is_ctx):
    y = x * lax.rsqrt(jnp.mean(x * x, axis=-1, keepdims=True) + EPS) * g
    shift = jnp.where(is_ctx, shc, shx)
    scale = jnp.where(is_ctx, scc, scx)
    return y * (1.0 + scale) + shift


def _join_norm_kernel(x_ref, c_ref, g_ref, shx_ref, scx_ref, shc_ref, scc_ref, xc_ref, h_ref, *, s_len, tm):
    i = pl.program_id(1)
    is_ctx = _row_is_ctx(i, tm, s_len)
    x = jnp.where(i < s_len // tm, x_ref[...], c_ref[...])
    xc_ref[...] = x
    h = _norm_mod(x, g_ref[...], shx_ref[...], scx_ref[...], shc_ref[...], scc_ref[...], is_ctx)
    h_ref[...] = h.astype(h_ref.dtype)


def _join_norm_call(x, ctx, g, mod3):
    b, s_len, d = x.shape
    c_len = ctx.shape[1]
    n = s_len + c_len
    tm = _pick(math.gcd(s_len, c_len), (ROW_TILE, 128))
    n_lat = s_len // tm
    mx = lambda k: pl.BlockSpec((None, 1, d), lambda bi, i: (bi, 0, k))
    mc = lambda k: pl.BlockSpec((None, 1, d), lambda bi, i: (b, 0, k))
    row_spec = pl.BlockSpec((None, tm, d), lambda bi, i: (bi, i, 0))
    return pl.pallas_call(
        functools.partial(_join_norm_kernel, s_len=s_len, tm=tm),
        grid=(b, n // tm),
        in_specs=[pl.BlockSpec((None, tm, d), lambda bi, i: (bi, jnp.minimum(i, n_lat - 1), 0)),
                  pl.BlockSpec((None, tm, d), lambda bi, i: (bi, jnp.maximum(i - n_lat, 0), 0)),
                  pl.BlockSpec((1, d), lambda bi, i: (0, 0)),
                  mx(0), mx(1), mc(0), mc(1)],
        out_specs=[row_spec, row_spec],
        out_shape=[jax.ShapeDtypeStruct((b, n, d), F32), jax.ShapeDtypeStruct((b, n, d), BF16)],
        compiler_params=_cparams(2, VMEM_LIMIT),
        name="join_norm",
    )(x, ctx, g.reshape(1, d), mod3, mod3, mod3, mod3)


def _mm_kernel(a_ref, w_ref, o_ref, wbf_ref):
    @pl.when(pl.program_id(1) == 0)
    def _():
        wbf_ref[...] = w_ref[...].astype(BF16)

    o_ref[...] = jnp.dot(a_ref[...], wbf_ref[...], preferred_element_type=F32).astype(o_ref.dtype)


def _matmul_cols(a, w, layer, col0, ncols, out_dtype, tn):
    m, k = a.shape
    tm = _pick(m, (1536, 768, 512, 384, 256, 128))
    off = col0 // tn
    return pl.pallas_call(
        _mm_kernel,
        grid=(ncols // tn, m // tm),
        in_specs=[pl.BlockSpec((tm, k), lambda j, i: (i, 0)),
                  pl.BlockSpec((None, k, tn), lambda j, i: (layer, 0, j + off))],
        out_specs=pl.BlockSpec((tm, tn), lambda j, i: (i, j)),
        out_shape=jax.ShapeDtypeStruct((m, ncols), out_dtype),
        scratch_shapes=[pltpu.VMEM((k, tn), BF16)],
        compiler_params=_cparams(2, VMEM_LIMIT),
        name="in_proj",
    )(a, w)


def _rope_tables(s_len, c_len):
    t = jnp.arange(s_len)
    row = (t // GRID_W).astype(F32)
    col = (t % GRID_W).astype(F32)
    n_freq = DA_DIM // 4
    inv_freq = 1.0 / (ROPE_BASE ** (jnp.arange(n_freq, dtype=F32) / n_freq))
    ang = jnp.concatenate([row[:, None] * inv_freq, col[:, None] * inv_freq], axis=-1)
    cos = jnp.concatenate([jnp.cos(ang), jnp.ones((c_len, DA_DIM // 2), F32)], axis=0)
    sin = jnp.concatenate([jnp.sin(ang), jnp.zeros((c_len, DA_DIM // 2), F32)], axis=0)
    return jnp.tile(cos, (1, 4)), jnp.tile(jnp.concatenate([-sin, sin], axis=-1), (1, 2))


def _rope_kernel(q_ref, k_ref, v_ref, c_ref, s_ref, qt_ref, kr_ref, vt_ref, *, tm, q_scale):
    cos = c_ref[...]
    sin = s_ref[...]
    lane = lax.broadcasted_iota(jnp.int32, (tm, HEAD_W), 1)
    first_half = (lane % DA_DIM) < (DA_DIM // 2)

    def rotate(x):
        swapped = jnp.where(first_half, pltpu.roll(x, HEAD_W - DA_DIM // 2, 1), pltpu.roll(x, DA_DIM // 2, 1))
        return x * cos + swapped * sin

    for h in range(DA_HEADS):
        cols = slice(h * HEAD_W, (h + 1) * HEAD_W)
        q = rotate(q_ref[:, cols].astype(F32)) * q_scale
        qt_ref[cols, :] = q.T.astype(qt_ref.dtype)
        kr_ref[:, cols] = rotate(k_ref[:, cols].astype(F32)).astype(kr_ref.dtype)
        vt_ref[h * DA_VT_ROWS:h * DA_VT_ROWS + DA_VDIM, :] = v_ref[:, cols].astype(F32).T.astype(vt_ref.dtype)
        pad = lax.broadcasted_iota(jnp.int32, (DA_VT_ROWS - DA_VDIM, tm), 0)
        vt_ref[h * DA_VT_ROWS + DA_VDIM:(h + 1) * DA_VT_ROWS, :] = jnp.where(pad == 0, 1.0, 0.0).astype(vt_ref.dtype)


def _rope_call(pxa, cos, sin):
    b, n, _ = pxa.shape
    tm = _pick(n, (ROW_TILE, 128))
    col_spec = lambda k: pl.BlockSpec((None, tm, MIX_W), lambda bi, i: (bi, i, k))
    t_spec = lambda rows: pl.BlockSpec((None, rows, tm), lambda bi, i: (bi, 0, i))
    return pl.pallas_call(
        functools.partial(_rope_kernel, tm=tm, q_scale=DA_DIM ** -0.5 * LOG2_E),
        grid=(b, n // tm),
        in_specs=[col_spec(3), col_spec(4), col_spec(5),
                  pl.BlockSpec((tm, HEAD_W), lambda bi, i: (i, 0)),
                  pl.BlockSpec((tm, HEAD_W), lambda bi, i: (i, 0))],
        out_specs=[t_spec(MIX_W), col_spec(0), t_spec(DA_HEADS * DA_VT_ROWS)],
        out_shape=[jax.ShapeDtypeStruct((b, MIX_W, n), BF16),
                   jax.ShapeDtypeStruct((b, n, MIX_W), BF16),
                   jax.ShapeDtypeStruct((b, DA_HEADS * DA_VT_ROWS, n), BF16)],
        compiler_params=_cparams(2, VMEM_LIMIT),
        name="rope",
    )(pxa, pxa, pxa, cos, sin)


def _na_plan(rows, r_q):
    kh = min(NA_WIN_H, rows)
    key_rows = r_q + kh - 1
    assert rows % r_q == 0 and rows >= key_rows
    patterns, var, ustart = [], [], []
    for j in range(rows // r_q):
        r = r_q * j + np.arange(r_q)
        rs = np.clip(r - kh // 2, 0, rows - kh)
        u = int(np.clip(rs.min(), 0, rows - key_rows))
        kr = u + np.arange(key_rows)
        valid = (kr[None, :] >= rs[:, None]) & (kr[None, :] < rs[:, None] + kh)
        assert (valid.sum(1) == kh).all()
        dr0 = u - r + (NA_WIN_H - 1)
        key = (valid.tobytes(), dr0.tobytes())
        for v, (k2, _, _) in enumerate(patterns):
            if k2 == key:
                break
        else:
            v = len(patterns)
            patterns.append((key, valid, dr0))
        var.append(v)
        ustart.append(u * GRID_W)
    valid = np.stack([p[1] for p in patterns])
    dr0 = np.stack([p[2] for p in patterns])
    return np.asarray(var, np.int32), np.asarray(ustart, np.int32), valid, dr0


def _na_bias_tables(rpb, valid, dr0):
    col = np.arange(GRID_W)
    cs = np.clip(col - NA_WIN_W // 2, 0, GRID_W - NA_WIN_W)
    cmask = (col[None, :] >= cs[:, None]) & (col[None, :] < cs[:, None] + NA_WIN_W)
    dc = np.clip(col[None, :] - col[:, None], -(NA_WIN_W - 1), NA_WIN_W - 1) + (NA_WIN_W - 1)
    heads, n_dr, n_dc = rpb.shape
    v, r_q, key_rows = valid.shape
    pick = (dc.reshape(1, -1) == np.arange(n_dc)[:, None]).astype(np.float32)
    by_col = jnp.dot(rpb.astype(F32).reshape(heads * n_dr, n_dc), pick,
                     precision=lax.Precision.HIGHEST).reshape(heads, n_dr, GRID_W, GRID_W)
    padded = jnp.pad(by_col, ((0, 0), (key_rows, key_rows), (0, 0), (0, 0)))
    tab = jnp.stack([lax.slice_in_dim(padded, int(d) + key_rows, int(d) + 2 * key_rows, axis=1)
                     for d in dr0.reshape(-1)], axis=1)
    tab = tab.reshape(heads, v, r_q, key_rows, GRID_W, GRID_W).transpose(0, 1, 2, 4, 3, 5)
    mask = valid[:, :, None, :, None] & cmask[None, None, :, None, :]
    tab = jnp.where(mask[None], tab, NEG_INF)
    return tab.reshape(heads, v, r_q * GRID_W, key_rows * GRID_W)


def _na_kernel(var_ref, u_ref, q_ref, k_ref, v_ref, kc_ref, vc_ref, bias_ref, o_ref, *, n_lat, kw, scale):
    j = pl.program_id(2)
    nt = (((1,), (1,)), ((), ()))

    def one_head(hh, local):
        cols = slice(hh * HEAD_W, (hh + 1) * HEAD_W)
        q = q_ref[:, cols]
        vc = vc_ref[:, cols]
        s_ctx = lax.dot_general(q, kc_ref[:, cols], nt, preferred_element_type=F32) * scale
        m = jnp.max(s_ctx, axis=-1, keepdims=True)
        if local:
            u = pl.multiple_of(u_ref[j], GRID_W)
            s = lax.dot_general(q, k_ref[pl.ds(u, kw), cols], nt, preferred_element_type=F32) * scale + bias_ref[hh]
            m = jnp.maximum(m, jnp.max(s, axis=-1, keepdims=True))
            p = jnp.exp(s - m)
        pc = jnp.exp(s_ctx - m)
        l = jnp.sum(pc, axis=-1, keepdims=True)
        o = jnp.dot(pc.astype(BF16), vc, preferred_element_type=F32)
        if local:
            l += jnp.sum(p, axis=-1, keepdims=True)
            o += jnp.dot(p.astype(BF16), v_ref[pl.ds(u, kw), cols], preferred_element_type=F32)
        o_ref[:, cols] = (o / l).astype(o_ref.dtype)

    @pl.when(j < n_lat)
    def _():
        for hh in range(NA_HEADS_PER_STEP):
            one_head(hh, True)

    @pl.when(j >= n_lat)
    def _():
        for hh in range(NA_HEADS_PER_STEP):
            one_head(hh, False)


def _na_call(pxa, rpb, s_len, c_len):
    b, n, _ = pxa.shape
    rows = s_len // GRID_W
    tq = NA_Q_ROWS * GRID_W
    hps = NA_HEADS_PER_STEP
    assert s_len % tq == 0 and c_len % tq == 0 and s_len % c_len == 0 and NA_HEADS % hps == 0
    var, ustart, valid, dr0 = _na_plan(rows, NA_Q_ROWS)
    n_lat, n_q = s_len // tq, n // tq
    kw = valid.shape[2] * GRID_W
    tabs = _na_bias_tables(rpb, valid, dr0)
    pad = np.zeros(n_q - n_lat, np.int32)
    var = jnp.asarray(np.concatenate([var, pad]))
    ustart = jnp.asarray(np.concatenate([ustart, pad]))
    ctx_blk = s_len // c_len
    wide = hps * HEAD_W
    k_blk, v_blk = NA_HEADS // hps, 2 * NA_HEADS // hps
    grid_spec = pltpu.PrefetchScalarGridSpec(
        num_scalar_prefetch=2,
        grid=(b, NA_HEADS // hps, n_q),
        in_specs=[pl.BlockSpec((None, tq, wide), lambda bi, h, j, vr, ur: (bi, j, h)),
                  pl.BlockSpec((None, s_len, wide), lambda bi, h, j, vr, ur: (bi, 0, k_blk + h)),
                  pl.BlockSpec((None, s_len, wide), lambda bi, h, j, vr, ur: (bi, 0, v_blk + h)),
                  pl.BlockSpec((None, c_len, wide), lambda bi, h, j, vr, ur: (bi, ctx_blk, k_blk + h)),
                  pl.BlockSpec((None, c_len, wide), lambda bi, h, j, vr, ur: (bi, ctx_blk, v_blk + h)),
                  pl.BlockSpec((hps, None, tq, kw), lambda bi, h, j, vr, ur: (h, vr[j], 0, 0))],
        out_specs=pl.BlockSpec((None, tq, wide), lambda bi, h, j, vr, ur: (bi, j, h)),
    )
    return pl.pallas_call(
        functools.partial(_na_kernel, n_lat=n_lat, kw=kw, scale=NA_DIM ** -0.5),
        grid_spec=grid_spec,
        out_shape=jax.ShapeDtypeStruct((b, n, MIX_W), BF16),
        compiler_params=_cparams(3, VMEM_LIMIT),
        name="nbr_attn",
    )(var, ustart, pxa, pxa, pxa, pxa, pxa, tabs)


def _cast_expert_weights(step, n_steps, layer, srcs, dsts, in_g, in_u, in_d, out_g, out_u, out_d, sem_in, sem_out):
    ins, outs = (in_g, in_u, in_d), (out_g, out_u, out_d)
    n_chunks = srcs[0].shape[1] // in_g.shape[1]
    cur = jnp.minimum(step, n_chunks - 1)
    nxt = jnp.minimum(step + 1, n_chunks - 1)
    slot = step % 2
    other = 1 - slot

    def in_copy(t, chunk, sl):
        rows = ins[t].shape[1]
        src = srcs[t].at[layer, pl.ds(pl.multiple_of(chunk * rows, rows), rows)]
        return pltpu.make_async_copy(src, ins[t].at[sl], sem_in.at[sl])

    def out_copy(t, chunk, sl):
        rows = outs[t].shape[1]
        dst = dsts[t].at[pl.ds(pl.multiple_of(chunk * rows, rows), rows)]
        return pltpu.make_async_copy(outs[t].at[sl], dst, sem_out.at[sl])

    @pl.when(step == 0)
    def _():
        for t in range(3):
            in_copy(t, cur, slot).start()

    for t in range(3):
        in_copy(t, nxt, other).start()
    for t in range(3):
        in_copy(t, cur, slot).wait()

    @pl.when(step > 0)
    def _():
        for t in range(3):
            out_copy(t, cur, other).wait()

    for t in range(3):
        outs[t][slot] = ins[t][slot].astype(BF16)
    for t in range(3):
        out_copy(t, cur, slot).start()

    @pl.when(step == n_steps - 1)
    def _():
        for t in range(3):
            out_copy(t, cur, slot).wait()
            in_copy(t, nxt, other).wait()


def _zero_fill(step, n_steps, dst_ref, zero_ref, sem):
    rows = zero_ref.shape[0]
    chunk = jnp.minimum(step, dst_ref.shape[0] // rows - 1)
    fill = pltpu.make_async_copy(zero_ref, dst_ref.at[pl.ds(pl.multiple_of(chunk * rows, rows), rows)], sem)

    @pl.when(step == 0)
    def _():
        zero_ref[...] = jnp.zeros_like(zero_ref)

    @pl.when(step > 0)
    def _():
        fill.wait()

    fill.start()

    @pl.when(step == n_steps - 1)
    def _():
        fill.wait()


def _da_kernel(qt_ref, k_ref, vt_ref, lam_ref, g_ref, wg_ref, wu_ref, wd_ref, o_ref, wgb_ref, wub_ref, wdb_ref,
               xs0_ref, *scratch, s_len, c_len, tq, tk, lam_init, heads, layer):
    score_refs, cast_refs, fill_refs = scratch[:2 * heads], scratch[2 * heads:-2], scratch[-2:]
    sa_refs, sb_refs = score_refs[0::2], score_refs[1::2]
    qi = pl.program_id(2)
    step = (pl.program_id(0) * pl.num_programs(1) + pl.program_id(1)) * pl.num_programs(2) + qi
    n_steps = pl.num_programs(0) * pl.num_programs(1) * pl.num_programs(2)
    _cast_expert_weights(step, n_steps, layer, (wg_ref, wu_ref, wd_ref), (wgb_ref, wub_ref, wdb_ref), *cast_refs)
    _zero_fill(step, n_steps, xs0_ref, *fill_refs)
    n = s_len + c_len
    feat = lax.broadcasted_iota(jnp.int32, (2 * DA_DIM, tq), 0)

    def q_tiles(idx):
        start = pl.multiple_of(idx * tq, tq)
        out = []
        for hh in range(heads):
            qt = qt_ref[hh * HEAD_W:(hh + 1) * HEAD_W, pl.ds(start, tq)]
            zero = jnp.zeros_like(qt)
            out.append(jnp.concatenate([jnp.where(feat < DA_DIM, qt, zero), jnp.where(feat >= DA_DIM, qt, zero)],
                                       axis=1))
        return out

    q2 = q_tiles(qi)

    def k_rows(hh, rows):
        return k_ref[rows, hh * HEAD_W:(hh + 1) * HEAD_W]

    def vt_cols(hh, cols):
        return vt_ref[hh * DA_VT_ROWS:(hh + 1) * DA_VT_ROWS, cols]

    def scores(k_blk, q):
        return jnp.dot(k_blk, q, preferred_element_type=F32)

    def update(s, vt_blk, carry):
        m, acc = carry
        m_new = jnp.maximum(m, jnp.max(s, axis=0, keepdims=True))
        alpha = jnp.exp2(m - m_new)
        p = jnp.exp2(s - m_new)
        acc_new = alpha * acc + jnp.dot(vt_blk, p.astype(BF16), preferred_element_type=F32)
        return m_new, acc_new

    def init():
        one = (jnp.full((1, 2 * tq), NEG_INF, F32), jnp.zeros((DA_VT_ROWS, 2 * tq), F32))
        return tuple(one for _ in range(heads))

    def finalize(stats):
        lv = lam_ref[...]
        t1 = jnp.sum(lv[0:1, :] * lv[1:2, :], axis=-1, keepdims=True)
        t2 = jnp.sum(lv[2:3, :] * lv[3:4, :], axis=-1, keepdims=True)
        lam = jnp.exp(t1) - jnp.exp(t2) + lam_init
        for hh in range(heads):
            _, acc = stats[hh]
            o = acc[:DA_VDIM, :] / acc[DA_VDIM:DA_VDIM + 1, :]
            o = o[:, :tq] - lam * o[:, tq:]
            y = o * lax.rsqrt(jnp.mean(o * o, axis=0, keepdims=True) + EPS) * g_ref[...]
            o_ref[:, hh * HEAD_W:(hh + 1) * HEAD_W] = (y * (1.0 - lam_init)).T.astype(o_ref.dtype)

    k_steps = n // tk
    full_pairs = (k_steps - 1) // 2
    carry_first = k_steps % 2 == 0

    @pl.when(qi < s_len // tq)
    def _():
        def scores_into(dst_refs, step, qs):
            rows = pl.ds(pl.multiple_of(step * tk, tk), tk)
            for hh in range(heads):
                dst_refs[hh][...] = scores(k_rows(hh, rows), qs[hh])

        def update_from(src_refs, step, stats):
            cols = pl.ds(pl.multiple_of(step * tk, tk), tk)
            return tuple(update(src_refs[hh][...], vt_cols(hh, cols), stats[hh]) for hh in range(heads))

        def pair(j, stats):
            c0 = 2 * j
            scores_into(sb_refs, c0 + 1, q2)
            stats = update_from(sa_refs, c0, stats)
            scores_into(sa_refs, c0 + 2, q2)
            return update_from(sb_refs, c0 + 1, stats)

        if carry_first:
            @pl.when(qi == 0)
            def _():
                scores_into(sa_refs, 0, q2)
        else:
            scores_into(sa_refs, 0, q2)
        stats = lax.fori_loop(0, full_pairs, pair, init())
        c0 = 2 * full_pairs
        if carry_first:
            scores_into(sb_refs, c0 + 1, q2)
            stats = update_from(sa_refs, c0, stats)
            scores_into(sa_refs, 0, q_tiles(qi + 1))
            stats = update_from(sb_refs, c0 + 1, stats)
        else:
            stats = update_from(sa_refs, c0, stats)
        finalize(stats)

    @pl.when(qi >= s_len // tq)
    def _():
        rows, stats = slice(s_len, n), init()
        finalize(tuple(update(scores(k_rows(hh, rows), q2[hh]), vt_cols(hh, rows), stats[hh])
                       for hh in range(heads)))


def _da_call(qt, kr, vt, lam_vecs, subln_g, exp_w, n_dispatch, s_len, c_len, layer):
    b, n, _ = kr.shape
    tq = _pick(math.gcd(s_len, c_len), (256, 128))
    tk = _pick(n, DA_KEY_CHUNKS)
    hps = DA_HEADS_PER_STEP
    assert DA_HEADS % hps == 0
    lam_init = 0.8 - 0.6 * math.exp(-0.3 * layer)
    grid = (b, DA_HEADS // hps, n // tq)
    n_chunks = 1 << (math.prod(grid).bit_length() - 1)
    flat = [w.reshape(w.shape[0], w.shape[1] * w.shape[2], w.shape[3]) for w in exp_w]
    chunk = [(w.shape[1] // n_chunks, w.shape[2]) for w in flat]
    assert all(w.shape[1] % n_chunks == 0 and r % 16 == 0 for w, (r, _) in zip(flat, chunk))
    d_model = exp_w[0].shape[2]
    fill_rows = n_dispatch // n_chunks
    assert n_dispatch % n_chunks == 0 and fill_rows % 8 == 0
    once = pl.Buffered(1)
    any_spec = pl.BlockSpec(memory_space=pl.ANY)
    outs = pl.pallas_call(
        functools.partial(_da_kernel, s_len=s_len, c_len=c_len, tq=tq, tk=tk, lam_init=lam_init, heads=hps,
                          layer=layer),
        grid=grid,
        in_specs=[pl.BlockSpec((None, hps * HEAD_W, n), lambda bi, h, i: (bi, h, 0), pipeline_mode=once),
                  pl.BlockSpec((None, n, hps * HEAD_W), lambda bi, h, i: (bi, 0, h), pipeline_mode=once),
                  pl.BlockSpec((None, hps * DA_VT_ROWS, n), lambda bi, h, i: (bi, h, 0), pipeline_mode=once),
                  pl.BlockSpec((4, DA_DIM), lambda bi, h, i: (0, 0)),
                  pl.BlockSpec((DA_VDIM, 1), lambda bi, h, i: (0, 0)),
                  any_spec, any_spec, any_spec],
        out_specs=[pl.BlockSpec((None, tq, hps * HEAD_W), lambda bi, h, i: (bi, i, h)),
                   any_spec, any_spec, any_spec, any_spec],
        out_shape=[jax.ShapeDtypeStruct((b, n, MIX_W), BF16)]
                  + [jax.ShapeDtypeStruct(w.shape[1:], BF16) for w in flat]
                  + [jax.ShapeDtypeStruct((n_dispatch, d_model), F32)],
        scratch_shapes=[pltpu.VMEM((tk, 2 * tq), F32)] * (2 * hps)
                       + [pltpu.VMEM((2,) + c, F32) for c in chunk]
                       + [pltpu.VMEM((2,) + c, BF16) for c in chunk]
                       + [pltpu.SemaphoreType.DMA((2,)), pltpu.SemaphoreType.DMA((2,))]
                       + [pltpu.VMEM((fill_rows, d_model), F32), pltpu.SemaphoreType.DMA],
        compiler_params=_cparams(3, VMEM_LIMIT),
        name="diff_attn",
    )(qt, kr, vt, lam_vecs, subln_g.reshape(DA_VDIM, 1), *flat)
    return outs[0], tuple(o.reshape(w.shape[1:]) for o, w in zip(outs[1:4], exp_w)), outs[4]


POOL_HALO = 8


def _pool_kernel(prev_ref, cur_ref, next_ref, w_ref, sc_ref, o_ref, buf_ref, *, s_len, c_len, tm):
    i = pl.program_id(1)
    n_lat = s_len // tm
    n_all = (s_len + c_len) // tm
    seq_start = (i == 0) | (i == n_lat)
    seq_end = (i == n_lat - 1) | (i == n_all - 1)
    buf_ref[0:POOL_HALO, :] = jnp.where(seq_start, 0.0, prev_ref[...])
    buf_ref[POOL_HALO:POOL_HALO + tm, :] = cur_ref[...]
    buf_ref[POOL_HALO + tm:, :] = jnp.where(seq_end, 0.0, next_ref[...])
    in_ctx = i >= n_lat
    seq_len = jnp.where(in_ctx, c_len, s_len)
    pos = i * tm - jnp.where(in_ctx, s_len, 0) + lax.broadcasted_iota(jnp.int32, (tm, 1), 0)
    for g, win in enumerate(POOL_WINDOWS):
        half = win // 2
        cols = slice(g * POOL_WIDTH, (g + 1) * POOL_WIDTH)
        tot = buf_ref[POOL_HALO - half:POOL_HALO - half + tm, cols]
        for d in range(-half + 1, half):
            tot = tot + buf_ref[POOL_HALO + d:POOL_HALO + d + tm, cols]
        cnt = jnp.clip(pos + half, 0, seq_len) - jnp.clip(pos - half, 0, seq_len)
        resid = tot / cnt.astype(F32) - cur_ref[:, cols]
        y = jnp.dot(resid.astype(BF16), w_ref[g].astype(BF16), preferred_element_type=F32)
        o_ref[:, cols] = (y * sc_ref[:, cols]).astype(o_ref.dtype)


def _pool_call(pxb, pool_w, pool_scale, s_len, c_len):
    b, n, _ = pxb.shape
    tm = _pick(math.gcd(s_len, c_len), (ROW_TILE, 128))
    hb = tm // POOL_HALO
    last = n // POOL_HALO - 1
    return pl.pallas_call(
        functools.partial(_pool_kernel, s_len=s_len, c_len=c_len, tm=tm),
        grid=(b, n // tm),
        in_specs=[pl.BlockSpec((None, POOL_HALO, MIX_W), lambda bi, i: (bi, jnp.maximum(i * hb - 1, 0), 0)),
                  pl.BlockSpec((None, tm, MIX_W), lambda bi, i: (bi, i, 0)),
                  pl.BlockSpec((None, POOL_HALO, MIX_W), lambda bi, i: (bi, jnp.minimum((i + 1) * hb, last), 0)),
                  pl.BlockSpec((POOL_GROUPS, POOL_WIDTH, POOL_WIDTH), lambda bi, i: (0, 0, 0)),
                  pl.BlockSpec((1, MIX_W), lambda bi, i: (0, 0))],
        out_specs=pl.BlockSpec((None, tm, MIX_W), lambda bi, i: (bi, i, 0)),
        out_shape=jax.ShapeDtypeStruct((b, n, MIX_W), BF16),
        scratch_shapes=[pltpu.VMEM((tm + 2 * POOL_HALO, MIX_W), F32)],
        compiler_params=_cparams(2, VMEM_LIMIT),
        name="pool",
    )(pxb, pxb, pxb, pool_w, pool_scale.reshape(1, MIX_W))


def _sg_kernel(u_ref, v_ref, g_ref, w_ref, b_ref, o_ref, *, tm):
    v = jax.nn.gelu(v_ref[...])
    vn = (v * lax.rsqrt(jnp.mean(v * v, axis=-1, keepdims=True) + EPS) * g_ref[...]).astype(BF16)
    bias = b_ref[...]
    for c in range(tm // SG_CHUNK):
        rows = slice(c * SG_CHUNK, (c + 1) * SG_CHUNK)
        for g in range(SG_GROUPS):
            cols = slice(g * SG_WIDTH, (g + 1) * SG_WIDTH)
            mixed = jnp.dot(w_ref[g].astype(BF16), vn[rows, cols], preferred_element_type=F32)
            mixed = mixed + bias[:, g:g + 1]
            o_ref[rows, cols] = (jax.nn.gelu(u_ref[rows, cols]) * mixed).astype(o_ref.dtype)


def _sg_call(pxb, sg_norm_g, sg_w, sg_b):
    b, n, _ = pxb.shape
    tm = _pick(n, (ROW_TILE, 128))
    return pl.pallas_call(
        functools.partial(_sg_kernel, tm=tm),
        grid=(b, n // tm),
        in_specs=[pl.BlockSpec((None, tm, MIX_W), lambda bi, i: (bi, i, 1)),
                  pl.BlockSpec((None, tm, MIX_W), lambda bi, i: (bi, i, 2)),
                  pl.BlockSpec((1, MIX_W), lambda bi, i: (0, 0)),
                  pl.BlockSpec((SG_GROUPS, SG_CHUNK, SG_CHUNK), lambda bi, i: (0, 0, 0)),
                  pl.BlockSpec((SG_CHUNK, SG_GROUPS), lambda bi, i: (0, 0))],
        out_specs=pl.BlockSpec((None, tm, MIX_W), lambda bi, i: (bi, i, 0)),
        out_shape=jax.ShapeDtypeStruct((b, n, MIX_W), BF16),
        compiler_params=_cparams(2, VMEM_LIMIT),
        name="spatial_gate",
    )(pxb, pxb, sg_norm_g.reshape(1, MIX_W), sg_w, sg_b.T)


def _first_argmax(vals):
    best, idx = vals[0], jnp.zeros(vals[0].shape, jnp.int32)
    for k in range(1, len(vals)):
        take = vals[k] > best
        best = jnp.where(take, vals[k], best)
        idx = jnp.where(take, k, idx)
    return idx, best


def _out_router_kernel(ma_ref, mb_ref, mc_ref, md_ref, w_ref, x_ref, gx_ref, gc_ref,
                       g_ref, shx_ref, scx_ref, shc_ref, scc_ref, rw_ref, rb_ref,
                       x1_ref, h_ref, ri_ref, rf_ref, cnt_ref, run_ref, *, s_len, tm, n_exp):
    bi, i = pl.program_id(0), pl.program_id(1)

    @pl.when((bi == 0) & (i == 0))
    def _():
        run_ref[...] = jnp.zeros_like(run_ref)

    mix = jnp.concatenate([ma_ref[...], mb_ref[...], mc_ref[...], md_ref[...]], axis=1)
    acc = jnp.dot(mix, w_ref[...], preferred_element_type=F32)
    is_ctx = _row_is_ctx(i, tm, s_len)
    x1 = x_ref[...] + jnp.where(is_ctx, gc_ref[...], gx_ref[...]) * acc
    x1_ref[...] = x1
    h = _norm_mod(x1, g_ref[...], shx_ref[...], scx_ref[...], shc_ref[...], scc_ref[...], is_ctx)
    h_ref[...] = h

    h_hi = h.astype(BF16)
    h_lo = (h - h_hi.astype(F32)).astype(BF16)
    part = jnp.dot(h_hi, rw_ref[0], preferred_element_type=F32)
    part += jnp.dot(h_lo, rw_ref[1], preferred_element_type=F32)
    part_t = part.T
    logits = part_t[0:n_exp, :] + part_t[n_exp:2 * n_exp, :]
    scores = jax.nn.sigmoid(logits)
    biased = scores + rb_ref[...]
    b_rows = [biased[e:e + 1, :] for e in range(n_exp)]
    s_rows = [scores[e:e + 1, :] for e in range(n_exp)]
    epg = EXPERTS_PER_GROUP
    group_scores = []
    for g in range(N_EXPERT_GROUPS):
        r = b_rows[g * epg:(g + 1) * epg]
        pair = None
        for a in range(epg):
            for c in range(a + 1, epg):
                t = r[a] + r[c]
                pair = t if pair is None else jnp.maximum(pair, t)
        group_scores.append(pair)
    group, _ = _first_argmax(group_scores)
    in_group = []
    for k in range(epg):
        sel = b_rows[k]
        for g in range(1, N_EXPERT_GROUPS):
            sel = jnp.where(group == g, b_rows[g * epg + k], sel)
        in_group.append(sel)
    loc0, _ = _first_argmax(in_group)
    rest = [jnp.where(loc0 == k, -jnp.inf, in_group[k]) for k in range(epg)]
    loc1, _ = _first_argmax(rest)
    e0 = group * epg + loc0
    e1 = group * epg + loc1
    g0 = jnp.zeros_like(s_rows[0])
    g1 = jnp.zeros_like(s_rows[0])
    for e in range(n_exp):
        g0 = jnp.where(e0 == e, s_rows[e], g0)
        g1 = jnp.where(e1 == e, s_rows[e], g1)
    tot = g0 + g1

    eidx = lax.broadcasted_iota(jnp.int32, (n_exp, tm), 0)
    hit0 = eidx == e0
    hit1 = eidx == e1
    onehot = jnp.where(hit0 | hit1, 1.0, 0.0)
    before = lax.broadcasted_iota(jnp.int32, (tm, tm), 0) < lax.broadcasted_iota(jnp.int32, (tm, tm), 1)
    tri = jnp.where(before, 1.0, 0.0).astype(BF16)
    run = run_ref[:, 0:1]
    rank = jnp.dot(onehot.astype(BF16), tri, preferred_element_type=F32) + run
    rank0 = jnp.sum(jnp.where(hit0, rank, 0.0), axis=0, keepdims=True)
    rank1 = jnp.sum(jnp.where(hit1, rank, 0.0), axis=0, keepdims=True)
    new_run = run + jnp.sum(onehot, axis=1, keepdims=True)
    run_ref[...] = jnp.broadcast_to(new_run, run_ref.shape)
    cnt_ref[...] = jnp.broadcast_to(new_run, cnt_ref.shape).astype(jnp.int32)

    zi = jnp.zeros((4, tm), jnp.int32)
    ri_ref[...] = jnp.concatenate([e0, e1, rank0.astype(jnp.int32), rank1.astype(jnp.int32), zi], axis=0)
    zf = jnp.zeros((6, tm), F32)
    rf_ref[...] = jnp.concatenate([g0 / tot, g1 / tot, zf], axis=0)


def _router_slabs(router_w):
    d, n_exp = router_w.shape
    w_hi = router_w.astype(BF16)
    w_lo = (router_w - w_hi.astype(F32)).astype(BF16)
    zeros = lambda cols: jnp.zeros((d, cols), BF16)
    return jnp.stack([jnp.concatenate([w_hi, w_lo, zeros(HEAD_W - 2 * n_exp)], axis=1),
                      jnp.concatenate([w_hi, zeros(HEAD_W - n_exp)], axis=1)])


def _out_router_call(mixes, w_out_bf, layer, xc, g2, mod3, router_w, router_b, s_len):
    b, n, d = xc.shape
    n_exp = router_w.shape[1]
    tm = _pick(n, (ROW_TILE, 128))
    per_b = n // tm
    mx = lambda k: pl.BlockSpec((None, 1, d), lambda bi, i: (bi, 0, k))
    mc = lambda k: pl.BlockSpec((None, 1, d), lambda bi, i: (b, 0, k))
    mix_spec = pl.BlockSpec((None, tm, MIX_W), lambda bi, i: (bi, i, 0))
    row_spec = pl.BlockSpec((None, tm, d), lambda bi, i: (bi, i, 0))
    tok_spec = pl.BlockSpec((8, tm), lambda bi, i: (0, bi * per_b + i))
    return pl.pallas_call(
        functools.partial(_out_router_kernel, s_len=s_len, tm=tm, n_exp=n_exp),
        grid=(b, per_b),
        in_specs=[mix_spec, mix_spec, mix_spec, mix_spec,
                  pl.BlockSpec((None, 4 * MIX_W, d), lambda bi, i: (layer, 0, 0), pipeline_mode=pl.Buffered(1)),
                  row_spec, mx(2), mc(2),
                  pl.BlockSpec((1, d), lambda bi, i: (0, 0)),
                  mx(3), mx(4), mc(3), mc(4),
                  pl.BlockSpec((2, d, HEAD_W), lambda bi, i: (0, 0, 0)),
                  pl.BlockSpec((n_exp, 1), lambda bi, i: (0, 0))],
        out_specs=[row_spec, row_spec, tok_spec, tok_spec,
                   pl.BlockSpec((n_exp, HEAD_W), lambda bi, i: (0, 0))],
        out_shape=[jax.ShapeDtypeStruct((b, n, d), F32),
                   jax.ShapeDtypeStruct((b, n, d), F32),
                   jax.ShapeDtypeStruct((8, b * n), jnp.int32),
                   jax.ShapeDtypeStruct((8, b * n), F32),
                   jax.ShapeDtypeStruct((n_exp, HEAD_W), jnp.int32)],
        scratch_shapes=[pltpu.VMEM((n_exp, HEAD_W), F32)],
        compiler_params=_cparams(2, VMEM_LIMIT),
        name="out_router",
    )(*mixes, w_out_bf, xc, mod3, mod3, g2.reshape(1, d), mod3, mod3, mod3, mod3,
      _router_slabs(router_w), router_b.reshape(n_exp, 1))


def _row_copy(src_ref, src_row, dst_ref, dst_row, sem):
    return pltpu.make_async_copy(src_ref.at[pl.ds(src_row, 1)], dst_ref.at[pl.ds(dst_row, 1)], sem)


def _dispatch_kernel(p0_ref, p1_ref, h_ref, xs_in_ref, xs_ref, sem, *, tm):
    del xs_in_ref
    base = pl.program_id(0) * tm

    for r in range(tm):
        _row_copy(h_ref, r, xs_ref, p0_ref[base + r], sem).start()
        _row_copy(h_ref, r, xs_ref, p1_ref[base + r], sem).start()

    tile_copy = pltpu.make_async_copy(h_ref, xs_ref.at[pl.ds(0, tm)], sem)
    tile_copy.wait()
    tile_copy.wait()


def _dispatch_call(h2, pos0, pos1, xs0):
    t, d = h2.shape
    n_pad = xs0.shape[0]
    tm = _pick(t, (ROW_TILE, 128))
    grid_spec = pltpu.PrefetchScalarGridSpec(
        num_scalar_prefetch=2,
        grid=(t // tm,),
        in_specs=[pl.BlockSpec((tm, d), lambda i, p0, p1: (i, 0)),
                  pl.BlockSpec(memory_space=pl.ANY)],
        out_specs=pl.BlockSpec(memory_space=pl.ANY),
        scratch_shapes=[pltpu.SemaphoreType.DMA],
    )
    return pl.pallas_call(
        functools.partial(_dispatch_kernel, tm=tm),
        grid_spec=grid_spec,
        out_shape=jax.ShapeDtypeStruct((n_pad, d), h2.dtype),
        input_output_aliases={3: 0},
        compiler_params=_cparams(1, VMEM_LIMIT),
        name="dispatch",
    )(pos0, pos1, h2, xs0)


def _ffn_kernel(be_ref, nb_ref, x_ref, wg_ref, wu_ref, wd_ref, y_ref):
    blk = pl.program_id(0)

    @pl.when(blk < nb_ref[0])
    def _():
        x = x_ref[...].astype(BF16)
        gate = jnp.dot(x, wg_ref[...], preferred_element_type=F32)
        up = jnp.dot(x, wu_ref[...], preferred_element_type=F32)
        hid = (jax.nn.silu(gate) * up).astype(BF16)
        y_ref[...] = jnp.dot(hid, wd_ref[...], preferred_element_type=F32)

    @pl.when(blk >= nb_ref[0])
    def _():
        y_ref[...] = jnp.zeros_like(y_ref)


def _ffn_call(xs, blk_expert, n_used, wg, wu, wd):
    n_pad, d = xs.shape
    d_exp = wg.shape[2]
    grid_spec = pltpu.PrefetchScalarGridSpec(
        num_scalar_prefetch=2,
        grid=(n_pad // MOE_BLOCK,),
        in_specs=[pl.BlockSpec((MOE_BLOCK, d), lambda i, be, nb: (i, 0)),
                  pl.BlockSpec((None, d, d_exp), lambda i, be, nb: (be[i], 0, 0)),
                  pl.BlockSpec((None, d, d_exp), lambda i, be, nb: (be[i], 0, 0)),
                  pl.BlockSpec((None, d_exp, d), lambda i, be, nb: (be[i], 0, 0))],
        out_specs=pl.BlockSpec((MOE_BLOCK, d), lambda i, be, nb: (i, 0)),
    )
    return pl.pallas_call(
        _ffn_kernel,
        grid_spec=grid_spec,
        out_shape=jax.ShapeDtypeStruct((n_pad, d), F32),
        compiler_params=_cparams(1, VMEM_LIMIT),
        name="expert_ffn",
    )(blk_expert, n_used, xs, wg, wu, wd)


def _combine_kernel(p0_ref, p1_ref, ys_ref, x_ref, gt_ref, gx_ref, gc_ref, *rest, s_len, n_len, tm, with_next):
    if with_next:
        g_ref, shx_ref, scx_ref, shc_ref, scc_ref, o_ref, h_ref, y0_ref, y1_ref, sem = rest
    else:
        g_ref, o_ref, y0_ref, y1_ref, sem = rest
    i = pl.program_id(0)
    last = pl.num_programs(0) - 1

    def start_tile(tile, slot):
        base = tile * tm
        for r in range(tm):
            _row_copy(ys_ref, p0_ref[base + r], y0_ref.at[slot], r, sem.at[slot]).start()
            _row_copy(ys_ref, p1_ref[base + r], y1_ref.at[slot], r, sem.at[slot]).start()

    def wait_tile(slot):
        pltpu.make_async_copy(ys_ref.at[pl.ds(0, tm)], y0_ref.at[slot], sem.at[slot]).wait()
        pltpu.make_async_copy(ys_ref.at[pl.ds(0, tm)], y1_ref.at[slot], sem.at[slot]).wait()

    def combine(slot):
        row = ((i * tm) % n_len) + lax.broadcasted_iota(jnp.int32, (tm, 1), 0)
        is_ctx = row >= s_len
        gate = jnp.where(is_ctx, gc_ref[...], gx_ref[...])
        gt = gt_ref[...]
        y = gt[:, 0:1] * y0_ref[slot] + gt[:, 1:2] * y1_ref[slot]
        x_new = x_ref[...] + gate * y
        if with_next:
            o_ref[...] = x_new
            h = _norm_mod(x_new, g_ref[...], shx_ref[...], scx_ref[...], shc_ref[...], scc_ref[...], is_ctx)
            h_ref[...] = h.astype(h_ref.dtype)
        else:
            @pl.when((i * tm) % n_len < s_len)
            def _():
                o_ref[...] = x_new * lax.rsqrt(jnp.mean(x_new * x_new, axis=-1, keepdims=True) + EPS) * g_ref[...]

    @pl.when(i == 0)
    def _():
        start_tile(0, 0)

    for slot in (0, 1):
        @pl.when(i % 2 == slot)
        def _(slot=slot):
            wait_tile(slot)
            start_tile(jnp.minimum(i + 1, last), 1 - slot)
            combine(slot)

            @pl.when(i == last)
            def _():
                wait_tile(1 - slot)


def _combine_call(ys, pos0, pos1, x1, gates, mod3, s_len, k_gate, next_norm, final_g):
    b, n, d = x1.shape
    t = b * n
    tm = _pick(n, (ROW_TILE, 128))
    row_spec = pl.BlockSpec((tm, d), lambda i, p0, p1: (i, 0))
    mx = lambda k: pl.BlockSpec((None, 1, d), lambda i, p0, p1: ((i * tm) // n, 0, k))
    mc = lambda k: pl.BlockSpec((None, 1, d), lambda i, p0, p1: (b, 0, k))
    in_specs = [pl.BlockSpec(memory_space=pl.ANY), row_spec,
                pl.BlockSpec((tm, TOP_K), lambda i, p0, p1: (i, 0)), mx(k_gate), mc(k_gate)]
    args = [pos0, pos1, ys, x1.reshape(t, d), gates, mod3, mod3]
    if next_norm is None:
        per_b, lat = n // tm, s_len // tm
        in_specs.append(pl.BlockSpec((1, d), lambda i, p0, p1: (0, 0)))
        args.append(final_g.reshape(1, d))
        out_specs = [pl.BlockSpec((tm, d), lambda i, p0, p1: ((i // per_b) * lat + jnp.minimum(i % per_b, lat - 1), 0))]
        out_shape = [jax.ShapeDtypeStruct((b * s_len, d), F32)]
    else:
        out_specs, out_shape = [row_spec], [jax.ShapeDtypeStruct((t, d), F32)]
    if next_norm is not None:
        g_next, mod_next = next_norm
        in_specs += [pl.BlockSpec((1, d), lambda i, p0, p1: (0, 0)), mx(0), mx(1), mc(0), mc(1)]
        args += [g_next.reshape(1, d), mod_next, mod_next, mod_next, mod_next]
        out_specs.append(row_spec)
        out_shape.append(jax.ShapeDtypeStruct((t, d), BF16))
    grid_spec = pltpu.PrefetchScalarGridSpec(
        num_scalar_prefetch=2,
        grid=(t // tm,),
        in_specs=in_specs,
        out_specs=out_specs,
        scratch_shapes=[pltpu.VMEM((2, tm, d), F32), pltpu.VMEM((2, tm, d), F32), pltpu.SemaphoreType.DMA((2,))],
    )
    outs = pl.pallas_call(
        functools.partial(_combine_kernel, s_len=s_len, n_len=n, tm=tm, with_next=next_norm is not None),
        grid_spec=grid_spec,
        out_shape=out_shape,
        compiler_params=_cparams(1, VMEM_LIMIT),
        name="combine",
    )(*args)
    if next_norm is None:
        return outs[0].reshape(b, s_len, d), None
    return outs[0].reshape(b, n, d), outs[1]


def _moe(x1, h2, ri, rf, cnt, xs0, mod3, wg, wu, wd, s_len, next_norm, final_g):
    b, n, d = x1.shape
    t = b * n
    counts = cnt[:, 0]
    n_blk = (counts + MOE_BLOCK - 1) // MOE_BLOCK
    blk_end = jnp.cumsum(n_blk)
    starts = (blk_end - n_blk) * MOE_BLOCK
    pos0 = (starts[ri[0]] + ri[2]).astype(jnp.int32)
    pos1 = (starts[ri[1]] + ri[3]).astype(jnp.int32)
    n_pad = xs0.shape[0]
    n_used = blk_end[-1:].astype(jnp.int32)
    blk_ids = jnp.minimum(jnp.arange(n_pad // MOE_BLOCK, dtype=jnp.int32), n_used[0] - 1)
    blk_expert = jnp.sum(blk_ids[:, None] >= blk_end[None, :], axis=1).astype(jnp.int32)
    xs = _dispatch_call(h2.reshape(t, d), pos0, pos1, xs0)
    ys = _ffn_call(xs, blk_expert, n_used, wg, wu, wd)
    return _combine_call(ys, pos0, pos1, x1, rf[:TOP_K].T, mod3, s_len, 5, next_norm, final_g)


def kernel(x, c, ctx, c_ctx, w_ada, b_ada, g_norm1, g_norm2, w_in, w_out, na_rpb, da_lambda, da_subln_g,
           pool_w, pool_scale, sg_norm_g, sg_w, sg_b, router_w, router_b, exp_w_gate, exp_w_up, exp_w_down,
           g_final):
    b, s_len, d = x.shape
    c_len = ctx.shape[1]
    n = s_len + c_len
    depth = w_ada.shape[0]
    assert b + 1 <= 8 and w_in.shape[2] == ATT_COLS + REST_COLS and w_out.shape[1] == 4 * MIX_W

    cond = jnp.zeros((8, d), F32).at[:b].set(c).at[b].set(c_ctx)
    mod_all = _ada_all(cond, w_ada, b_ada)
    cos, sin = _rope_tables(s_len, c_len)
    w_out_bf = w_out.astype(BF16)
    mods = [mod_all[l].reshape(8, 1, N_MOD * d) for l in range(depth)]
    n_dispatch = b * n * TOP_K + exp_w_gate.shape[1] * MOE_BLOCK

    xc, h1 = _join_norm_call(x, ctx, g_norm1[0], mods[0])
    h1 = h1.reshape(b * n, d)
    for l in range(depth):
        mod3 = mods[l]
        pxa = _matmul_cols(h1, w_in, l, 0, ATT_COLS, BF16, 1024).reshape(b, n, ATT_COLS)
        pxb = _matmul_cols(h1, w_in, l, ATT_COLS, REST_COLS, F32, 768).reshape(b, n, REST_COLS)
        qt, kr, vt = _rope_call(pxa, cos, sin)
        mix_a = _na_call(pxa, na_rpb[l], s_len, c_len)
        mix_b, (wg, wu, wd), xs0 = _da_call(qt, kr, vt, da_lambda[l], da_subln_g[l],
                                            (exp_w_gate, exp_w_up, exp_w_down), n_dispatch, s_len, c_len, l)
        mix_c = _pool_call(pxb, pool_w[l], pool_scale[l], s_len, c_len)
        mix_d = _sg_call(pxb, sg_norm_g[l], sg_w[l], sg_b[l])
        x1, h2, ri, rf, cnt = _out_router_call((mix_a, mix_b, mix_c, mix_d), w_out_bf, l, xc, g_norm2[l], mod3,
                                               router_w, router_b, s_len)
        next_norm = (g_norm1[l + 1], mods[l + 1]) if l + 1 < depth else None
        xc, h1 = _moe(x1, h2, ri, rf, cnt, xs0, mod3, wg, wu, wd, s_len, next_norm, g_final)
    return xc
```

```python
import functools
import math

import numpy as np
import jax
import jax.numpy as jnp
from jax import lax
from jax.experimental import pallas as pl
from jax.experimental.pallas import tpu as pltpu

GRID_W = 64
NA_HEADS = 4
NA_DIM = 128
NA_WIN_H = 8
NA_WIN_W = 16
DA_HEADS = 4
DA_DIM = 64
DA_VDIM = 2 * DA_DIM
ROPE_BASE = 10000.0
POOL_GROUPS = 4
POOL_WIDTH = 128
POOL_WINDOWS = (2, 4, 8, 16)
SG_GROUPS = 4
SG_WIDTH = 128
SG_CHUNK = 128
N_EXPERT_GROUPS = 4
EXPERTS_PER_GROUP = 4
TOP_K = 2
N_MOD = 6
EPS = 1e-6
NEG_INF = -1e30
LOG2_E = math.log2(math.e)

HEAD_W = 128
MIX_W = 512
ATT_COLS = 6 * MIX_W
REST_COLS = 3 * MIX_W
MOE_BLOCK = 256
NA_Q_ROWS = 4
NA_HEADS_PER_STEP = 4
DA_HEADS_PER_STEP = 2
ROW_TILE = 256
WIDE_ROW_TILE = 768
DA_KEY_CHUNKS = (1408, 768, 512, 256, 128)
DA_VT_ROWS = DA_VDIM + 16
VMEM_LIMIT = 56 * 1024 * 1024

F32 = jnp.float32
BF16 = jnp.bfloat16


def _cparams(n_axes, vmem=None):
    return pltpu.CompilerParams(dimension_semantics=("arbitrary",) * n_axes,
                                vmem_limit_bytes=vmem)


def _pick(n, candidates):
    for c in candidates:
        if n % c == 0:
            return c
    raise ValueError(f"no tile in {candidates} divides {n}")


def _ada_kernel(c_ref, w_ref, b_ref, o_ref):
    cs = jax.nn.silu(c_ref[...]).astype(BF16)
    o_ref[...] = jnp.dot(cs, w_ref[...].astype(BF16), preferred_element_type=F32) + b_ref[...]


def _ada_all(cond, w_ada, b_ada):
    depth, d, n6 = w_ada.shape
    tn = _pick(n6, (1024, 512, 256, 128))
    return pl.pallas_call(
        _ada_kernel,
        grid=(depth, n6 // tn),
        in_specs=[pl.BlockSpec((8, d), lambda l, j: (0, 0)),
                  pl.BlockSpec((None, d, tn), lambda l, j: (l, 0, j)),
                  pl.BlockSpec((None, 1, tn), lambda l, j: (l, 0, j))],
        out_specs=pl.BlockSpec((None, 8, tn), lambda l, j: (l, 0, j)),
        out_shape=jax.ShapeDtypeStruct((depth, 8, n6), F32),
        compiler_params=_cparams(2, VMEM_LIMIT),
        name="adaln",
    )(cond, w_ada, b_ada.reshape(depth, 1, n6))


def _row_is_ctx(i, tm, s_len):
    row = i * tm + lax.broadcasted_iota(jnp.int32, (tm, 1), 0)
    return row >= s_len


def _norm_mod(x, g, shx, scx, shc, scc, is_ctx):
    y = x * lax.rsqrt(jnp.mean(x * x, axis=-1, keepdims=True) + EPS) * g
    shift = jnp.where(is_ctx, shc, shx)
    scale = jnp.where(is_ctx, scc, scx)
    return y * (1.0 + scale) + shift


def _join_norm_kernel(x_ref, c_ref, g_ref, shx_ref, scx_ref, shc_ref, scc_ref, xc_ref, h_ref, *, s_len, tm):
    i = pl.program_id(1)
    is_ctx = _row_is_ctx(i, tm, s_len)
    x = jnp.where(i < s_len // tm, x_ref[...], c_ref[...])
    xc_ref[...] = x
    h = _norm_mod(x, g_ref[...], shx_ref[...], scx_ref[...], shc_ref[...], scc_ref[...], is_ctx)
    h_ref[...] = h.astype(h_ref.dtype)


def _join_norm_call(x, ctx, g, mod3):
    b, s_len, d = x.shape
    c_len = ctx.shape[1]
    n = s_len + c_len
    tm = _pick(math.gcd(s_len, c_len), (ROW_TILE, 128))
    n_lat = s_len // tm
    mx = lambda k: pl.BlockSpec((None, 1, d), lambda bi, i: (bi, 0, k))
    mc = lambda k: pl.BlockSpec((None, 1, d), lambda bi, i: (b, 0, k))
    row_spec = pl.BlockSpec((None, tm, d), lambda bi, i: (bi, i, 0))
    return pl.pallas_call(
        functools.partial(_join_norm_kernel, s_len=s_len, tm=tm),
        grid=(b, n // tm),
        in_specs=[pl.BlockSpec((None, tm, d), lambda bi, i: (bi, jnp.minimum(i, n_lat - 1), 0)),
                  pl.BlockSpec((None, tm, d), lambda bi, i: (bi, jnp.maximum(i - n_lat, 0), 0)),
                  pl.BlockSpec((1, d), lambda bi, i: (0, 0)),
                  mx(0), mx(1), mc(0), mc(1)],
        out_specs=[row_spec, row_spec],
        out_shape=[jax.ShapeDtypeStruct((b, n, d), F32), jax.ShapeDtypeStruct((b, n, d), BF16)],
        compiler_params=_cparams(2, VMEM_LIMIT),
        name="join_norm",
    )(x, ctx, g.reshape(1, d), mod3, mod3, mod3, mod3)


def _mm_kernel(a_ref, w_ref, o_ref, wbf_ref):
    @pl.when(pl.program_id(1) == 0)
    def _():
        wbf_ref[...] = w_ref[...].astype(BF16)

    o_ref[...] = jnp.dot(a_ref[...], wbf_ref[...], preferred_element_type=F32).astype(o_ref.dtype)


def _matmul_cols(a, w, layer, col0, ncols, out_dtype, tn):
    m, k = a.shape
    tm = _pick(m, (1536, 768, 512, 384, 256, 128))
    off = col0 // tn
    return pl.pallas_call(
        _mm_kernel,
        grid=(ncols // tn, m // tm),
        in_specs=[pl.BlockSpec((tm, k), lambda j, i: (i, 0)),
                  pl.BlockSpec((None, k, tn), lambda j, i: (layer, 0, j + off))],
        out_specs=pl.BlockSpec((tm, tn), lambda j, i: (i, j)),
        out_shape=jax.ShapeDtypeStruct((m, ncols), out_dtype),
        scratch_shapes=[pltpu.VMEM((k, tn), BF16)],
        compiler_params=_cparams(2, VMEM_LIMIT),
        name="in_proj",
    )(a, w)


def _rope_tables(s_len, c_len):
    t = jnp.arange(s_len)
    row = (t // GRID_W).astype(F32)
    col = (t % GRID_W).astype(F32)
    n_freq = DA_DIM // 4
    inv_freq = 1.0 / (ROPE_BASE ** (jnp.arange(n_freq, dtype=F32) / n_freq))
    ang = jnp.concatenate([row[:, None] * inv_freq, col[:, None] * inv_freq], axis=-1)
    cos = jnp.concatenate([jnp.cos(ang), jnp.ones((c_len, DA_DIM // 2), F32)], axis=0)
    sin = jnp.concatenate([jnp.sin(ang), jnp.zeros((c_len, DA_DIM // 2), F32)], axis=0)
    return jnp.tile(cos, (1, 4)), jnp.tile(jnp.concatenate([-sin, sin], axis=-1), (1, 2))


def _rope_kernel(q_ref, k_ref, v_ref, c_ref, s_ref, qt_ref, kr_ref, vt_ref, *, tm, q_scale):
    cos = c_ref[...]
    sin = s_ref[...]
    lane = lax.broadcasted_iota(jnp.int32, (tm, HEAD_W), 1)
    first_half = (lane % DA_DIM) < (DA_DIM // 2)

    def rotate(x):
        swapped = jnp.where(first_half, pltpu.roll(x, HEAD_W - DA_DIM // 2, 1), pltpu.roll(x, DA_DIM // 2, 1))
        return x * cos + swapped * sin

    for h in range(DA_HEADS):
        cols = slice(h * HEAD_W, (h + 1) * HEAD_W)
        q = rotate(q_ref[:, cols].astype(F32)) * q_scale
        qt_ref[cols, :] = q.T.astype(qt_ref.dtype)
        kr_ref[:, cols] = rotate(k_ref[:, cols].astype(F32)).astype(kr_ref.dtype)
        vt_ref[h * DA_VT_ROWS:h * DA_VT_ROWS + DA_VDIM, :] = v_ref[:, cols].astype(F32).T.astype(vt_ref.dtype)
        pad = lax.broadcasted_iota(jnp.int32, (DA_VT_ROWS - DA_VDIM, tm), 0)
        vt_ref[h * DA_VT_ROWS + DA_VDIM:(h + 1) * DA_VT_ROWS, :] = jnp.where(pad == 0, 1.0, 0.0).astype(vt_ref.dtype)


def _rope_call(pxa, cos, sin):
    b, n, _ = pxa.shape
    tm = _pick(n, (WIDE_ROW_TILE, ROW_TILE, 128))
    col_spec = lambda k: pl.BlockSpec((None, tm, MIX_W), lambda bi, i: (bi, i, k))
    t_spec = lambda rows: pl.BlockSpec((None, rows, tm), lambda bi, i: (bi, 0, i))
    return pl.pallas_call(
        functools.partial(_rope_kernel, tm=tm, q_scale=DA_DIM ** -0.5 * LOG2_E),
        grid=(b, n // tm),
        in_specs=[col_spec(3), col_spec(4), col_spec(5),
                  pl.BlockSpec((tm, HEAD_W), lambda bi, i: (i, 0)),
                  pl.BlockSpec((tm, HEAD_W), lambda bi, i: (i, 0))],
        out_specs=[t_spec(MIX_W), col_spec(0), t_spec(DA_HEADS * DA_VT_ROWS)],
        out_shape=[jax.ShapeDtypeStruct((b, MIX_W, n), BF16),
                   jax.ShapeDtypeStruct((b, n, MIX_W), BF16),
                   jax.ShapeDtypeStruct((b, DA_HEADS * DA_VT_ROWS, n), BF16)],
        compiler_params=_cparams(2, VMEM_LIMIT),
        name="rope",
    )(pxa, pxa, pxa, cos, sin)


def _na_plan(rows, r_q):
    kh = min(NA_WIN_H, rows)
    key_rows = r_q + kh - 1
    assert rows % r_q == 0 and rows >= key_rows
    patterns, var, ustart = [], [], []
    for j in range(rows // r_q):
        r = r_q * j + np.arange(r_q)
        rs = np.clip(r - kh // 2, 0, rows - kh)
        u = int(np.clip(rs.min(), 0, rows - key_rows))
        kr = u + np.arange(key_rows)
        valid = (kr[None, :] >= rs[:, None]) & (kr[None, :] < rs[:, None] + kh)
        assert (valid.sum(1) == kh).all()
        dr0 = u - r + (NA_WIN_H - 1)
        key = (valid.tobytes(), dr0.tobytes())
        for v, (k2, _, _) in enumerate(patterns):
            if k2 == key:
                break
        else:
            v = len(patterns)
            patterns.append((key, valid, dr0))
        var.append(v)
        ustart.append(u * GRID_W)
    valid = np.stack([p[1] for p in patterns])
    dr0 = np.stack([p[2] for p in patterns])
    return np.asarray(var, np.int32), np.asarray(ustart, np.int32), valid, dr0


def _na_bias_tables(rpb, valid, dr0):
    col = np.arange(GRID_W)
    cs = np.clip(col - NA_WIN_W // 2, 0, GRID_W - NA_WIN_W)
    cmask = (col[None, :] >= cs[:, None]) & (col[None, :] < cs[:, None] + NA_WIN_W)
    dc = np.clip(col[None, :] - col[:, None], -(NA_WIN_W - 1), NA_WIN_W - 1) + (NA_WIN_W - 1)
    heads, n_dr, n_dc = rpb.shape
    v, r_q, key_rows = valid.shape
    pick = (dc.reshape(1, -1) == np.arange(n_dc)[:, None]).astype(np.float32)
    by_col = jnp.dot(rpb.astype(F32).reshape(heads * n_dr, n_dc), pick,
                     precision=lax.Precision.HIGHEST).reshape(heads, n_dr, GRID_W, GRID_W)
    padded = jnp.pad(by_col, ((0, 0), (key_rows, key_rows), (0, 0), (0, 0)))
    tab = jnp.stack([lax.slice_in_dim(padded, int(d) + key_rows, int(d) + 2 * key_rows, axis=1)
                     for d in dr0.reshape(-1)], axis=1)
    tab = tab.reshape(heads, v, r_q, key_rows, GRID_W, GRID_W).transpose(0, 1, 2, 4, 3, 5)
    mask = valid[:, :, None, :, None] & cmask[None, None, :, None, :]
    tab = jnp.where(mask[None], tab, NEG_INF)
    return tab.reshape(heads, v, r_q * GRID_W, key_rows * GRID_W)


def _na_kernel(var_ref, u_ref, q_ref, k_ref, v_ref, kc_ref, vc_ref, bias_ref, o_ref, *, n_lat, kw, scale):
    j = pl.program_id(2)
    nt = (((1,), (1,)), ((), ()))

    def one_head(hh, local):
        cols = slice(hh * HEAD_W, (hh + 1) * HEAD_W)
        q = q_ref[:, cols]
        vc = vc_ref[:, cols]
        s_ctx = lax.dot_general(q, kc_ref[:, cols], nt, preferred_element_type=F32) * scale
        m = jnp.max(s_ctx, axis=-1, keepdims=True)
        if local:
            u = pl.multiple_of(u_ref[j], GRID_W)
            s = lax.dot_general(q, k_ref[pl.ds(u, kw), cols], nt, preferred_element_type=F32) * scale + bias_ref[hh]
            m = jnp.maximum(m, jnp.max(s, axis=-1, keepdims=True))
            p = jnp.exp(s - m)
        pc = jnp.exp(s_ctx - m)
        l = jnp.sum(pc, axis=-1, keepdims=True)
        o = jnp.dot(pc.astype(BF16), vc, preferred_element_type=F32)
        if local:
            l += jnp.sum(p, axis=-1, keepdims=True)
            o += jnp.dot(p.astype(BF16), v_ref[pl.ds(u, kw), cols], preferred_element_type=F32)
        o_ref[:, cols] = (o / l).astype(o_ref.dtype)

    @pl.when(j < n_lat)
    def _():
        for hh in range(NA_HEADS_PER_STEP):
            one_head(hh, True)

    @pl.when(j >= n_lat)
    def _():
        for hh in range(NA_HEADS_PER_STEP):
            one_head(hh, False)


def _na_call(pxa, rpb, s_len, c_len):
    b, n, _ = pxa.shape
    rows = s_len // GRID_W
    tq = NA_Q_ROWS * GRID_W
    hps = NA_HEADS_PER_STEP
    assert s_len % tq == 0 and c_len % tq == 0 and s_len % c_len == 0 and NA_HEADS % hps == 0
    var, ustart, valid, dr0 = _na_plan(rows, NA_Q_ROWS)
    n_lat, n_q = s_len // tq, n // tq
    kw = valid.shape[2] * GRID_W
    tabs = _na_bias_tables(rpb, valid, dr0)
    pad = np.zeros(n_q - n_lat, np.int32)
    var = jnp.asarray(np.concatenate([var, pad]))
    ustart = jnp.asarray(np.concatenate([ustart, pad]))
    ctx_blk = s_len // c_len
    wide = hps * HEAD_W
    k_blk, v_blk = NA_HEADS // hps, 2 * NA_HEADS // hps
    grid_spec = pltpu.PrefetchScalarGridSpec(
        num_scalar_prefetch=2,
        grid=(b, NA_HEADS // hps, n_q),
        in_specs=[pl.BlockSpec((None, tq, wide), lambda bi, h, j, vr, ur: (bi, j, h)),
                  pl.BlockSpec((None, s_len, wide), lambda bi, h, j, vr, ur: (bi, 0, k_blk + h)),
                  pl.BlockSpec((None, s_len, wide), lambda bi, h, j, vr, ur: (bi, 0, v_blk + h)),
                  pl.BlockSpec((None, c_len, wide), lambda bi, h, j, vr, ur: (bi, ctx_blk, k_blk + h)),
                  pl.BlockSpec((None, c_len, wide), lambda bi, h, j, vr, ur: (bi, ctx_blk, v_blk + h)),
                  pl.BlockSpec((hps, None, tq, kw), lambda bi, h, j, vr, ur: (h, vr[j], 0, 0))],
        out_specs=pl.BlockSpec((None, tq, wide), lambda bi, h, j, vr, ur: (bi, j, h)),
    )
    return pl.pallas_call(
        functools.partial(_na_kernel, n_lat=n_lat, kw=kw, scale=NA_DIM ** -0.5),
        grid_spec=grid_spec,
        out_shape=jax.ShapeDtypeStruct((b, n, MIX_W), BF16),
        compiler_params=_cparams(3, VMEM_LIMIT),
        name="nbr_attn",
    )(var, ustart, pxa, pxa, pxa, pxa, pxa, tabs)


def _cast_expert_weights(step, n_steps, layer, srcs, dsts, in_g, in_u, in_d, out_g, out_u, out_d, sem_in, sem_out):
    ins, outs = (in_g, in_u, in_d), (out_g, out_u, out_d)
    n_chunks = srcs[0].shape[1] // in_g.shape[1]
    cur = jnp.minimum(step, n_chunks - 1)
    nxt = jnp.minimum(step + 1, n_chunks - 1)
    slot = step % 2
    other = 1 - slot

    def in_copy(t, chunk, sl):
        rows = ins[t].shape[1]
        src = srcs[t].at[layer, pl.ds(pl.multiple_of(chunk * rows, rows), rows)]
        return pltpu.make_async_copy(src, ins[t].at[sl], sem_in.at[sl])

    def out_copy(t, chunk, sl):
        rows = outs[t].shape[1]
        dst = dsts[t].at[pl.ds(pl.multiple_of(chunk * rows, rows), rows)]
        return pltpu.make_async_copy(outs[t].at[sl], dst, sem_out.at[sl])

    @pl.when(step == 0)
    def _():
        for t in range(3):
            in_copy(t, cur, slot).start()

    for t in range(3):
        in_copy(t, nxt, other).start()
    for t in range(3):
        in_copy(t, cur, slot).wait()

    @pl.when(step > 0)
    def _():
        for t in range(3):
            out_copy(t, cur, other).wait()

    for t in range(3):
        outs[t][slot] = ins[t][slot].astype(BF16)
    for t in range(3):
        out_copy(t, cur, slot).start()

    @pl.when(step == n_steps - 1)
    def _():
        for t in range(3):
            out_copy(t, cur, slot).wait()
            in_copy(t, nxt, other).wait()


def _zero_fill(step, n_steps, dst_ref, zero_ref, sem):
    rows = zero_ref.shape[0]
    chunk = jnp.minimum(step, dst_ref.shape[0] // rows - 1)
    fill = pltpu.make_async_copy(zero_ref, dst_ref.at[pl.ds(pl.multiple_of(chunk * rows, rows), rows)], sem)

    @pl.when(step == 0)
    def _():
        zero_ref[...] = jnp.zeros_like(zero_ref)

    @pl.when(step > 0)
    def _():
        fill.wait()

    fill.start()

    @pl.when(step == n_steps - 1)
    def _():
        fill.wait()


def _da_kernel(qt_ref, k_ref, vt_ref, lam_ref, g_ref, wg_ref, wu_ref, wd_ref, o_ref, wgb_ref, wub_ref, wdb_ref,
               xs0_ref, *scratch, s_len, c_len, tq, tk, lam_init, heads, layer):
    score_refs, cast_refs, fill_refs = scratch[:2 * heads], scratch[2 * heads:-2], scratch[-2:]
    sa_refs, sb_refs = score_refs[0::2], score_refs[1::2]
    qi = pl.program_id(2)
    step = (pl.program_id(0) * pl.num_programs(1) + pl.program_id(1)) * pl.num_programs(2) + qi
    n_steps = pl.num_programs(0) * pl.num_programs(1) * pl.num_programs(2)
    _cast_expert_weights(step, n_steps, layer, (wg_ref, wu_ref, wd_ref), (wgb_ref, wub_ref, wdb_ref), *cast_refs)
    _zero_fill(step, n_steps, xs0_ref, *fill_refs)
    n = s_len + c_len
    feat = lax.broadcasted_iota(jnp.int32, (2 * DA_DIM, tq), 0)

    def q_tiles(idx):
        start = pl.multiple_of(idx * tq, tq)
        out = []
        for hh in range(heads):
            qt = qt_ref[hh * HEAD_W:(hh + 1) * HEAD_W, pl.ds(start, tq)]
            zero = jnp.zeros_like(qt)
            out.append(jnp.concatenate([jnp.where(feat < DA_DIM, qt, zero), jnp.where(feat >= DA_DIM, qt, zero)],
                                       axis=1))
        return out

    q2 = q_tiles(qi)

    def k_rows(hh, rows):
        return k_ref[rows, hh * HEAD_W:(hh + 1) * HEAD_W]

    def vt_cols(hh, cols):
        return vt_ref[hh * DA_VT_ROWS:(hh + 1) * DA_VT_ROWS, cols]

    def scores(k_blk, q):
        return jnp.dot(k_blk, q, preferred_element_type=F32)

    def update(s, vt_blk, carry):
        m, acc = carry
        m_new = jnp.maximum(m, jnp.max(s, axis=0, keepdims=True))
        alpha = jnp.exp2(m - m_new)
        p = jnp.exp2(s - m_new)
        acc_new = alpha * acc + jnp.dot(vt_blk, p.astype(BF16), preferred_element_type=F32)
        return m_new, acc_new

    def init():
        one = (jnp.full((1, 2 * tq), NEG_INF, F32), jnp.zeros((DA_VT_ROWS, 2 * tq), F32))
        return tuple(one for _ in range(heads))

    def finalize(stats):
        lv = lam_ref[...]
        t1 = jnp.sum(lv[0:1, :] * lv[1:2, :], axis=-1, keepdims=True)
        t2 = jnp.sum(lv[2:3, :] * lv[3:4, :], axis=-1, keepdims=True)
        lam = jnp.exp(t1) - jnp.exp(t2) + lam_init
        for hh in range(heads):
            _, acc = stats[hh]
            o = acc[:DA_VDIM, :] / acc[DA_VDIM:DA_VDIM + 1, :]
            o = o[:, :tq] - lam * o[:, tq:]
            y = o * lax.rsqrt(jnp.mean(o * o, axis=0, keepdims=True) + EPS) * g_ref[...]
            o_ref[:, hh * HEAD_W:(hh + 1) * HEAD_W] = (y * (1.0 - lam_init)).T.astype(o_ref.dtype)

    k_steps = n // tk
    full_pairs = (k_steps - 1) // 2
    carry_first = k_steps % 2 == 0

    @pl.when(qi < s_len // tq)
    def _():
        def scores_into(dst_refs, step, qs):
            rows = pl.ds(pl.multiple_of(step * tk, tk), tk)
            for hh in range(heads):
                dst_refs[hh][...] = scores(k_rows(hh, rows), qs[hh])

        def update_from(src_refs, step, stats):
            cols = pl.ds(pl.multiple_of(step * tk, tk), tk)
            return tuple(update(src_refs[hh][...], vt_cols(hh, cols), stats[hh]) for hh in range(heads))

        def pair(j, stats):
            c0 = 2 * j
            scores_into(sb_refs, c0 + 1, q2)
            stats = update_from(sa_refs, c0, stats)
            scores_into(sa_refs, c0 + 2, q2)
            return update_from(sb_refs, c0 + 1, stats)

        if carry_first:
            @pl.when(qi == 0)
            def _():
                scores_into(sa_refs, 0, q2)
        else:
            scores_into(sa_refs, 0, q2)
        stats = lax.fori_loop(0, full_pairs, pair, init())
        c0 = 2 * full_pairs
        if carry_first:
            scores_into(sb_refs, c0 + 1, q2)
            stats = update_from(sa_refs, c0, stats)
            scores_into(sa_refs, 0, q_tiles(qi + 1))
            stats = update_from(sb_refs, c0 + 1, stats)
        else:
            stats = update_from(sa_refs, c0, stats)
        finalize(stats)

    @pl.when(qi >= s_len // tq)
    def _():
        rows, stats = slice(s_len, n), init()
        finalize(tuple(update(scores(k_rows(hh, rows), q2[hh]), vt_cols(hh, rows), stats[hh])
                       for hh in range(heads)))


def _da_call(qt, kr, vt, lam_vecs, subln_g, exp_w, n_dispatch, s_len, c_len, layer):
    b, n, _ = kr.shape
    tq = _pick(math.gcd(s_len, c_len), (256, 128))
    tk = _pick(n, DA_KEY_CHUNKS)
    hps = DA_HEADS_PER_STEP
    assert DA_HEADS % hps == 0
    lam_init = 0.8 - 0.6 * math.exp(-0.3 * layer)
    grid = (b, DA_HEADS // hps, n // tq)
    n_chunks = 1 << (math.prod(grid).bit_length() - 1)
    flat = [w.reshape(w.shape[0], w.shape[1] * w.shape[2], w.shape[3]) for w in exp_w]
    chunk = [(w.shape[1] // n_chunks, w.shape[2]) for w in flat]
    assert all(w.shape[1] % n_chunks == 0 and r % 16 == 0 for w, (r, _) in zip(flat, chunk))
    d_model = exp_w[0].shape[2]
    fill_rows = n_dispatch // n_chunks
    assert n_dispatch % n_chunks == 0 and fill_rows % 8 == 0
    once = pl.Buffered(1)
    any_spec = pl.BlockSpec(memory_space=pl.ANY)
    outs = pl.pallas_call(
        functools.partial(_da_kernel, s_len=s_len, c_len=c_len, tq=tq, tk=tk, lam_init=lam_init, heads=hps,
                          layer=layer),
        grid=grid,
        in_specs=[pl.BlockSpec((None, hps * HEAD_W, n), lambda bi, h, i: (bi, h, 0), pipeline_mode=once),
                  pl.BlockSpec((None, n, hps * HEAD_W), lambda bi, h, i: (bi, 0, h), pipeline_mode=once),
                  pl.BlockSpec((None, hps * DA_VT_ROWS, n), lambda bi, h, i: (bi, h, 0), pipeline_mode=once),
                  pl.BlockSpec((4, DA_DIM), lambda bi, h, i: (0, 0)),
                  pl.BlockSpec((DA_VDIM, 1), lambda bi, h, i: (0, 0)),
                  any_spec, any_spec, any_spec],
        out_specs=[pl.BlockSpec((None, tq, hps * HEAD_W), lambda bi, h, i: (bi, i, h)),
                   any_spec, any_spec, any_spec, any_spec],
        out_shape=[jax.ShapeDtypeStruct((b, n, MIX_W), BF16)]
                  + [jax.ShapeDtypeStruct(w.shape[1:], BF16) for w in flat]
                  + [jax.ShapeDtypeStruct((n_dispatch, d_model), F32)],
        scratch_shapes=[pltpu.VMEM((tk, 2 * tq), F32)] * (2 * hps)
                       + [pltpu.VMEM((2,) + c, F32) for c in chunk]
                       + [pltpu.VMEM((2,) + c, BF16) for c in chunk]
                       + [pltpu.SemaphoreType.DMA((2,)), pltpu.SemaphoreType.DMA((2,))]
                       + [pltpu.VMEM((fill_rows, d_model), F32), pltpu.SemaphoreType.DMA],
        compiler_params=_cparams(3, VMEM_LIMIT),
        name="diff_attn",
    )(qt, kr, vt, lam_vecs, subln_g.reshape(DA_VDIM, 1), *flat)
    return outs[0], tuple(o.reshape(w.shape[1:]) for o, w in zip(outs[1:4], exp_w)), outs[4]


POOL_HALO = 8


def _pool_kernel(prev_ref, cur_ref, next_ref, w_ref, sc_ref, o_ref, buf_ref, *, s_len, c_len, tm):
    i = pl.program_id(1)
    n_lat = s_len // tm
    n_all = (s_len + c_len) // tm
    seq_start = (i == 0) | (i == n_lat)
    seq_end = (i == n_lat - 1) | (i == n_all - 1)
    buf_ref[0:POOL_HALO, :] = jnp.where(seq_start, 0.0, prev_ref[...])
    buf_ref[POOL_HALO:POOL_HALO + tm, :] = cur_ref[...]
    buf_ref[POOL_HALO + tm:, :] = jnp.where(seq_end, 0.0, next_ref[...])
    in_ctx = i >= n_lat
    seq_len = jnp.where(in_ctx, c_len, s_len)
    pos = i * tm - jnp.where(in_ctx, s_len, 0) + lax.broadcasted_iota(jnp.int32, (tm, 1), 0)
    for g, win in enumerate(POOL_WINDOWS):
        half = win // 2
        cols = slice(g * POOL_WIDTH, (g + 1) * POOL_WIDTH)
        tot = buf_ref[POOL_HALO - half:POOL_HALO - half + tm, cols]
        for d in range(-half + 1, half):
            tot = tot + buf_ref[POOL_HALO + d:POOL_HALO + d + tm, cols]
        cnt = jnp.clip(pos + half, 0, seq_len) - jnp.clip(pos - half, 0, seq_len)
        resid = tot / cnt.astype(F32) - cur_ref[:, cols]
        y = jnp.dot(resid.astype(BF16), w_ref[g].astype(BF16), preferred_element_type=F32)
        o_ref[:, cols] = (y * sc_ref[:, cols]).astype(o_ref.dtype)


def _pool_call(pxb, pool_w, pool_scale, s_len, c_len):
    b, n, _ = pxb.shape
    tm = _pick(math.gcd(s_len, c_len), (ROW_TILE, 128))
    hb = tm // POOL_HALO
    last = n // POOL_HALO - 1
    return pl.pallas_call(
        functools.partial(_pool_kernel, s_len=s_len, c_len=c_len, tm=tm),
        grid=(b, n // tm),
        in_specs=[pl.BlockSpec((None, POOL_HALO, MIX_W), lambda bi, i: (bi, jnp.maximum(i * hb - 1, 0), 0)),
                  pl.BlockSpec((None, tm, MIX_W), lambda bi, i: (bi, i, 0)),
                  pl.BlockSpec((None, POOL_HALO, MIX_W), lambda bi, i: (bi, jnp.minimum((i + 1) * hb, last), 0)),
                  pl.BlockSpec((POOL_GROUPS, POOL_WIDTH, POOL_WIDTH), lambda bi, i: (0, 0, 0)),
                  pl.BlockSpec((1, MIX_W), lambda bi, i: (0, 0))],
        out_specs=pl.BlockSpec((None, tm, MIX_W), lambda bi, i: (bi, i, 0)),
        out_shape=jax.ShapeDtypeStruct((b, n, MIX_W), BF16),
        scratch_shapes=[pltpu.VMEM((tm + 2 * POOL_HALO, MIX_W), F32)],
        compiler_params=_cparams(2, VMEM_LIMIT),
        name="pool",
    )(pxb, pxb, pxb, pool_w, pool_scale.reshape(1, MIX_W))


def _sg_kernel(u_ref, v_ref, g_ref, w_ref, b_ref, o_ref, *, tm):
    v = jax.nn.gelu(v_ref[...])
    vn = (v * lax.rsqrt(jnp.mean(v * v, axis=-1, keepdims=True) + EPS) * g_ref[...]).astype(BF16)
    bias = b_ref[...]
    for c in range(tm // SG_CHUNK):
        rows = slice(c * SG_CHUNK, (c + 1) * SG_CHUNK)
        for g in range(SG_GROUPS):
            cols = slice(g * SG_WIDTH, (g + 1) * SG_WIDTH)
            mixed = jnp.dot(w_ref[g].astype(BF16), vn[rows, cols], preferred_element_type=F32)
            mixed = mixed + bias[:, g:g + 1]
            o_ref[rows, cols] = (jax.nn.gelu(u_ref[rows, cols]) * mixed).astype(o_ref.dtype)


def _sg_call(pxb, sg_norm_g, sg_w, sg_b):
    b, n, _ = pxb.shape
    tm = _pick(n, (WIDE_ROW_TILE, ROW_TILE, 128))
    return pl.pallas_call(
        functools.partial(_sg_kernel, tm=tm),
        grid=(b, n // tm),
        in_specs=[pl.BlockSpec((None, tm, MIX_W), lambda bi, i: (bi, i, 1)),
                  pl.BlockSpec((None, tm, MIX_W), lambda bi, i: (bi, i, 2)),
                  pl.BlockSpec((1, MIX_W), lambda bi, i: (0, 0)),
                  pl.BlockSpec((SG_GROUPS, SG_CHUNK, SG_CHUNK), lambda bi, i: (0, 0, 0)),
                  pl.BlockSpec((SG_CHUNK, SG_GROUPS), lambda bi, i: (0, 0))],
        out_specs=pl.BlockSpec((None, tm, MIX_W), lambda bi, i: (bi, i, 0)),
        out_shape=jax.ShapeDtypeStruct((b, n, MIX_W), BF16),
        compiler_params=_cparams(2, VMEM_LIMIT),
        name="spatial_gate",
    )(pxb, pxb, sg_norm_g.reshape(1, MIX_W), sg_w, sg_b.T)


def _first_argmax(vals):
    best, idx = vals[0], jnp.zeros(vals[0].shape, jnp.int32)
    for k in range(1, len(vals)):
        take = vals[k] > best
        best = jnp.where(take, vals[k], best)
        idx = jnp.where(take, k, idx)
    return idx, best


def _out_router_kernel(ma_ref, mb_ref, mc_ref, md_ref, w_ref, x_ref, gx_ref, gc_ref,
                       g_ref, shx_ref, scx_ref, shc_ref, scc_ref, rw_ref, rb_ref,
                       x1_ref, h_ref, ri_ref, rf_ref, cnt_ref, run_ref, *, s_len, tm, n_exp):
    bi, i = pl.program_id(0), pl.program_id(1)

    @pl.when((bi == 0) & (i == 0))
    def _():
        run_ref[...] = jnp.zeros_like(run_ref)

    mix = jnp.concatenate([ma_ref[...], mb_ref[...], mc_ref[...], md_ref[...]], axis=1)
    acc = jnp.dot(mix, w_ref[...], preferred_element_type=F32)
    is_ctx = _row_is_ctx(i, tm, s_len)
    x1 = x_ref[...] + jnp.where(is_ctx, gc_ref[...], gx_ref[...]) * acc
    x1_ref[...] = x1
    h = _norm_mod(x1, g_ref[...], shx_ref[...], scx_ref[...], shc_ref[...], scc_ref[...], is_ctx)
    h_ref[...] = h

    h_hi = h.astype(BF16)
    h_lo = (h - h_hi.astype(F32)).astype(BF16)
    part = jnp.dot(h_hi, rw_ref[0], preferred_element_type=F32)
    part += jnp.dot(h_lo, rw_ref[1], preferred_element_type=F32)
    part_t = part.T
    logits = part_t[0:n_exp, :] + part_t[n_exp:2 * n_exp, :]
    scores = jax.nn.sigmoid(logits)
    biased = scores + rb_ref[...]
    b_rows = [biased[e:e + 1, :] for e in range(n_exp)]
    s_rows = [scores[e:e + 1, :] for e in range(n_exp)]
    epg = EXPERTS_PER_GROUP
    group_scores = []
    for g in range(N_EXPERT_GROUPS):
        r = b_rows[g * epg:(g + 1) * epg]
        pair = None
        for a in range(epg):
            for c in range(a + 1, epg):
                t = r[a] + r[c]
                pair = t if pair is None else jnp.maximum(pair, t)
        group_scores.append(pair)
    group, _ = _first_argmax(group_scores)
    in_group = []
    for k in range(epg):
        sel = b_rows[k]
        for g in range(1, N_EXPERT_GROUPS):
            sel = jnp.where(group == g, b_rows[g * epg + k], sel)
        in_group.append(sel)
    loc0, _ = _first_argmax(in_group)
    rest = [jnp.where(loc0 == k, -jnp.inf, in_group[k]) for k in range(epg)]
    loc1, _ = _first_argmax(rest)
    e0 = group * epg + loc0
    e1 = group * epg + loc1
    g0 = jnp.zeros_like(s_rows[0])
    g1 = jnp.zeros_like(s_rows[0])
    for e in range(n_exp):
        g0 = jnp.where(e0 == e, s_rows[e], g0)
        g1 = jnp.where(e1 == e, s_rows[e], g1)
    tot = g0 + g1

    eidx = lax.broadcasted_iota(jnp.int32, (n_exp, tm), 0)
    hit0 = eidx == e0
    hit1 = eidx == e1
    onehot = jnp.where(hit0 | hit1, 1.0, 0.0)
    before = lax.broadcasted_iota(jnp.int32, (tm, tm), 0) < lax.broadcasted_iota(jnp.int32, (tm, tm), 1)
    tri = jnp.where(before, 1.0, 0.0).astype(BF16)
    run = run_ref[:, 0:1]
    rank = jnp.dot(onehot.astype(BF16), tri, preferred_element_type=F32) + run
    rank0 = jnp.sum(jnp.where(hit0, rank, 0.0), axis=0, keepdims=True)
    rank1 = jnp.sum(jnp.where(hit1, rank, 0.0), axis=0, keepdims=True)
    new_run = run + jnp.sum(onehot, axis=1, keepdims=True)
    run_ref[...] = jnp.broadcast_to(new_run, run_ref.shape)
    cnt_ref[...] = jnp.broadcast_to(new_run, cnt_ref.shape).astype(jnp.int32)

    zi = jnp.zeros((4, tm), jnp.int32)
    ri_ref[...] = jnp.concatenate([e0, e1, rank0.astype(jnp.int32), rank1.astype(jnp.int32), zi], axis=0)
    zf = jnp.zeros((6, tm), F32)
    rf_ref[...] = jnp.concatenate([g0 / tot, g1 / tot, zf], axis=0)


def _router_slabs(router_w):
    d, n_exp = router_w.shape
    w_hi = router_w.astype(BF16)
    w_lo = (router_w - w_hi.astype(F32)).astype(BF16)
    zeros = lambda cols: jnp.zeros((d, cols), BF16)
    return jnp.stack([jnp.concatenate([w_hi, w_lo, zeros(HEAD_W - 2 * n_exp)], axis=1),
                      jnp.concatenate([w_hi, zeros(HEAD_W - n_exp)], axis=1)])


def _out_router_call(mixes, w_out_bf, layer, xc, g2, mod3, router_w, router_b, s_len):
    b, n, d = xc.shape
    n_exp = router_w.shape[1]
    tm = _pick(n, (ROW_TILE, 128))
    per_b = n // tm
    mx = lambda k: pl.BlockSpec((None, 1, d), lambda bi, i: (bi, 0, k))
    mc = lambda k: pl.BlockSpec((None, 1, d), lambda bi, i: (b, 0, k))
    mix_spec = pl.BlockSpec((None, tm, MIX_W), lambda bi, i: (bi, i, 0))
    row_spec = pl.BlockSpec((None, tm, d), lambda bi, i: (bi, i, 0))
    tok_spec = pl.BlockSpec((8, tm), lambda bi, i: (0, bi * per_b + i))
    return pl.pallas_call(
        functools.partial(_out_router_kernel, s_len=s_len, tm=tm, n_exp=n_exp),
        grid=(b, per_b),
        in_specs=[mix_spec, mix_spec, mix_spec, mix_spec,
                  pl.BlockSpec((None, 4 * MIX_W, d), lambda bi, i: (layer, 0, 0), pipeline_mode=pl.Buffered(1)),
                  row_spec, mx(2), mc(2),
                  pl.BlockSpec((1, d), lambda bi, i: (0, 0)),
                  mx(3), mx(4), mc(3), mc(4),
                  pl.BlockSpec((2, d, HEAD_W), lambda bi, i: (0, 0, 0)),
                  pl.BlockSpec((n_exp, 1), lambda bi, i: (0, 0))],
        out_specs=[row_spec, row_spec, tok_spec, tok_spec,
                   pl.BlockSpec((n_exp, HEAD_W), lambda bi, i: (0, 0))],
        out_shape=[jax.ShapeDtypeStruct((b, n, d), F32),
                   jax.ShapeDtypeStruct((b, n, d), F32),
                   jax.ShapeDtypeStruct((8, b * n), jnp.int32),
                   jax.ShapeDtypeStruct((8, b * n), F32),
                   jax.ShapeDtypeStruct((n_exp, HEAD_W), jnp.int32)],
        scratch_shapes=[pltpu.VMEM((n_exp, HEAD_W), F32)],
        compiler_params=_cparams(2, VMEM_LIMIT),
        name="out_router",
    )(*mixes, w_out_bf, xc, mod3, mod3, g2.reshape(1, d), mod3, mod3, mod3, mod3,
      _router_slabs(router_w), router_b.reshape(n_exp, 1))


def _row_copy(src_ref, src_row, dst_ref, dst_row, sem):
    return pltpu.make_async_copy(src_ref.at[pl.ds(src_row, 1)], dst_ref.at[pl.ds(dst_row, 1)], sem)


def _dispatch_kernel(p0_ref, p1_ref, h_ref, xs_in_ref, xs_ref, sem, *, tm):
    del xs_in_ref
    base = pl.program_id(0) * tm

    for r in range(tm):
        _row_copy(h_ref, r, xs_ref, p0_ref[base + r], sem).start()
        _row_copy(h_ref, r, xs_ref, p1_ref[base + r], sem).start()

    tile_copy = pltpu.make_async_copy(h_ref, xs_ref.at[pl.ds(0, tm)], sem)
    tile_copy.wait()
    tile_copy.wait()


def _dispatch_call(h2, pos0, pos1, xs0):
    t, d = h2.shape
    n_pad = xs0.shape[0]
    tm = _pick(t, (ROW_TILE, 128))
    grid_spec = pltpu.PrefetchScalarGridSpec(
        num_scalar_prefetch=2,
        grid=(t // tm,),
        in_specs=[pl.BlockSpec((tm, d), lambda i, p0, p1: (i, 0)),
                  pl.BlockSpec(memory_space=pl.ANY)],
        out_specs=pl.BlockSpec(memory_space=pl.ANY),
        scratch_shapes=[pltpu.SemaphoreType.DMA],
    )
    return pl.pallas_call(
        functools.partial(_dispatch_kernel, tm=tm),
        grid_spec=grid_spec,
        out_shape=jax.ShapeDtypeStruct((n_pad, d), h2.dtype),
        input_output_aliases={3: 0},
        compiler_params=_cparams(1, VMEM_LIMIT),
        name="dispatch",
    )(pos0, pos1, h2, xs0)


def _ffn_kernel(be_ref, nb_ref, x_ref, wg_ref, wu_ref, wd_ref, y_ref):
    blk = pl.program_id(0)

    @pl.when(blk < nb_ref[0])
    def _():
        x = x_ref[...].astype(BF16)
        gate = jnp.dot(x, wg_ref[...], preferred_element_type=F32)
        up = jnp.dot(x, wu_ref[...], preferred_element_type=F32)
        hid = (jax.nn.silu(gate) * up).astype(BF16)
        y_ref[...] = jnp.dot(hid, wd_ref[...], preferred_element_type=F32)

    @pl.when(blk >= nb_ref[0])
    def _():
        y_ref[...] = jnp.zeros_like(y_ref)


def _ffn_call(xs, blk_expert, n_used, wg, wu, wd):
    n_pad, d = xs.shape
    d_exp = wg.shape[2]
    grid_spec = pltpu.PrefetchScalarGridSpec(
        num_scalar_prefetch=2,
        grid=(n_pad // MOE_BLOCK,),
        in_specs=[pl.BlockSpec((MOE_BLOCK, d), lambda i, be, nb: (i, 0)),
                  pl.BlockSpec((None, d, d_exp), lambda i, be, nb: (be[i], 0, 0)),
                  pl.BlockSpec((None, d, d_exp), lambda i, be, nb: (be[i], 0, 0)),
                  pl.BlockSpec((None, d_exp, d), lambda i, be, nb: (be[i], 0, 0))],
        out_specs=pl.BlockSpec((MOE_BLOCK, d), lambda i, be, nb: (i, 0)),
    )
    return pl.pallas_call(
        _ffn_kernel,
        grid_spec=grid_spec,
        out_shape=jax.ShapeDtypeStruct((n_pad, d), F32),
        compiler_params=_cparams(1, VMEM_LIMIT),
        name="expert_ffn",
    )(blk_expert, n_used, xs, wg, wu, wd)


def _combine_kernel(p0_ref, p1_ref, ys_ref, x_ref, gt_ref, gx_ref, gc_ref, *rest, s_len, n_len, tm, with_next):
    if with_next:
        g_ref, shx_ref, scx_ref, shc_ref, scc_ref, o_ref, h_ref, y0_ref, y1_ref, sem = rest
    else:
        g_ref, o_ref, y0_ref, y1_ref, sem = rest
    i = pl.program_id(0)
    last = pl.num_programs(0) - 1

    def start_tile(tile, slot):
        base = tile * tm
        for r in range(tm):
            _row_copy(ys_ref, p0_ref[base + r], y0_ref.at[slot], r, sem.at[slot]).start()
            _row_copy(ys_ref, p1_ref[base + r], y1_ref.at[slot], r, sem.at[slot]).start()

    def wait_tile(slot):
        pltpu.make_async_copy(ys_ref.at[pl.ds(0, tm)], y0_ref.at[slot], sem.at[slot]).wait()
        pltpu.make_async_copy(ys_ref.at[pl.ds(0, tm)], y1_ref.at[slot], sem.at[slot]).wait()

    def combine(slot):
        row = ((i * tm) % n_len) + lax.broadcasted_iota(jnp.int32, (tm, 1), 0)
        is_ctx = row >= s_len
        gate = jnp.where(is_ctx, gc_ref[...], gx_ref[...])
        gt = gt_ref[...]
        y = gt[:, 0:1] * y0_ref[slot] + gt[:, 1:2] * y1_ref[slot]
        x_new = x_ref[...] + gate * y
        if with_next:
            o_ref[...] = x_new
            h = _norm_mod(x_new, g_ref[...], shx_ref[...], scx_ref[...], shc_ref[...], scc_ref[...], is_ctx)
            h_ref[...] = h.astype(h_ref.dtype)
        else:
            @pl.when((i * tm) % n_len < s_len)
            def _():
                o_ref[...] = x_new * lax.rsqrt(jnp.mean(x_new * x_new, axis=-1, keepdims=True) + EPS) * g_ref[...]

    @pl.when(i == 0)
    def _():
        start_tile(0, 0)

    for slot in (0, 1):
        @pl.when(i % 2 == slot)
        def _(slot=slot):
            wait_tile(slot)
            start_tile(jnp.minimum(i + 1, last), 1 - slot)
            combine(slot)

            @pl.when(i == last)
            def _():
                wait_tile(1 - slot)


def _combine_call(ys, pos0, pos1, x1, gates, mod3, s_len, k_gate, next_norm, final_g):
    b, n, d = x1.shape
    t = b * n
    tm = _pick(n, (ROW_TILE, 128))
    row_spec = pl.BlockSpec((tm, d), lambda i, p0, p1: (i, 0))
    mx = lambda k: pl.BlockSpec((None, 1, d), lambda i, p0, p1: ((i * tm) // n, 0, k))
    mc = lambda k: pl.BlockSpec((None, 1, d), lambda i, p0, p1: (b, 0, k))
    in_specs = [pl.BlockSpec(memory_space=pl.ANY), row_spec,
                pl.BlockSpec((tm, TOP_K), lambda i, p0, p1: (i, 0)), mx(k_gate), mc(k_gate)]
    args = [pos0, pos1, ys, x1.reshape(t, d), gates, mod3, mod3]
    if next_norm is None:
        per_b, lat = n // tm, s_len // tm
        in_specs.append(pl.BlockSpec((1, d), lambda i, p0, p1: (0, 0)))
        args.append(final_g.reshape(1, d))
        out_specs = [pl.BlockSpec((tm, d), lambda i, p0, p1: ((i // per_b) * lat + jnp.minimum(i % per_b, lat - 1), 0))]
        out_shape = [jax.ShapeDtypeStruct((b * s_len, d), F32)]
    else:
        out_specs, out_shape = [row_spec], [jax.ShapeDtypeStruct((t, d), F32)]
    if next_norm is not None:
        g_next, mod_next = next_norm
        in_specs += [pl.BlockSpec((1, d), lambda i, p0, p1: (0, 0)), mx(0), mx(1), mc(0), mc(1)]
        args += [g_next.reshape(1, d), mod_next, mod_next, mod_next, mod_next]
        out_specs.append(row_spec)
        out_shape.append(jax.ShapeDtypeStruct((t, d), BF16))
    grid_spec = pltpu.PrefetchScalarGridSpec(
        num_scalar_prefetch=2,
        grid=(t // tm,),
        in_specs=in_specs,
        out_specs=out_specs,
        scratch_shapes=[pltpu.VMEM((2, tm, d), F32), pltpu.VMEM((2, tm, d), F32), pltpu.SemaphoreType.DMA((2,))],
    )
    outs = pl.pallas_call(
        functools.partial(_combine_kernel, s_len=s_len, n_len=n, tm=tm, with_next=next_norm is not None),
        grid_spec=grid_spec,
        out_shape=out_shape,
        compiler_params=_cparams(1, VMEM_LIMIT),
        name="combine",
    )(*args)
    if next_norm is None:
        return outs[0].reshape(b, s_len, d), None
    return outs[0].reshape(b, n, d), outs[1]


def _moe(x1, h2, ri, rf, cnt, xs0, mod3, wg, wu, wd, s_len, next_norm, final_g):
    b, n, d = x1.shape
    t = b * n
    counts = cnt[:, 0]
    n_blk = (counts + MOE_BLOCK - 1) // MOE_BLOCK
    blk_end = jnp.cumsum(n_blk)
    starts = (blk_end - n_blk) * MOE_BLOCK
    pos0 = (starts[ri[0]] + ri[2]).astype(jnp.int32)
    pos1 = (starts[ri[1]] + ri[3]).astype(jnp.int32)
    n_pad = xs0.shape[0]
    n_used = blk_end[-1:].astype(jnp.int32)
    blk_ids = jnp.minimum(jnp.arange(n_pad // MOE_BLOCK, dtype=jnp.int32), n_used[0] - 1)
    blk_expert = jnp.sum(blk_ids[:, None] >= blk_end[None, :], axis=1).astype(jnp.int32)
    xs = _dispatch_call(h2.reshape(t, d), pos0, pos1, xs0)
    ys = _ffn_call(xs, blk_expert, n_used, wg, wu, wd)
    return _combine_call(ys, pos0, pos1, x1, rf[:TOP_K].T, mod3, s_len, 5, next_norm, final_g)


def kernel(x, c, ctx, c_ctx, w_ada, b_ada, g_norm1, g_norm2, w_in, w_out, na_rpb, da_lambda, da_subln_g,
           pool_w, pool_scale, sg_norm_g, sg_w, sg_b, router_w, router_b, exp_w_gate, exp_w_up, exp_w_down,
           g_final):
    b, s_len, d = x.shape
    c_len = ctx.shape[1]
    n = s_len + c_len
    depth = w_ada.shape[0]
    assert b + 1 <= 8 and w_in.shape[2] == ATT_COLS + REST_COLS and w_out.shape[1] == 4 * MIX_W

    cond = jnp.zeros((8, d), F32).at[:b].set(c).at[b].set(c_ctx)
    mod_all = _ada_all(cond, w_ada, b_ada)
    cos, sin = _rope_tables(s_len, c_len)
    w_out_bf = w_out.astype(BF16)
    mods = [mod_all[l].reshape(8, 1, N_MOD * d) for l in range(depth)]
    n_dispatch = b * n * TOP_K + exp_w_gate.shape[1] * MOE_BLOCK

    xc, h1 = _join_norm_call(x, ctx, g_norm1[0], mods[0])
    h1 = h1.reshape(b * n, d)
    for l in range(depth):
        mod3 = mods[l]
        pxa = _matmul_cols(h1, w_in, l, 0, ATT_COLS, BF16, 1024).reshape(b, n, ATT_COLS)
        pxb = _matmul_cols(h1, w_in, l, ATT_COLS, REST_COLS, F32, 768).reshape(b, n, REST_COLS)
        qt, kr, vt = _rope_call(pxa, cos, sin)
        mix_a = _na_call(pxa, na_rpb[l], s_len, c_len)
        mix_b, (wg, wu, wd), xs0 = _da_call(qt, kr, vt, da_lambda[l], da_subln_g[l],
                                            (exp_w_gate, exp_w_up, exp_w_down), n_dispatch, s_len, c_len, l)
        mix_c = _pool_call(pxb, pool_w[l], pool_scale[l], s_len, c_len)
        mix_d = _sg_call(pxb, sg_norm_g[l], sg_w[l], sg_b[l])
        x1, h2, ri, rf, cnt = _out_router_call((mix_a, mix_b, mix_c, mix_d), w_out_bf, l, xc, g_norm2[l], mod3,
                                               router_w, router_b, s_len)
        next_norm = (g_norm1[l + 1], mods[l + 1]) if l + 1 < depth else None
        xc, h1 = _moe(x1, h2, ri, rf, cnt, xs0, mod3, wg, wu, wd, s_len, next_norm, g_final)
    return xc
```
